```python
import jax, jax.numpy as jnp
from jax import lax
import numpy as np


D_MODEL = 1024
BATCH = 2
SEQ = 16384
DEPTH = 4
DEC_BATCH = 2
DEC_SEQ = 8192
PAST_LEN = 128

GRID_W = 64
N_EVEN = (DEPTH + 1) // 2
N_ODD = DEPTH // 2
EPS = 1e-6
NEG = -1e30
A_WIDTH = D_MODEL // 2
A_HEADS = 4
A_DK = A_WIDTH // A_HEADS
CHUNK = 64
NA_WIDTH = D_MODEL // 2
NA_HEADS = 8
NA_HD = NA_WIDTH // NA_HEADS
NA_KR_MAX = 8
NA_KC = 16
NA_QB = 16
NA_BAND = 32
NA_NCB = GRID_W // NA_QB
EVEN_IN = 5 * A_WIDTH + 3 * NA_WIDTH
CONV_W = 3
D_FF = -(-8 * D_MODEL // (3 * 256)) * 256

kernel_name = 'hybrid_hgrn2_natten_shortconv_encoder'


def _rmsnorm(x, g):
    x32 = x.astype(jnp.float32)
    y = x32 * lax.rsqrt(jnp.mean(x32 * x32, axis=-1, keepdims=True) + EPS) * g.astype(jnp.float32)
    return y.astype(x.dtype)


def _hgrn2_direction(q, k, v, logf):
    bsz, h, t, dk = q.shape
    dv = v.shape[-1]
    nc = t // CHUNK

    def chunks(a):
        return jnp.moveaxis(a.reshape(bsz, h, nc, CHUNK, a.shape[-1]), 2, 0)

    causal = jnp.tril(jnp.ones((CHUNK, CHUNK), dtype=bool))[:, :, None]

    def step(state, inp):
        qc, kc, vc, lf = inp
        b = jnp.cumsum(lf, axis=2)
        o_inter = jnp.einsum('bhti,bhij->bhtj', qc * jnp.exp(b), state)
        rel = b[:, :, :, None, :] - b[:, :, None, :, :]
        decay = jnp.exp(jnp.where(causal, rel, NEG))
        scores = jnp.einsum('bhti,bhtsi,bhsi->bhts', qc, decay, kc)
        o = o_inter + jnp.einsum('bhts,bhsj->bhtj', scores, vc)
        b_last = b[:, :, -1:, :]
        state = (jnp.exp(b_last[:, :, 0, :, None]) * state
                 + jnp.einsum('bhsi,bhsj->bhij', kc * jnp.exp(b_last - b), vc))
        return state, o

    s0 = jnp.zeros((bsz, h, dk, dv), jnp.float32)
    _, o = lax.scan(step, s0, (chunks(q), chunks(k), chunks(v), chunks(logf)))
    return jnp.moveaxis(o, 0, 2).reshape(bsz, h, t, dv)


def _hgrn2_mixer(q_a, z_f, z_b, i_a, g_a, lb, o_gain):
    bsz, t, _ = q_a.shape

    def heads(a):
        return a.astype(jnp.float32).reshape(bsz, t, A_HEADS, A_DK).transpose(0, 2, 1, 3)

    qh = heads(jax.nn.silu(q_a))
    ih = heads(i_a)
    lb32 = lb.astype(jnp.float32)

    def gate(z, lbd):
        z = z.astype(jnp.float32)
        f = lbd + (1.0 - lbd) * jax.nn.sigmoid(z)
        k = (1.0 - lbd) * jax.nn.sigmoid(-z)
        return heads(k), heads(jnp.log(f))

    k_f, lf_f = gate(z_f, lb32[0])
    k_b, lf_b = gate(z_b, lb32[1])
    flip = lambda a: jnp.flip(a, axis=2)
    o = (_hgrn2_direction(qh, k_f, ih, lf_f)
         + flip(_hgrn2_direction(flip(qh), flip(k_b), flip(ih), flip(lf_b))))
    o = o * lax.rsqrt(jnp.mean(o * o, axis=-1, keepdims=True) + EPS)
    o = o.transpose(0, 2, 1, 3).reshape(bsz, t, A_WIDTH) * o_gain.astype(jnp.float32)
    return (o * jax.nn.silu(g_a.astype(jnp.float32))).astype(q_a.dtype)


def _neighbourhood_attention(q, k, v, rpb):
    bsz, t, h, hd = q.shape
    rows = t // GRID_W
    kr = min(NA_KR_MAX, rows)
    r = jnp.arange(rows)
    row_start = jnp.clip(r - kr // 2, 0, rows - kr)
    key_rows = row_start[:, None] + jnp.arange(kr)
    j = jnp.arange(NA_NCB)
    band_start = jnp.clip(j * NA_QB - NA_KC // 2, 0, GRID_W - NA_BAND)
    key_cols = band_start[:, None] + jnp.arange(NA_BAND)
    qcol = j[:, None] * NA_QB + jnp.arange(NA_QB)
    win_start = jnp.clip(qcol - NA_KC // 2, 0, GRID_W - NA_KC)
    col_mask = ((key_cols[:, None, :] >= win_start[..., None])
                & (key_cols[:, None, :] < win_start[..., None] + NA_KC))
    tok = key_rows[:, None, :, None] * GRID_W + key_cols[None, :, None, :]
    kg = k[:, tok]
    vg = v[:, tok]
    qb = q.reshape(bsz, rows, NA_NCB, NA_QB, h, hd)
    s = jnp.einsum('brjqhd,brjkwhd->bhrjqkw', qb, kg).astype(jnp.float32) * (hd ** -0.5)
    row_off = key_rows - r[:, None] + NA_KR_MAX - 1
    col_off = jnp.clip(key_cols[:, None, :] - qcol[..., None], -(NA_KC - 1), NA_KC - 1) + NA_KC - 1
    bias = rpb.astype(jnp.float32)[:, row_off[:, None, None, :, None], col_off[None, :, :, None, :]]
    s = s + bias[None]
    s = jnp.where(col_mask[:, :, None, :], s, NEG)
    p = jax.nn.softmax(s.reshape(s.shape[:5] + (kr * NA_BAND,)), axis=-1).reshape(s.shape)
    o = jnp.einsum('bhrjqkw,brjkwhd->brjqhd', p, vg.astype(jnp.float32))
    return o.reshape(bsz, t, h * hd).astype(q.dtype)


def _even_mixer(h, w_in, w_out, lb, o_gain, rpb):
    bsz, t, _ = h.shape
    proj = h @ w_in
    splits = [A_WIDTH * n for n in range(1, 6)] + [5 * A_WIDTH + NA_WIDTH, 5 * A_WIDTH + 2 * NA_WIDTH]
    q_a, z_f, z_b, i_a, g_a, q_n, k_n, v_n = jnp.split(proj, splits, axis=-1)
    out_a = _hgrn2_mixer(q_a, z_f, z_b, i_a, g_a, lb, o_gain)
    hs = lambda a: a.reshape(bsz, t, NA_HEADS, NA_HD)
    out_n = _neighbourhood_attention(hs(q_n), hs(k_n), hs(v_n), rpb)
    return jnp.concatenate([out_a, out_n], axis=-1) @ w_out


def _conv_mixer(h, w_in, conv_w, w_out):
    b_g, c_g, u = jnp.split(h @ w_in, 3, axis=-1)
    z = c_g * u
    t = z.shape[1]
    pad = CONV_W // 2
    zp = jnp.pad(z, ((0, 0), (pad, pad), (0, 0)))
    conv = sum(zp[:, tap:tap + t] * conv_w[tap] for tap in range(CONV_W))
    return (b_g * conv) @ w_out


def _swiglu(h, w_in, w_out):
    gate, up = jnp.split(h @ w_in, 2, axis=-1)
    return (jax.nn.silu(gate) * up) @ w_out


def _trunk(x, norm_mix, norm_ffn, norm_final, even_w_in, even_w_out, hgrn_lb_logits, hgrn_norm,
           na_rpb, conv_w_in, conv_w, conv_w_out, ffn_w_in, ffn_w_out):
    p = jax.nn.softmax(hgrn_lb_logits.astype(jnp.float32), axis=0)
    lbs = jnp.cumsum(p, axis=0) - p[0]
    for l in range(DEPTH):
        h = _rmsnorm(x, norm_mix[l])
        if l % 2 == 0:
            e = l // 2
            x = x + _even_mixer(h, even_w_in[e], even_w_out[e], lbs[e], hgrn_norm[e], na_rpb[e])
        else:
            o = l // 2
            x = x + _conv_mixer(h, conv_w_in[o], conv_w[o], conv_w_out[o])
        x = x + _swiglu(_rmsnorm(x, norm_ffn[l]), ffn_w_in[l], ffn_w_out[l])
    return _rmsnorm(x, norm_final)


def setup_inputs(seed: int = 0) -> dict:
    key = jax.random.key(seed)
    ks = jax.random.split(key, 15)
    nrm = lambda k, shape, s: jax.random.normal(k, shape, jnp.float32) * s
    return {
        'x_prompt': nrm(ks[0], (BATCH, SEQ, D_MODEL), 1.0),
        'x_sample': nrm(ks[1], (DEC_BATCH, DEC_SEQ, D_MODEL), 1.0),
        'norm_mix': 1.0 + nrm(ks[2], (DEPTH, D_MODEL), 0.02),
        'norm_ffn': 1.0 + nrm(ks[3], (DEPTH, D_MODEL), 0.02),
        'norm_final': 1.0 + nrm(ks[4], (D_MODEL,), 0.02),
        'even_w_in': nrm(ks[5], (N_EVEN, D_MODEL, EVEN_IN), D_MODEL ** -0.5),
        'even_w_out': nrm(ks[6], (N_EVEN, A_WIDTH + NA_WIDTH, D_MODEL), (A_WIDTH + NA_WIDTH) ** -0.5),
        'hgrn_lb_logits': nrm(ks[7], (N_EVEN, 2, A_WIDTH), 0.5),
        'hgrn_norm': 1.0 + nrm(ks[8], (N_EVEN, A_WIDTH), 0.02),
        'na_rpb': nrm(ks[9], (N_EVEN, NA_HEADS, 2 * NA_KR_MAX - 1, 2 * NA_KC - 1), 0.1),
        'conv_w_in': nrm(ks[10], (N_ODD, D_MODEL, 3 * D_MODEL), D_MODEL ** -0.5),
        'conv_w': nrm(ks[11], (N_ODD, CONV_W, D_MODEL), CONV_W ** -0.5),
        'conv_w_out': nrm(ks[12], (N_ODD, D_MODEL, D_MODEL), D_MODEL ** -0.5),
        'ffn_w_in': nrm(ks[13], (DEPTH, D_MODEL, 2 * D_FF), D_MODEL ** -0.5),
        'ffn_w_out': nrm(ks[14], (DEPTH, D_FF, D_MODEL), D_FF ** -0.5),
    }


def reference(x_prompt, x_sample, norm_mix, norm_ffn, norm_final, even_w_in, even_w_out,
              hgrn_lb_logits, hgrn_norm, na_rpb, conv_w_in, conv_w, conv_w_out, ffn_w_in, ffn_w_out):
    y_prompt = _trunk(x_prompt, norm_mix, norm_ffn, norm_final, even_w_in, even_w_out, hgrn_lb_logits,
                      hgrn_norm, na_rpb, conv_w_in, conv_w, conv_w_out, ffn_w_in, ffn_w_out)
    y_sample = _trunk(x_sample, norm_mix, norm_ffn, norm_final, even_w_in, even_w_out, hgrn_lb_logits,
                      hgrn_norm, na_rpb, conv_w_in, conv_w, conv_w_out, ffn_w_in, ffn_w_out)
    return (y_prompt, y_sample)
```

```python
import functools

import numpy as np
import jax
import jax.numpy as jnp
from jax import lax
from jax.experimental import pallas as pl
from jax.experimental.pallas import tpu as pltpu

F32 = jnp.float32
BF16 = jnp.bfloat16

D_MODEL = 1024
EPS = 1e-6
NEG = -1e30
GRID_W = 64
A_WIDTH = 512
A_HEADS = 4
A_DK = 128
CHUNK = 64
DIAG = 8
LEVELS = (16, 32, 64)
NA_WIDTH = 512
NA_HEADS = 8
NA_HD = 64
NA_KR = 8
NA_KC = 16
CONV_W = 3
D_FF = 2816
FF_CHUNK = 256
PA_W = 5 * A_WIDTH
PN_W = 3 * NA_WIDTH
EVEN_IN = PA_W + PN_W

TM = 512
HGRN_BLOCK = 256
NA_ROWS = 8
HALO = 8
VMEM_LIMIT = 56 * 1024 * 1024

NT_DIMS = (((1,), (1,)), ((), ()))
TN_DIMS = (((0,), (0,)), ((), ()))


def _resident(shape):
    nd = len(shape)
    return pl.BlockSpec(shape, lambda *_: (0,) * nd, pipeline_mode=pl.Buffered(1))


def _rms(x, g):
    return x * lax.rsqrt(jnp.mean(x * x, axis=-1, keepdims=True) + EPS) * g


def _sigmoid(x):
    return jax.nn.sigmoid(x)


def _dot(a, b):
    return jnp.dot(a, b, preferred_element_type=F32)


def _in_even_body(x_ref, g_ref, w_ref, pa_ref, pn_ref):
    h = _rms(x_ref[...], g_ref[...]).astype(BF16)
    for c in range(0, PA_W, 512):
        pa_ref[:, c:c + 512] = _dot(h, w_ref[:, c:c + 512])
    for c in range(0, PN_W, 512):
        pn_ref[:, c:c + 512] = _dot(h, w_ref[:, PA_W + c:PA_W + c + 512]).astype(BF16)


def _in_even(x, g, w):
    t = x.shape[0]
    return pl.pallas_call(
        _in_even_body,
        grid=(t // TM,),
        in_specs=[pl.BlockSpec((TM, D_MODEL), lambda i: (i, 0)),
                  _resident((1, D_MODEL)),
                  _resident((D_MODEL, EVEN_IN))],
        out_specs=[pl.BlockSpec((TM, PA_W), lambda i: (i, 0)),
                   pl.BlockSpec((TM, PN_W), lambda i: (i, 0))],
        out_shape=[jax.ShapeDtypeStruct((t, PA_W), F32),
                   jax.ShapeDtypeStruct((t, PN_W), BF16)],
        compiler_params=pltpu.CompilerParams(dimension_semantics=("arbitrary",),
                                             vmem_limit_bytes=VMEM_LIMIT),
        name="in_even",
    )(x, g, w)


def _level_masks(rev):
    t = lax.broadcasted_iota(jnp.int32, (CHUNK, CHUNK), 0)
    s = lax.broadcasted_iota(jnp.int32, (CHUNK, CHUNK), 1)
    masks = []
    for L in LEVELS:
        half = L // 2
        same = (t // L) == (s // L)
        if rev:
            m = same & ((t % L) < half) & ((s % L) >= half)
        else:
            m = same & ((t % L) >= half) & ((s % L) < half)
        masks.append(m)
    return masks


def _diag_masks(rev):
    t = lax.broadcasted_iota(jnp.int32, (CHUNK, 1), 0) % DIAG
    return [((t + j) < DIAG) if rev else (t >= j) for j in range(DIAG)]


def _tri(rev):
    t = lax.broadcasted_iota(jnp.int32, (CHUNK, CHUNK), 0)
    s = lax.broadcasted_iota(jnp.int32, (CHUNK, CHUNK), 1)
    return jnp.where((s >= t) if rev else (s <= t), 1.0, 0.0).astype(BF16)


def _hgrn_unit(q, k, v, b, s_t, rev, lmasks, dmasks):
    q0 = (q * jnp.exp(b)).astype(BF16)
    o = lax.dot_general(q0, s_t.astype(BF16), NT_DIMS, preferred_element_type=F32)
    a = None
    for li, L in enumerate(LEVELS):
        half = L // 2
        refs = []
        for blk in range(CHUNK // L):
            r = blk * L + (half if rev else half - 1)
            refs.append(jnp.broadcast_to(b[r:r + 1, :], (L, A_DK)))
        ref = refs[0] if len(refs) == 1 else jnp.concatenate(refs, axis=0)
        ql = (q * jnp.exp(jnp.minimum(b - ref, 0.0))).astype(BF16)
        kl = (k * jnp.exp(jnp.minimum(ref - b, 0.0))).astype(BF16)
        sl = lax.dot_general(ql, kl, NT_DIMS, preferred_element_type=F32)
        a = jnp.where(lmasks[li], sl, 0.0 if a is None else a)
    o = o + _dot(a.astype(BF16), v.astype(BF16))
    o = o + jnp.sum(q * k, axis=-1, keepdims=True) * v
    for j in range(1, DIAG):
        sh = (CHUNK - j) if rev else j
        kj = pltpu.roll(k, sh, 0)
        bj = pltpu.roll(b, sh, 0)
        vj = pltpu.roll(v, sh, 0)
        p = q * kj * jnp.exp(jnp.minimum(b - bj, 0.0))
        aj = jnp.sum(p, axis=-1, keepdims=True)
        o = o + jnp.where(dmasks[j], aj, 0.0) * vj
    e = b[0:1, :] if rev else b[CHUNK - 1:CHUNK, :]
    k2 = (k * jnp.exp(e - b)).astype(BF16)
    ds = lax.dot_general(v.astype(BF16), k2, TN_DIMS, preferred_element_type=F32)
    return o, jnp.exp(e) * s_t + ds


def _hgrn_body(layer, tbl_ref, qf_ref, zf_ref, vf_ref, qb_ref, zb_ref, vb_ref, lbl_ref,
               of_ref, ob_ref, s_ref):
    step = pl.program_id(0)

    @pl.when(tbl_ref[2, step] == 1)
    def _():
        s_ref[...] = jnp.zeros_like(s_ref)

    n_even = lbl_ref.shape[0]
    logits = [lbl_ref[i] for i in range(n_even)]
    mx = functools.reduce(jnp.maximum, logits)
    ex = [jnp.exp(l - mx) for l in logits]
    den = functools.reduce(lambda x, y: x + y, ex)
    ps = [e / den for e in ex]
    lbs = functools.reduce(lambda x, y: x + y, ps[:layer + 1]) - ps[0]

    nc = HGRN_BLOCK // CHUNK
    refs = ((qf_ref, zf_ref, vf_ref, of_ref), (qb_ref, zb_ref, vb_ref, ob_ref))
    consts = [(_tri(rev), _level_masks(rev), _diag_masks(rev)) for rev in (False, True)]

    def chunk(c, carry):
        for d, rev in enumerate((False, True)):
            q_ref, z_ref, v_ref, o_ref = refs[d]
            tri, lmasks, dmasks = consts[d]
            r0 = pl.multiple_of(((nc - 1 - c) if rev else c) * CHUNK, CHUNK)
            rows = pl.ds(r0, CHUNK)
            qa = q_ref[rows, :]
            z = z_ref[rows, :]
            v = v_ref[rows, :]
            lb = lbs[d:d + 1, :]
            q = qa * _sigmoid(qa)
            f = lb + (1.0 - lb) * _sigmoid(z)
            k = (1.0 - lb) * _sigmoid(-z)
            lf = jnp.log(f)
            hi = lf.astype(BF16)
            r1 = lf - hi.astype(F32)
            mid = r1.astype(BF16)
            lo = (r1 - mid.astype(F32)).astype(BF16)
            cs = _dot(tri, jnp.concatenate([hi, mid, lo], axis=1))
            b = cs[:, :A_WIDTH] + cs[:, A_WIDTH:2 * A_WIDTH] + cs[:, 2 * A_WIDTH:]
            for h in range(A_HEADS):
                sl = slice(h * A_DK, (h + 1) * A_DK)
                o, s_new = _hgrn_unit(q[:, sl], k[:, sl], v[:, sl], b[:, sl], s_ref[d, h],
                                      rev, lmasks, dmasks)
                s_ref[d, h] = s_new
                o_ref[rows, sl] = o
        return carry

    lax.fori_loop(0, nc, chunk, 0)


def _hgrn_table(segs):
    rows = []
    for off, batch, seq in segs:
        nblk = seq // HGRN_BLOCK
        for bi in range(batch):
            base = (off + bi * seq) // HGRN_BLOCK
            for i in range(nblk):
                rows.append((base + i, base + nblk - 1 - i, int(i == 0)))
    return np.asarray(rows, np.int32).T.copy()


def _hgrn(pa, lb_logits, layer, segs):
    t = pa.shape[0]
    tbl = _hgrn_table(segs)
    steps = tbl.shape[1]

    def spec(col, which):
        return pl.BlockSpec((HGRN_BLOCK, A_WIDTH), lambda i, tb: (tb[which, i], col))

    grid_spec = pltpu.PrefetchScalarGridSpec(
        num_scalar_prefetch=1,
        grid=(steps,),
        in_specs=[spec(0, 0), spec(1, 0), spec(3, 0), spec(0, 1), spec(2, 1), spec(3, 1),
                  pl.BlockSpec(lb_logits.shape, lambda i, tb: (0, 0, 0))],
        out_specs=[pl.BlockSpec((HGRN_BLOCK, A_WIDTH), lambda i, tb: (tb[0, i], 0)),
                   pl.BlockSpec((HGRN_BLOCK, A_WIDTH), lambda i, tb: (tb[1, i], 0))],
        scratch_shapes=[pltpu.VMEM((2, A_HEADS, A_DK, A_DK), F32)],
    )
    return pl.pallas_call(
        functools.partial(_hgrn_body, layer),
        grid_spec=grid_spec,
        out_shape=[jax.ShapeDtypeStruct((t, A_WIDTH), F32),
                   jax.ShapeDtypeStruct((t, A_WIDTH), F32)],
        compiler_params=pltpu.CompilerParams(dimension_semantics=("arbitrary",),
                                             vmem_limit_bytes=VMEM_LIMIT),
        name="hgrn",
    )(jnp.asarray(tbl), pa, pa, pa, pa, pa, pa, lb_logits)


def _na_bias_table(rpb):
    qc = np.arange(GRID_W)[:, None]
    kc = np.arange(GRID_W)[None, :]
    col_off = np.clip(kc - qc, -(NA_KC - 1), NA_KC - 1) + NA_KC - 1
    ws = np.clip(qc - NA_KC // 2, 0, GRID_W - NA_KC)
    mask = (kc >= ws) & (kc < ws + NA_KC)
    dd = np.arange(NA_KR)[:, None]
    ki = np.arange(NA_KR)[None, :]
    row_off = dd + ki
    g = rpb.astype(F32)[:, row_off[:, None, :, None], col_off[None, :, None, :]]
    g = jnp.where(jnp.asarray(mask)[None, None, :, None, :], g, NEG)
    return g.reshape(NA_HEADS, NA_KR, GRID_W, NA_KR * GRID_W)


def _na_body(tbl_ref, q_ref, kp_ref, kc_ref, kn_ref, vp_ref, vc_ref, vn_ref, bias_ref,
             o_ref, kbuf, vbuf):
    step = pl.program_id(0)
    blk = NA_ROWS * GRID_W
    kbuf[0:blk, :] = kp_ref[...]
    kbuf[blk:2 * blk, :] = kc_ref[...]
    kbuf[2 * blk:3 * blk, :] = kn_ref[...]
    vbuf[0:blk, :] = vp_ref[...]
    vbuf[blk:2 * blk, :] = vc_ref[...]
    vbuf[2 * blk:3 * blk, :] = vn_ref[...]
    r0 = tbl_ref[4, step]
    n_rows = tbl_ref[5, step]
    lane = lax.broadcasted_iota(jnp.int32, (GRID_W, 2 * NA_HD), 1)
    low = lane < NA_HD
    scale = NA_HD ** -0.5
    nkeys = NA_KR * GRID_W

    def row(rr, carry):
        r = r0 + rr
        row_start = jnp.clip(r - NA_KR // 2, 0, n_rows - NA_KR)
        dd = row_start - r + (NA_KR - 1)
        k0 = pl.multiple_of((row_start - r0 + NA_ROWS) * GRID_W, GRID_W)
        q0 = pl.multiple_of(rr * GRID_W, GRID_W)
        for hp in range(NA_HEADS // 2):
            lanes = slice(hp * 2 * NA_HD, (hp + 1) * 2 * NA_HD)
            qp = q_ref[pl.ds(q0, GRID_W), lanes]
            kw = kbuf[pl.ds(k0, nkeys), lanes]
            vw = vbuf[pl.ds(k0, nkeys), lanes]
            zero = jnp.zeros_like(qp)
            q2 = jnp.concatenate([jnp.where(low, qp, zero), jnp.where(low, zero, qp)], axis=0)
            s = lax.dot_general(q2, kw, NT_DIMS, preferred_element_type=F32) * scale
            bias = jnp.concatenate([bias_ref[2 * hp, dd], bias_ref[2 * hp + 1, dd]], axis=0)
            s = s + bias
            m = jnp.max(s, axis=-1, keepdims=True)
            p = jnp.exp(s - m)
            l = jnp.sum(p, axis=-1, keepdims=True)
            pv = _dot(p.astype(BF16), vw) / l
            o = jnp.where(low, pv[:GRID_W], pv[GRID_W:])
            o_ref[pl.ds(q0, GRID_W), lanes] = o.astype(o_ref.dtype)
        return carry

    lax.fori_loop(0, NA_ROWS, row, 0)


def _na_table(segs):
    blk = NA_ROWS * GRID_W
    rows = []
    for off, batch, seq in segs:
        nblk = seq // blk
        for bi in range(batch):
            base = (off + bi * seq) // blk
            for i in range(nblk):
                rows.append((base + i, base + max(i - 1, 0), base + min(i + 1, nblk - 1), 0,
                             i * NA_ROWS, seq // GRID_W))
    return np.asarray(rows, np.int32).T.copy()


def _na(pn, rpb, segs):
    t = pn.shape[0]
    tbl = _na_table(segs)
    steps = tbl.shape[1]
    blk = NA_ROWS * GRID_W
    bias = _na_bias_table(rpb)

    def spec(col, which):
        return pl.BlockSpec((blk, NA_WIDTH), lambda i, tb: (tb[which, i], col))

    grid_spec = pltpu.PrefetchScalarGridSpec(
        num_scalar_prefetch=1,
        grid=(steps,),
        in_specs=[spec(0, 0), spec(1, 1), spec(1, 0), spec(1, 2), spec(2, 1), spec(2, 0), spec(2, 2),
                  pl.BlockSpec(bias.shape, lambda i, tb: (0, 0, 0, 0), pipeline_mode=pl.Buffered(1))],
        out_specs=pl.BlockSpec((blk, NA_WIDTH), lambda i, tb: (tb[0, i], 0)),
        scratch_shapes=[pltpu.VMEM((3 * blk, NA_WIDTH), BF16), pltpu.VMEM((3 * blk, NA_WIDTH), BF16)],
    )
    return pl.pallas_call(
        _na_body,
        grid_spec=grid_spec,
        out_shape=jax.ShapeDtypeStruct((t, NA_WIDTH), BF16),
        compiler_params=pltpu.CompilerParams(dimension_semantics=("arbitrary",),
                                             vmem_limit_bytes=VMEM_LIMIT),
        name="natten",
    )(jnp.asarray(tbl), pn, pn, pn, pn, pn, pn, pn, bias)


def _ffn_tail(x1, gf_ref, wi_ref, wo_ref, gfin_ref, o_ref, final):
    h = _rms(x1, gf_ref[...]).astype(BF16)
    acc = x1
    for c in range(0, D_FF, FF_CHUNK):
        g = _dot(h, wi_ref[:, c:c + FF_CHUNK])
        u = _dot(h, wi_ref[:, D_FF + c:D_FF + c + FF_CHUNK])
        a = (g * _sigmoid(g) * u).astype(BF16)
        acc = acc + _dot(a, wo_ref[c:c + FF_CHUNK, :])
    if final:
        acc = _rms(acc, gfin_ref[...])
    o_ref[...] = acc


def _post_even_body(final, x_ref, of_ref, ob_ref, ga_ref, n_ref, gain_ref, wm_ref, gf_ref, wi_ref,
                    wo_ref, gfin_ref, o_ref):
    o = of_ref[...] + ob_ref[...]
    parts = []
    for h in range(A_HEADS):
        oh = o[:, h * A_DK:(h + 1) * A_DK]
        parts.append(oh * lax.rsqrt(jnp.mean(oh * oh, axis=-1, keepdims=True) + EPS))
    on = jnp.concatenate(parts, axis=1) * gain_ref[...]
    g = ga_ref[...]
    oa = (on * (g * _sigmoid(g))).astype(BF16)
    mix = _dot(oa, wm_ref[0:A_WIDTH, :]) + _dot(n_ref[...], wm_ref[A_WIDTH:A_WIDTH + NA_WIDTH, :])
    _ffn_tail(x_ref[...] + mix, gf_ref, wi_ref, wo_ref, gfin_ref, o_ref, final)


def _post_even(x, o_f, o_b, pa, o_n, gain, w_mix, g_ffn, w_ffn_in, w_ffn_out, g_final, final):
    t = x.shape[0]
    tok = lambda w, col=0: pl.BlockSpec((TM, w), lambda i: (i, col))
    return pl.pallas_call(
        functools.partial(_post_even_body, final),
        grid=(t // TM,),
        in_specs=[tok(D_MODEL), tok(A_WIDTH), tok(A_WIDTH), tok(A_WIDTH, 4), tok(NA_WIDTH),
                  _resident((1, A_WIDTH)), _resident((A_WIDTH + NA_WIDTH, D_MODEL)),
                  _resident((1, D_MODEL)), _resident((D_MODEL, 2 * D_FF)), _resident((D_FF, D_MODEL)),
                  _resident((1, D_MODEL))],
        out_specs=tok(D_MODEL),
        out_shape=jax.ShapeDtypeStruct((t, D_MODEL), F32),
        compiler_params=pltpu.CompilerParams(dimension_semantics=("arbitrary",),
                                             vmem_limit_bytes=VMEM_LIMIT),
        name="post_even",
    )(x, o_f, o_b, pa, o_n, gain, w_mix, g_ffn, w_ffn_in, w_ffn_out, g_final)


def _odd_body(final, tbl_ref, x_ref, xp_ref, xn_ref, gm_ref, wc_ref, cw_ref, wco_ref, gf_ref, wi_ref,
              wo_ref, gfin_ref, o_ref):
    step = pl.program_id(0)
    x = x_ref[...]
    gm = gm_ref[...]
    h = _rms(x, gm).astype(BF16)
    z = _dot(h, wc_ref[:, D_MODEL:2 * D_MODEL]) * _dot(h, wc_ref[:, 2 * D_MODEL:3 * D_MODEL])
    hh = _rms(jnp.concatenate([xp_ref[...], xn_ref[...]], axis=0), gm).astype(BF16)
    zh = _dot(hh, wc_ref[:, D_MODEL:2 * D_MODEL]) * _dot(hh, wc_ref[:, 2 * D_MODEL:3 * D_MODEL])
    z_prev = jnp.where(tbl_ref[2, step] == 1, 0.0, zh[HALO - 1:HALO, :])
    z_next = jnp.where(tbl_ref[3, step] == 1, 0.0, zh[HALO:HALO + 1, :])
    row = lax.broadcasted_iota(jnp.int32, (TM, 1), 0)
    z_dn = jnp.where(row == 0, z_prev, pltpu.roll(z, 1, 0))
    z_up = jnp.where(row == TM - 1, z_next, pltpu.roll(z, TM - 1, 0))
    conv = z_dn * cw_ref[0:1, :] + z * cw_ref[1:2, :] + z_up * cw_ref[2:3, :]
    y = (_dot(h, wc_ref[:, 0:D_MODEL]) * conv).astype(BF16)
    _ffn_tail(x + _dot(y, wco_ref[...]), gf_ref, wi_ref, wo_ref, gfin_ref, o_ref, final)


def _odd_table(segs, t):
    rows = []
    starts = set()
    ends = set()
    for off, batch, seq in segs:
        for bi in range(batch):
            starts.add(off + bi * seq)
            ends.add(off + (bi + 1) * seq)
    for i in range(t // TM):
        t0 = i * TM
        first = int(t0 in starts)
        last = int(t0 + TM in ends)
        rows.append((max(t0 // HALO - 1, 0), min((t0 + TM) // HALO, t // HALO - 1), first, last))
    return np.asarray(rows, np.int32).T.copy()


def _odd(x, g_mix, w_conv_in, conv_w, w_conv_out, g_ffn, w_ffn_in, w_ffn_out, g_final, final, segs):
    t = x.shape[0]
    tbl = _odd_table(segs, t)
    res = lambda shape: pl.BlockSpec(shape, lambda i, tb: (0,) * len(shape), pipeline_mode=pl.Buffered(1))
    grid_spec = pltpu.PrefetchScalarGridSpec(
        num_scalar_prefetch=1,
        grid=(t // TM,),
        in_specs=[pl.BlockSpec((TM, D_MODEL), lambda i, tb: (i, 0)),
                  pl.BlockSpec((HALO, D_MODEL), lambda i, tb: (tb[0, i], 0)),
                  pl.BlockSpec((HALO, D_MODEL), lambda i, tb: (tb[1, i], 0)),
                  res((1, D_MODEL)), res((D_MODEL, 3 * D_MODEL)), res((CONV_W, D_MODEL)),
                  res((D_MODEL, D_MODEL)), res((1, D_MODEL)), res((D_MODEL, 2 * D_FF)),
                  res((D_FF, D_MODEL)), res((1, D_MODEL))],
        out_specs=pl.BlockSpec((TM, D_MODEL), lambda i, tb: (i, 0)),
    )
    return pl.pallas_call(
        functools.partial(_odd_body, final),
        grid_spec=grid_spec,
        out_shape=jax.ShapeDtypeStruct((t, D_MODEL), F32),
        compiler_params=pltpu.CompilerParams(dimension_semantics=("arbitrary",),
                                             vmem_limit_bytes=VMEM_LIMIT),
        name="odd_layer",
    )(jnp.asarray(tbl), x, x, x, g_mix, w_conv_in, conv_w, w_conv_out, g_ffn, w_ffn_in, w_ffn_out, g_final)


def _trunk(xs, norm_mix, norm_ffn, norm_final, even_w_in, even_w_out, hgrn_lb_logits, hgrn_norm,
           na_rpb, conv_w_in, conv_w, conv_w_out, ffn_w_in, ffn_w_out):
    depth = norm_mix.shape[0]
    segs = []
    off = 0
    for a in xs:
        segs.append((off, a.shape[0], a.shape[1]))
        off += a.shape[0] * a.shape[1]
    x = jnp.concatenate([a.reshape(-1, D_MODEL) for a in xs], axis=0)
    row = lambda v: v.reshape(1, -1).astype(F32)
    g_final = row(norm_final)
    for l in range(depth):
        final = l == depth - 1
        w_fi = ffn_w_in[l].astype(BF16)
        w_fo = ffn_w_out[l].astype(BF16)
        if l % 2 == 0:
            e = l // 2
            pa, pn = _in_even(x, row(norm_mix[l]), even_w_in[e].astype(BF16))
            o_f, o_b = _hgrn(pa, hgrn_lb_logits.astype(F32), e, segs)
            o_n = _na(pn, na_rpb[e], segs)
            x = _post_even(x, o_f, o_b, pa, o_n, row(hgrn_norm[e]), even_w_out[e].astype(BF16),
                           row(norm_ffn[l]), w_fi, w_fo, g_final, final)
        else:
            o = l // 2
            x = _odd(x, row(norm_mix[l]), conv_w_in[o].astype(BF16), conv_w[o].astype(F32),
                     conv_w_out[o].astype(BF16), row(norm_ffn[l]), w_fi, w_fo, g_final, final, segs)
    outs = []
    for (off, batch, seq) in segs:
        outs.append(x[off:off + batch * seq].reshape(batch, seq, D_MODEL))
    return tuple(outs)


def kernel(x_prompt, x_sample, norm_mix, norm_ffn, norm_final, even_w_in, even_w_out, hgrn_lb_logits,
           hgrn_norm, na_rpb, conv_w_in, conv_w, conv_w_out, ffn_w_in, ffn_w_out):
    return _trunk([x_prompt, x_sample], norm_mix, norm_ffn, norm_final, even_w_in, even_w_out,
                  hgrn_lb_logits, hgrn_norm, na_rpb, conv_w_in, conv_w, conv_w_out, ffn_w_in, ffn_w_out)
```

```python
import functools

import numpy as np
import jax
import jax.numpy as jnp
from jax import lax
from jax.experimental import pallas as pl
from jax.experimental.pallas import tpu as pltpu

F32 = jnp.float32
BF16 = jnp.bfloat16

SUBLANES = 8
D_MODEL = 1024
EPS = 1e-6
NEG = -1e30
GRID_W = 64
A_WIDTH = 512
A_HEADS = 4
A_DK = 128
CHUNK = 64
LEVELS = (2, 4, 8, 16, 32, 64)
NA_WIDTH = 512
NA_HEADS = 8
NA_HD = 64
NA_KR = 8
NA_KC = 16
CONV_W = 3
D_FF = 2816
FF_CHUNK = 256
PA_W = 5 * A_WIDTH
PN_W = 3 * NA_WIDTH
EVEN_IN = PA_W + PN_W

TM = 512
HGRN_BLOCK = 256
NA_ROWS = 8
NA_WIN = 3 * NA_ROWS
HALO = 8
VMEM_LIMIT = 56 * 1024 * 1024

NT_DIMS = (((1,), (1,)), ((), ()))
TN_DIMS = (((0,), (0,)), ((), ()))


def _resident(shape):
    nd = len(shape)
    return pl.BlockSpec(shape, lambda *_: (0,) * nd, pipeline_mode=pl.Buffered(1))


def _rms(x, g):
    return x * lax.rsqrt(jnp.mean(x * x, axis=-1, keepdims=True) + EPS) * g


def _sigmoid(x):
    return jax.nn.sigmoid(x)


def _dot(a, b):
    return jnp.dot(a, b, preferred_element_type=F32)


def _dot_nt(a, b):
    return lax.dot_general(a, b, NT_DIMS, preferred_element_type=F32)


def _in_even_body(x_ref, g_ref, w_ref, pa_ref, pn_ref):
    h = _rms(x_ref[...], g_ref[...]).astype(BF16)
    for c in range(0, PA_W, 512):
        pa_ref[:, c:c + 512] = _dot(h, w_ref[:, c:c + 512])
    for c in range(0, PN_W, 512):
        pn_ref[:, c:c + 512] = _dot(h, w_ref[:, PA_W + c:PA_W + c + 512]).astype(BF16)


def _in_even(x, g, w):
    t = x.shape[0]
    return pl.pallas_call(
        _in_even_body,
        grid=(t // TM,),
        in_specs=[pl.BlockSpec((TM, D_MODEL), lambda i: (i, 0)),
                  _resident((1, D_MODEL)),
                  _resident((D_MODEL, EVEN_IN))],
        out_specs=[pl.BlockSpec((TM, PA_W), lambda i: (i, 0)),
                   pl.BlockSpec((TM, PN_W), lambda i: (i, 0))],
        out_shape=[jax.ShapeDtypeStruct((t, PA_W), F32),
                   jax.ShapeDtypeStruct((t, PN_W), BF16)],
        compiler_params=pltpu.CompilerParams(dimension_semantics=("arbitrary",),
                                             vmem_limit_bytes=VMEM_LIMIT),
        name="in_even",
    )(x, g, w)


def _hgrn_masks():
    t = np.arange(CHUNK)[:, None]
    s = np.arange(CHUNK)[None, :]
    out = np.zeros((2, 1 + len(LEVELS), CHUNK, CHUNK), np.float32)
    for d, rev in enumerate((False, True)):
        out[d, 0] = (t == s)
        for li, L in enumerate(LEVELS):
            half = L // 2
            same = (t // L) == (s // L)
            if rev:
                m = same & ((t % L) < half) & ((s % L) >= half)
            else:
                m = same & ((t % L) >= half) & ((s % L) < half)
            out[d, 1 + li] = m
    return np.tile(out, (1, 1, 1, A_HEADS))


def _tri(rev):
    t = lax.broadcasted_iota(jnp.int32, (CHUNK, CHUNK), 0)
    s = lax.broadcasted_iota(jnp.int32, (CHUNK, CHUNK), 1)
    return jnp.where((s >= t) if rev else (s <= t), 1.0, 0.0).astype(BF16)


def _level_ref(b3, L, rev):
    g, sub, w = b3.shape
    half = L // 2
    if L >= 2 * SUBLANES:
        n = L // SUBLANES
        pieces = []
        for blk in range(CHUNK // L):
            r = blk * L + (half if rev else half - 1)
            pieces.append(jnp.broadcast_to(b3[r // SUBLANES:r // SUBLANES + 1, r % SUBLANES:r % SUBLANES + 1, :],
                                           (n, sub, w)))
        return pieces[0] if len(pieces) == 1 else jnp.concatenate(pieces, axis=0)
    srow = lax.broadcasted_iota(jnp.int32, b3.shape, 1)
    if L == SUBLANES:
        r = half if rev else half - 1
        return jnp.broadcast_to(b3[:, r:r + 1, :], b3.shape)
    if L == 4:
        r = half if rev else half - 1
        lo = jnp.broadcast_to(b3[:, r:r + 1, :], b3.shape)
        hi = jnp.broadcast_to(b3[:, r + 4:r + 5, :], b3.shape)
        return jnp.where(srow < 4, lo, hi)
    odd = (srow % 2) == 1
    if rev:
        return jnp.where(odd, b3, pltpu.roll(b3, SUBLANES - 1, 1))
    return jnp.where(odd, pltpu.roll(b3, 1, 1), b3)


def _head_blockdiag(x):
    zero = jnp.zeros((CHUNK, A_DK), x.dtype)
    rows = []
    for h in range(A_HEADS):
        rows.append(jnp.concatenate([x[:, j * A_DK:(j + 1) * A_DK] if j == h else zero
                                     for j in range(A_HEADS)], axis=1))
    return jnp.concatenate(rows, axis=0)


def _hgrn_chunk(q, k, v, b, s_ref, d, rev, m_ref):
    b3 = b.reshape(CHUNK // SUBLANES, SUBLANES, A_WIDTH)
    a = _dot_nt(q.astype(BF16), _head_blockdiag(k.astype(BF16))) * m_ref[d, 0]
    for li, L in enumerate(LEVELS):
        e = jnp.exp(-jnp.abs(b3 - _level_ref(b3, L, rev))).reshape(CHUNK, A_WIDTH)
        sl = _dot_nt((q * e).astype(BF16), _head_blockdiag((k * e).astype(BF16)))
        a = a + sl * m_ref[d, 1 + li]
    o = _dot(a.astype(BF16), _head_blockdiag(v.astype(BF16)))
    tot = b[0:1, :] if rev else b[CHUNK - 1:CHUNK, :]
    q0 = (q * jnp.exp(b)).astype(BF16)
    k2 = (k * jnp.exp(tot - b)).astype(BF16)
    vb = v.astype(BF16)
    dec = jnp.exp(tot)
    outs = []
    for h in range(A_HEADS):
        sl = slice(h * A_DK, (h + 1) * A_DK)
        s_t = s_ref[d, h]
        outs.append(o[:, sl] + _dot_nt(q0[:, sl], s_t.astype(BF16)))
        ds = lax.dot_general(vb[:, sl], k2[:, sl], TN_DIMS, preferred_element_type=F32)
        s_ref[d, h] = dec[:, sl] * s_t + ds
    return jnp.concatenate(outs, axis=1)


def _hgrn_body(layer, tbl_ref, qf_ref, zf_ref, vf_ref, qb_ref, zb_ref, vb_ref, lbl_ref, m_ref,
               of_ref, ob_ref, s_ref):
    step = pl.program_id(0)

    @pl.when(tbl_ref[2, step] == 1)
    def _():
        s_ref[...] = jnp.zeros_like(s_ref)

    n_even = lbl_ref.shape[0]
    logits = [lbl_ref[i] for i in range(n_even)]
    mx = functools.reduce(jnp.maximum, logits)
    ex = [jnp.exp(l - mx) for l in logits]
    den = functools.reduce(lambda x, y: x + y, ex)
    ps = [e / den for e in ex]
    lbs = functools.reduce(lambda x, y: x + y, ps[:layer + 1]) - ps[0]

    nc = HGRN_BLOCK // CHUNK
    refs = ((qf_ref, zf_ref, vf_ref, of_ref), (qb_ref, zb_ref, vb_ref, ob_ref))
    tris = (_tri(False), _tri(True))

    def chunk(c, carry):
        for d, rev in enumerate((False, True)):
            q_ref, z_ref, v_ref, o_ref = refs[d]
            r0 = pl.multiple_of(((nc - 1 - c) if rev else c) * CHUNK, CHUNK)
            rows = pl.ds(r0, CHUNK)
            qa = q_ref[rows, :]
            z = z_ref[rows, :]
            lb = lbs[d:d + 1, :]
            q = qa * _sigmoid(qa)
            f = lb + (1.0 - lb) * _sigmoid(z)
            k = (1.0 - lb) * _sigmoid(-z)
            lf = jnp.log(f)
            hi = lf.astype(BF16)
            r1 = lf - hi.astype(F32)
            mid = r1.astype(BF16)
            lo = (r1 - mid.astype(F32)).astype(BF16)
            cs = _dot(tris[d], jnp.concatenate([hi, mid, lo], axis=1))
            b = cs[:, :A_WIDTH] + cs[:, A_WIDTH:2 * A_WIDTH] + cs[:, 2 * A_WIDTH:]
            o_ref[rows, :] = _hgrn_chunk(q, k, v_ref[rows, :], b, s_ref, d, rev, m_ref)
        return carry

    lax.fori_loop(0, nc, chunk, 0)


def _hgrn_table(segs):
    rows = []
    for off, batch, seq in segs:
        nblk = seq // HGRN_BLOCK
        for bi in range(batch):
            base = (off + bi * seq) // HGRN_BLOCK
            for i in range(nblk):
                rows.append((base + i, base + nblk - 1 - i, int(i == 0)))
    return np.asarray(rows, np.int32).T.copy()


def _hgrn(pa, lb_logits, layer, segs):
    t = pa.shape[0]
    tbl = _hgrn_table(segs)
    steps = tbl.shape[1]
    masks = _hgrn_masks()

    def spec(col, which):
        return pl.BlockSpec((HGRN_BLOCK, A_WIDTH), lambda i, tb: (tb[which, i], col))

    grid_spec = pltpu.PrefetchScalarGridSpec(
        num_scalar_prefetch=1,
        grid=(steps,),
        in_specs=[spec(0, 0), spec(1, 0), spec(3, 0), spec(0, 1), spec(2, 1), spec(3, 1),
                  pl.BlockSpec(lb_logits.shape, lambda i, tb: (0, 0, 0)),
                  pl.BlockSpec(masks.shape, lambda i, tb: (0, 0, 0, 0))],
        out_specs=[pl.BlockSpec((HGRN_BLOCK, A_WIDTH), lambda i, tb: (tb[0, i], 0)),
                   pl.BlockSpec((HGRN_BLOCK, A_WIDTH), lambda i, tb: (tb[1, i], 0))],
        scratch_shapes=[pltpu.VMEM((2, A_HEADS, A_DK, A_DK), F32)],
    )
    return pl.pallas_call(
        functools.partial(_hgrn_body, layer),
        grid_spec=grid_spec,
        out_shape=[jax.ShapeDtypeStruct((t, A_WIDTH), F32),
                   jax.ShapeDtypeStruct((t, A_WIDTH), F32)],
        compiler_params=pltpu.CompilerParams(dimension_semantics=("arbitrary",),
                                             vmem_limit_bytes=VMEM_LIMIT),
        name="hgrn",
    )(jnp.asarray(tbl), pa, pa, pa, pa, pa, pa, lb_logits, jnp.asarray(masks))


def _na_build_bias(rpb_ref, bias_ref):
    n_ro = 2 * NA_KR - 1
    n_co = 2 * NA_KC - 1
    qc = lax.broadcasted_iota(jnp.int32, (GRID_W, GRID_W), 0)
    kc = lax.broadcasted_iota(jnp.int32, (GRID_W, GRID_W), 1)
    co = jnp.clip(kc - qc, -(NA_KC - 1), NA_KC - 1) + NA_KC - 1
    ws = jnp.clip(qc - NA_KC // 2, 0, GRID_W - NA_KC)
    valid = (kc >= ws) & (kc < ws + NA_KC)

    def build(hr, carry):
        h = hr // n_ro
        ro = hr % n_ro
        t = jnp.full((GRID_W, GRID_W), NEG, F32)
        for j in range(n_co):
            t = jnp.where(co == j, rpb_ref[hr * n_co + j], t)
        t = jnp.where(valid, t, NEG)
        for i in range(NA_KR):
            dd = ro - i

            @pl.when((dd >= 0) & (dd < NA_KR))
            def _():
                bias_ref[h, dd, :, i * GRID_W:(i + 1) * GRID_W] = t
        return carry

    lax.fori_loop(0, NA_HEADS * n_ro, build, 0)


def _na_body(tbl_ref, rpb_ref, q_ref, k_ref, v_ref, o_ref, bias_ref):
    step = pl.program_id(0)

    @pl.when(step == 0)
    def _():
        _na_build_bias(rpb_ref, bias_ref)

    r0 = tbl_ref[2, step]
    w0 = tbl_ref[3, step]
    n_rows = tbl_ref[4, step]
    lane = lax.broadcasted_iota(jnp.int32, (GRID_W, 2 * NA_HD), 1)
    low = lane < NA_HD
    nkeys = NA_KR * GRID_W
    scale = jnp.asarray(NA_HD ** -0.5, BF16)

    def row(rr, carry):
        r = r0 + rr
        row_start = jnp.clip(r - NA_KR // 2, 0, n_rows - NA_KR)
        dd = row_start - r + (NA_KR - 1)
        k0 = pl.multiple_of((row_start - w0) * GRID_W, GRID_W)
        q0 = pl.multiple_of(rr * GRID_W, GRID_W)
        for hp in range(NA_HEADS // 2):
            lanes = slice(hp * 2 * NA_HD, (hp + 1) * 2 * NA_HD)
            qp = q_ref[pl.ds(q0, GRID_W), lanes] * scale
            kw = k_ref[pl.ds(k0, nkeys), lanes]
            vw = v_ref[pl.ds(k0, nkeys), lanes]
            zero = jnp.zeros_like(qp)
            q2 = jnp.concatenate([jnp.where(low, qp, zero), jnp.where(low, zero, qp)], axis=0)
            bias = jnp.concatenate([bias_ref[2 * hp, dd], bias_ref[2 * hp + 1, dd]], axis=0)
            s = _dot_nt(q2, kw) + bias
            m = jnp.max(s, axis=-1, keepdims=True)
            p = jnp.exp(s - m)
            l = jnp.sum(p, axis=-1, keepdims=True)
            pv = _dot(p.astype(BF16), vw) / l
            o = jnp.where(low, pv[:GRID_W], pv[GRID_W:])
            o_ref[pl.ds(q0, GRID_W), lanes] = o.astype(o_ref.dtype)
        return carry

    lax.fori_loop(0, NA_ROWS, row, 0)


def _na_table(segs):
    blk = NA_ROWS * GRID_W
    rows = []
    for off, batch, seq in segs:
        n_rows = seq // GRID_W
        assert n_rows >= NA_WIN
        for bi in range(batch):
            base = off + bi * seq
            for i in range(seq // blk):
                r0 = i * NA_ROWS
                w0 = min(max(r0 - NA_ROWS, 0), n_rows - NA_WIN)
                rows.append((base // blk + i, base // GRID_W + w0, r0, w0, n_rows))
    return np.asarray(rows, np.int32).T.copy()


def _na(pn, rpb, segs):
    t = pn.shape[0]
    tbl = _na_table(segs)
    steps = tbl.shape[1]
    blk = NA_ROWS * GRID_W

    def window(col):
        return pl.BlockSpec((pl.Element(NA_WIN * GRID_W), pl.Element(NA_WIDTH)),
                            lambda i, tb: (tb[1, i] * GRID_W, col * NA_WIDTH))

    grid_spec = pltpu.PrefetchScalarGridSpec(
        num_scalar_prefetch=1,
        grid=(steps,),
        in_specs=[pl.BlockSpec(memory_space=pltpu.SMEM),
                  pl.BlockSpec((blk, NA_WIDTH), lambda i, tb: (tb[0, i], 0)),
                  window(1), window(2)],
        out_specs=pl.BlockSpec((blk, NA_WIDTH), lambda i, tb: (tb[0, i], 0)),
        scratch_shapes=[pltpu.VMEM((NA_HEADS, NA_KR, GRID_W, NA_KR * GRID_W), F32)],
    )
    return pl.pallas_call(
        _na_body,
        grid_spec=grid_spec,
        out_shape=jax.ShapeDtypeStruct((t, NA_WIDTH), BF16),
        compiler_params=pltpu.CompilerParams(dimension_semantics=("arbitrary",),
                                             vmem_limit_bytes=VMEM_LIMIT),
        name="natten",
    )(jnp.asarray(tbl), rpb.astype(F32).reshape(-1), pn, pn, pn)


def _ffn_tail(x1, gf_ref, wi_ref, wo_ref, gfin_ref, o_ref, final):
    h = _rms(x1, gf_ref[...]).astype(BF16)
    acc = x1
    for c in range(0, D_FF, FF_CHUNK):
        g = _dot(h, wi_ref[:, c:c + FF_CHUNK])
        u = _dot(h, wi_ref[:, D_FF + c:D_FF + c + FF_CHUNK])
        a = (g * _sigmoid(g) * u).astype(BF16)
        acc = acc + _dot(a, wo_ref[c:c + FF_CHUNK, :])
    if final:
        acc = _rms(acc, gfin_ref[...])
    o_ref[...] = acc


def _post_even_body(final, x_ref, of_ref, ob_ref, ga_ref, n_ref, gain_ref, wm_ref, gf_ref, wi_ref,
                    wo_ref, gfin_ref, o_ref):
    o = of_ref[...] + ob_ref[...]
    parts = []
    for h in range(A_HEADS):
        oh = o[:, h * A_DK:(h + 1) * A_DK]
        parts.append(oh * lax.rsqrt(jnp.mean(oh * oh, axis=-1, keepdims=True) + EPS))
    on = jnp.concatenate(parts, axis=1) * gain_ref[...]
    g = ga_ref[...]
    oa = (on * (g * _sigmoid(g))).astype(BF16)
    mix = _dot(oa, wm_ref[0:A_WIDTH, :]) + _dot(n_ref[...], wm_ref[A_WIDTH:A_WIDTH + NA_WIDTH, :])
    _ffn_tail(x_ref[...] + mix, gf_ref, wi_ref, wo_ref, gfin_ref, o_ref, final)


def _post_even(x, o_f, o_b, pa, o_n, gain, w_mix, g_ffn, w_ffn_in, w_ffn_out, g_final, final):
    t = x.shape[0]
    tok = lambda w, col=0: pl.BlockSpec((TM, w), lambda i: (i, col))
    return pl.pallas_call(
        functools.partial(_post_even_body, final),
        grid=(t // TM,),
        in_specs=[tok(D_MODEL), tok(A_WIDTH), tok(A_WIDTH), tok(A_WIDTH, 4), tok(NA_WIDTH),
                  _resident((1, A_WIDTH)), _resident((A_WIDTH + NA_WIDTH, D_MODEL)),
                  _resident((1, D_MODEL)), _resident((D_MODEL, 2 * D_FF)), _resident((D_FF, D_MODEL)),
                  _resident((1, D_MODEL))],
        out_specs=tok(D_MODEL),
        out_shape=jax.ShapeDtypeStruct((t, D_MODEL), F32),
        compiler_params=pltpu.CompilerParams(dimension_semantics=("arbitrary",),
                                             vmem_limit_bytes=VMEM_LIMIT),
        name="post_even",
    )(x, o_f, o_b, pa, o_n, gain, w_mix, g_ffn, w_ffn_in, w_ffn_out, g_final)


def _odd_body(final, tbl_ref, x_ref, xp_ref, xn_ref, gm_ref, wc_ref, cw_ref, wco_ref, gf_ref, wi_ref,
              wo_ref, gfin_ref, o_ref):
    step = pl.program_id(0)
    x = x_ref[...]
    gm = gm_ref[...]
    h = _rms(x, gm).astype(BF16)
    z = _dot(h, wc_ref[:, D_MODEL:2 * D_MODEL]) * _dot(h, wc_ref[:, 2 * D_MODEL:3 * D_MODEL])
    hh = _rms(jnp.concatenate([xp_ref[...], xn_ref[...]], axis=0), gm).astype(BF16)
    zh = _dot(hh, wc_ref[:, D_MODEL:2 * D_MODEL]) * _dot(hh, wc_ref[:, 2 * D_MODEL:3 * D_MODEL])
    z_prev = jnp.where(tbl_ref[2, step] == 1, 0.0, zh[HALO - 1:HALO, :])
    z_next = jnp.where(tbl_ref[3, step] == 1, 0.0, zh[HALO:HALO + 1, :])
    row = lax.broadcasted_iota(jnp.int32, (TM, 1), 0)
    z_dn = jnp.where(row == 0, z_prev, pltpu.roll(z, 1, 0))
    z_up = jnp.where(row == TM - 1, z_next, pltpu.roll(z, TM - 1, 0))
    conv = z_dn * cw_ref[0:1, :] + z * cw_ref[1:2, :] + z_up * cw_ref[2:3, :]
    y = (_dot(h, wc_ref[:, 0:D_MODEL]) * conv).astype(BF16)
    _ffn_tail(x + _dot(y, wco_ref[...]), gf_ref, wi_ref, wo_ref, gfin_ref, o_ref, final)


def _odd_table(segs, t):
    rows = []
    starts = set()
    ends = set()
    for off, batch, seq in segs:
        for bi in range(batch):
            starts.add(off + bi * seq)
            ends.add(off + (bi + 1) * seq)
    for i in range(t // TM):
        t0 = i * TM
        first = int(t0 in starts)
        last = int(t0 + TM in ends)
        rows.append((max(t0 // HALO - 1, 0), min((t0 + TM) // HALO, t // HALO - 1), first, last))
    return np.asarray(rows, np.int32).T.copy()


def _odd(x, g_mix, w_conv_in, conv_w, w_conv_out, g_ffn, w_ffn_in, w_ffn_out, g_final, final, segs):
    t = x.shape[0]
    tbl = _odd_table(segs, t)
    res = lambda shape: pl.BlockSpec(shape, lambda i, tb: (0,) * len(shape), pipeline_mode=pl.Buffered(1))
    grid_spec = pltpu.PrefetchScalarGridSpec(
        num_scalar_prefetch=1,
        grid=(t // TM,),
        in_specs=[pl.BlockSpec((TM, D_MODEL), lambda i, tb: (i, 0)),
                  pl.BlockSpec((HALO, D_MODEL), lambda i, tb: (tb[0, i], 0)),
                  pl.BlockSpec((HALO, D_MODEL), lambda i, tb: (tb[1, i], 0)),
                  res((1, D_MODEL)), res((D_MODEL, 3 * D_MODEL)), res((CONV_W, D_MODEL)),
                  res((D_MODEL, D_MODEL)), res((1, D_MODEL)), res((D_MODEL, 2 * D_FF)),
                  res((D_FF, D_MODEL)), res((1, D_MODEL))],
        out_specs=pl.BlockSpec((TM, D_MODEL), lambda i, tb: (i, 0)),
    )
    return pl.pallas_call(
        functools.partial(_odd_body, final),
        grid_spec=grid_spec,
        out_shape=jax.ShapeDtypeStruct((t, D_MODEL), F32),
        compiler_params=pltpu.CompilerParams(dimension_semantics=("arbitrary",),
                                             vmem_limit_bytes=VMEM_LIMIT),
        name="odd_layer",
    )(jnp.asarray(tbl), x, x, x, g_mix, w_conv_in, conv_w, w_conv_out, g_ffn, w_ffn_in, w_ffn_out, g_final)


def _trunk(xs, norm_mix, norm_ffn, norm_final, even_w_in, even_w_out, hgrn_lb_logits, hgrn_norm,
           na_rpb, conv_w_in, conv_w, conv_w_out, ffn_w_in, ffn_w_out):
    depth = norm_mix.shape[0]
    segs = []
    off = 0
    for a in xs:
        segs.append((off, a.shape[0], a.shape[1]))
        off += a.shape[0] * a.shape[1]
    x = jnp.concatenate([a.reshape(-1, D_MODEL) for a in xs], axis=0)
    row = lambda v: v.reshape(1, -1).astype(F32)
    g_final = row(norm_final)
    for l in range(depth):
        final = l == depth - 1
        w_fi = ffn_w_in[l].astype(BF16)
        w_fo = ffn_w_out[l].astype(BF16)
        if l % 2 == 0:
            e = l // 2
            pa, pn = _in_even(x, row(norm_mix[l]), even_w_in[e].astype(BF16))
            o_f, o_b = _hgrn(pa, hgrn_lb_logits.astype(F32), e, segs)
            o_n = _na(pn, na_rpb[e], segs)
            x = _post_even(x, o_f, o_b, pa, o_n, row(hgrn_norm[e]), even_w_out[e].astype(BF16),
                           row(norm_ffn[l]), w_fi, w_fo, g_final, final)
        else:
            o = l // 2
            x = _odd(x, row(norm_mix[l]), conv_w_in[o].astype(BF16), conv_w[o].astype(F32),
                     conv_w_out[o].astype(BF16), row(norm_ffn[l]), w_fi, w_fo, g_final, final, segs)
    outs = []
    for (off, batch, seq) in segs:
        outs.append(x[off:off + batch * seq].reshape(batch, seq, D_MODEL))
    return tuple(outs)


def kernel(x_prompt, x_sample, norm_mix, norm_ffn, norm_final, even_w_in, even_w_out, hgrn_lb_logits,
           hgrn_norm, na_rpb, conv_w_in, conv_w, conv_w_out, ffn_w_in, ffn_w_out):
    return _trunk([x_prompt, x_sample], norm_mix, norm_ffn, norm_final, even_w_in, even_w_out,
                  hgrn_lb_logits, hgrn_norm, na_rpb, conv_w_in, conv_w, conv_w_out, ffn_w_in, ffn_w_out)
```

```python
import functools

import numpy as np
import jax
import jax.numpy as jnp
from jax import lax
from jax.experimental import pallas as pl
from jax.experimental.pallas import tpu as pltpu

F32 = jnp.float32
BF16 = jnp.bfloat16

SUBLANES = 8
D_MODEL = 1024
EPS = 1e-6
NEG = -1e30
GRID_W = 64
A_WIDTH = 512
A_HEADS = 4
A_DK = 128
CHUNK = 64
LEVELS = (2, 4, 8, 16, 32, 64)
NA_WIDTH = 512
NA_HEADS = 8
NA_HD = 64
NA_KR = 8
NA_KC = 16
CONV_W = 3
D_FF = 2816
FF_CHUNK = 256
PA_W = 5 * A_WIDTH
PN_W = 3 * NA_WIDTH
EVEN_IN = PA_W + PN_W

TM = 512
HGRN_BLOCK = 256
NA_ROWS = 8
NA_WIN = 3 * NA_ROWS
HALO = 8
VMEM_LIMIT = 56 * 1024 * 1024

NT_DIMS = (((1,), (1,)), ((), ()))
TN_DIMS = (((0,), (0,)), ((), ()))


def _resident(shape):
    nd = len(shape)
    return pl.BlockSpec(shape, lambda *_: (0,) * nd, pipeline_mode=pl.Buffered(1))


def _rms(x, g):
    return x * lax.rsqrt(jnp.mean(x * x, axis=-1, keepdims=True) + EPS) * g


def _sigmoid(x):
    return jax.nn.sigmoid(x)


def _dot(a, b):
    return jnp.dot(a, b, preferred_element_type=F32)


def _dot_nt(a, b):
    return lax.dot_general(a, b, NT_DIMS, preferred_element_type=F32)


def _in_even_body(x_ref, g_ref, w_ref, pa_ref, pn_ref):
    h = _rms(x_ref[...], g_ref[...]).astype(BF16)
    for c in range(0, PA_W, 512):
        pa_ref[:, c:c + 512] = _dot(h, w_ref[:, c:c + 512])
    for c in range(0, PN_W, 512):
        pn_ref[:, c:c + 512] = _dot(h, w_ref[:, PA_W + c:PA_W + c + 512]).astype(BF16)


def _in_even(x, g, w):
    t = x.shape[0]
    return pl.pallas_call(
        _in_even_body,
        grid=(t // TM,),
        in_specs=[pl.BlockSpec((TM, D_MODEL), lambda i: (i, 0)),
                  _resident((1, D_MODEL)),
                  _resident((D_MODEL, EVEN_IN))],
        out_specs=[pl.BlockSpec((TM, PA_W), lambda i: (i, 0)),
                   pl.BlockSpec((TM, PN_W), lambda i: (i, 0))],
        out_shape=[jax.ShapeDtypeStruct((t, PA_W), F32),
                   jax.ShapeDtypeStruct((t, PN_W), BF16)],
        compiler_params=pltpu.CompilerParams(dimension_semantics=("arbitrary",),
                                             vmem_limit_bytes=VMEM_LIMIT),
        name="in_even",
    )(x, g, w)


def _hgrn_masks():
    t = np.arange(CHUNK)[:, None]
    s = np.arange(CHUNK)[None, :]
    out = np.zeros((2, 1 + len(LEVELS), CHUNK, CHUNK), np.float32)
    for d, rev in enumerate((False, True)):
        out[d, 0] = (t == s)
        for li, L in enumerate(LEVELS):
            half = L // 2
            same = (t // L) == (s // L)
            if rev:
                m = same & ((t % L) < half) & ((s % L) >= half)
            else:
                m = same & ((t % L) >= half) & ((s % L) < half)
            out[d, 1 + li] = m
    return np.tile(out, (1, 1, 1, A_HEADS))


def _tri(rev):
    t = lax.broadcasted_iota(jnp.int32, (CHUNK, CHUNK), 0)
    s = lax.broadcasted_iota(jnp.int32, (CHUNK, CHUNK), 1)
    return jnp.where((s >= t) if rev else (s <= t), 1.0, 0.0).astype(BF16)


def _level_ref(b3, L, rev):
    g, sub, w = b3.shape
    half = L // 2
    if L >= 2 * SUBLANES:
        n = L // SUBLANES
        pieces = []
        for blk in range(CHUNK // L):
            r = blk * L + (half if rev else half - 1)
            pieces.append(jnp.broadcast_to(b3[r // SUBLANES:r // SUBLANES + 1, r % SUBLANES:r % SUBLANES + 1, :],
                                           (n, sub, w)))
        return pieces[0] if len(pieces) == 1 else jnp.concatenate(pieces, axis=0)
    srow = lax.broadcasted_iota(jnp.int32, b3.shape, 1)
    if L == SUBLANES:
        r = half if rev else half - 1
        return jnp.broadcast_to(b3[:, r:r + 1, :], b3.shape)
    if L == 4:
        r = half if rev else half - 1
        lo = jnp.broadcast_to(b3[:, r:r + 1, :], b3.shape)
        hi = jnp.broadcast_to(b3[:, r + 4:r + 5, :], b3.shape)
        return jnp.where(srow < 4, lo, hi)
    odd = (srow % 2) == 1
    if rev:
        return jnp.where(odd, b3, pltpu.roll(b3, SUBLANES - 1, 1))
    return jnp.where(odd, pltpu.roll(b3, 1, 1), b3)


def _head_blockdiag(x):
    zero = jnp.zeros((CHUNK, A_DK), x.dtype)
    rows = []
    for h in range(A_HEADS):
        rows.append(jnp.concatenate([x[:, j * A_DK:(j + 1) * A_DK] if j == h else zero
                                     for j in range(A_HEADS)], axis=1))
    return jnp.concatenate(rows, axis=0)


def _hgrn_chunk(q, k, v, b, s_ref, d, rev, m_ref):
    tot = b[0:1, :] if rev else b[CHUNK - 1:CHUNK, :]
    q0 = (q * jnp.exp(b)).astype(BF16)
    k2 = (k * jnp.exp(tot - b)).astype(BF16)
    vb = v.astype(BF16)
    dec = jnp.exp(tot)
    inter = []
    for h in range(A_HEADS):
        sl = slice(h * A_DK, (h + 1) * A_DK)
        s_t = s_ref[d, h]
        inter.append(_dot_nt(q0[:, sl], s_t.astype(BF16)))
        ds = lax.dot_general(vb[:, sl], k2[:, sl], TN_DIMS, preferred_element_type=F32)
        s_ref[d, h] = dec[:, sl] * s_t + ds
    b3 = b.reshape(CHUNK // SUBLANES, SUBLANES, A_WIDTH)
    a = _dot_nt(q.astype(BF16), _head_blockdiag(k.astype(BF16))) * m_ref[d, 0]
    for li, L in enumerate(LEVELS):
        e = jnp.exp(-jnp.abs(b3 - _level_ref(b3, L, rev))).reshape(CHUNK, A_WIDTH)
        sl = _dot_nt((q * e).astype(BF16), _head_blockdiag((k * e).astype(BF16)))
        a = a + sl * m_ref[d, 1 + li]
    o = _dot(a.astype(BF16), _head_blockdiag(vb))
    return o + jnp.concatenate(inter, axis=1)


def _hgrn_body(layer, tbl_ref, qf_ref, zf_ref, vf_ref, qb_ref, zb_ref, vb_ref, lbl_ref, m_ref,
               of_ref, ob_ref, s_ref):
    step = pl.program_id(0)

    @pl.when(tbl_ref[2, step] == 1)
    def _():
        s_ref[...] = jnp.zeros_like(s_ref)

    n_even = lbl_ref.shape[0]
    logits = [lbl_ref[i] for i in range(n_even)]
    mx = functools.reduce(jnp.maximum, logits)
    ex = [jnp.exp(l - mx) for l in logits]
    den = functools.reduce(lambda x, y: x + y, ex)
    ps = [e / den for e in ex]
    lbs = functools.reduce(lambda x, y: x + y, ps[:layer + 1]) - ps[0]

    nc = HGRN_BLOCK // CHUNK
    refs = ((qf_ref, zf_ref, vf_ref, of_ref), (qb_ref, zb_ref, vb_ref, ob_ref))
    tris = (_tri(False), _tri(True))

    def chunk(c, carry):
        prep = []
        for d, rev in enumerate((False, True)):
            q_ref, z_ref, v_ref, _ = refs[d]
            r0 = pl.multiple_of(((nc - 1 - c) if rev else c) * CHUNK, CHUNK)
            rows = pl.ds(r0, CHUNK)
            qa = q_ref[rows, :]
            z = z_ref[rows, :]
            lb = lbs[d:d + 1, :]
            q = qa * _sigmoid(qa)
            f = lb + (1.0 - lb) * _sigmoid(z)
            k = (1.0 - lb) * _sigmoid(-z)
            lf = jnp.log(f)
            hi = lf.astype(BF16)
            r1 = lf - hi.astype(F32)
            mid = r1.astype(BF16)
            lo = (r1 - mid.astype(F32)).astype(BF16)
            cs = _dot(tris[d], jnp.concatenate([hi, mid, lo], axis=1))
            b = cs[:, :A_WIDTH] + cs[:, A_WIDTH:2 * A_WIDTH] + cs[:, 2 * A_WIDTH:]
            prep.append((rows, q, k, v_ref[rows, :], b))
        for d, rev in enumerate((False, True)):
            rows, q, k, v, b = prep[d]
            refs[d][3][rows, :] = _hgrn_chunk(q, k, v, b, s_ref, d, rev, m_ref)
        return carry

    lax.fori_loop(0, nc, chunk, 0)


def _hgrn_table(segs):
    rows = []
    for off, batch, seq in segs:
        nblk = seq // HGRN_BLOCK
        for bi in range(batch):
            base = (off + bi * seq) // HGRN_BLOCK
            for i in range(nblk):
                rows.append((base + i, base + nblk - 1 - i, int(i == 0)))
    return np.asarray(rows, np.int32).T.copy()


def _hgrn(pa, lb_logits, layer, segs):
    t = pa.shape[0]
    tbl = _hgrn_table(segs)
    steps = tbl.shape[1]
    masks = _hgrn_masks()

    def spec(col, which):
        return pl.BlockSpec((HGRN_BLOCK, A_WIDTH), lambda i, tb: (tb[which, i], col))

    grid_spec = pltpu.PrefetchScalarGridSpec(
        num_scalar_prefetch=1,
        grid=(steps,),
        in_specs=[spec(0, 0), spec(1, 0), spec(3, 0), spec(0, 1), spec(2, 1), spec(3, 1),
                  pl.BlockSpec(lb_logits.shape, lambda i, tb: (0, 0, 0)),
                  pl.BlockSpec(masks.shape, lambda i, tb: (0, 0, 0, 0))],
        out_specs=[pl.BlockSpec((HGRN_BLOCK, A_WIDTH), lambda i, tb: (tb[0, i], 0)),
                   pl.BlockSpec((HGRN_BLOCK, A_WIDTH), lambda i, tb: (tb[1, i], 0))],
        scratch_shapes=[pltpu.VMEM((2, A_HEADS, A_DK, A_DK), F32)],
    )
    return pl.pallas_call(
        functools.partial(_hgrn_body, layer),
        grid_spec=grid_spec,
        out_shape=[jax.ShapeDtypeStruct((t, A_WIDTH), F32),
                   jax.ShapeDtypeStruct((t, A_WIDTH), F32)],
        compiler_params=pltpu.CompilerParams(dimension_semantics=("arbitrary",),
                                             vmem_limit_bytes=VMEM_LIMIT),
        name="hgrn",
    )(jnp.asarray(tbl), pa, pa, pa, pa, pa, pa, lb_logits, jnp.asarray(masks))


def _na_build_bias(rpb_ref, bias_ref):
    n_ro = 2 * NA_KR - 1
    n_co = 2 * NA_KC - 1
    qc = lax.broadcasted_iota(jnp.int32, (GRID_W, GRID_W), 0)
    kc = lax.broadcasted_iota(jnp.int32, (GRID_W, GRID_W), 1)
    co = jnp.clip(kc - qc, -(NA_KC - 1), NA_KC - 1) + NA_KC - 1
    ws = jnp.clip(qc - NA_KC // 2, 0, GRID_W - NA_KC)
    valid = (kc >= ws) & (kc < ws + NA_KC)

    def build(hr, carry):
        h = hr // n_ro
        ro = hr % n_ro
        t = jnp.full((GRID_W, GRID_W), NEG, F32)
        for j in range(n_co):
            t = jnp.where(co == j, rpb_ref[hr * n_co + j], t)
        t = jnp.where(valid, t, NEG)
        for i in range(NA_KR):
            dd = ro - i

            @pl.when((dd >= 0) & (dd < NA_KR))
            def _():
                bias_ref[h, dd, :, i * GRID_W:(i + 1) * GRID_W] = t
        return carry

    lax.fori_loop(0, NA_HEADS * n_ro, build, 0)


def _na_body(tbl_ref, rpb_ref, q_ref, k_ref, v_ref, o_ref, bias_ref):
    step = pl.program_id(0)

    @pl.when(step == 0)
    def _():
        _na_build_bias(rpb_ref, bias_ref)

    r0 = tbl_ref[2, step]
    w0 = tbl_ref[3, step]
    n_rows = tbl_ref[4, step]
    lane = lax.broadcasted_iota(jnp.int32, (GRID_W, 2 * NA_HD), 1)
    low = lane < NA_HD
    nkeys = NA_KR * GRID_W
    scale = jnp.asarray(NA_HD ** -0.5, BF16)

    def row(rr, carry):
        r = r0 + rr
        row_start = jnp.clip(r - NA_KR // 2, 0, n_rows - NA_KR)
        dd = row_start - r + (NA_KR - 1)
        k0 = pl.multiple_of((row_start - w0) * GRID_W, GRID_W)
        q0 = pl.multiple_of(rr * GRID_W, GRID_W)
        pairs = [slice(hp * 2 * NA_HD, (hp + 1) * 2 * NA_HD) for hp in range(NA_HEADS // 2)]
        scores = []
        for lanes in pairs:
            qp = q_ref[pl.ds(q0, GRID_W), lanes] * scale
            kw = k_ref[pl.ds(k0, nkeys), lanes]
            zero = jnp.zeros_like(qp)
            q2 = jnp.concatenate([jnp.where(low, qp, zero), jnp.where(low, zero, qp)], axis=0)
            scores.append(_dot_nt(q2, kw))
        probs = []
        for hp, s in enumerate(scores):
            s = s + jnp.concatenate([bias_ref[2 * hp, dd], bias_ref[2 * hp + 1, dd]], axis=0)
            m = jnp.max(s, axis=-1, keepdims=True)
            p = jnp.exp(s - m)
            probs.append((p.astype(BF16), jnp.sum(p, axis=-1, keepdims=True)))
        for lanes, (p, l) in zip(pairs, probs):
            pv = _dot(p, v_ref[pl.ds(k0, nkeys), lanes]) / l
            o = jnp.where(low, pv[:GRID_W], pv[GRID_W:])
            o_ref[pl.ds(q0, GRID_W), lanes] = o.astype(o_ref.dtype)
        return carry

    lax.fori_loop(0, NA_ROWS, row, 0)


def _na_table(segs):
    blk = NA_ROWS * GRID_W
    rows = []
    for off, batch, seq in segs:
        n_rows = seq // GRID_W
        assert n_rows >= NA_WIN
        for bi in range(batch):
            base = off + bi * seq
            for i in range(seq // blk):
                r0 = i * NA_ROWS
                w0 = min(max(r0 - NA_ROWS, 0), n_rows - NA_WIN)
                rows.append((base // blk + i, base // GRID_W + w0, r0, w0, n_rows))
    return np.asarray(rows, np.int32).T.copy()


def _na(pn, rpb, segs):
    t = pn.shape[0]
    tbl = _na_table(segs)
    steps = tbl.shape[1]
    blk = NA_ROWS * GRID_W

    def window(col):
        return pl.BlockSpec((pl.Element(NA_WIN * GRID_W), pl.Element(NA_WIDTH)),
                            lambda i, tb: (tb[1, i] * GRID_W, col * NA_WIDTH))

    grid_spec = pltpu.PrefetchScalarGridSpec(
        num_scalar_prefetch=1,
        grid=(steps,),
        in_specs=[pl.BlockSpec(memory_space=pltpu.SMEM),
                  pl.BlockSpec((blk, NA_WIDTH), lambda i, tb: (tb[0, i], 0)),
                  window(1), window(2)],
        out_specs=pl.BlockSpec((blk, NA_WIDTH), lambda i, tb: (tb[0, i], 0)),
        scratch_shapes=[pltpu.VMEM((NA_HEADS, NA_KR, GRID_W, NA_KR * GRID_W), F32)],
    )
    return pl.pallas_call(
        _na_body,
        grid_spec=grid_spec,
        out_shape=jax.ShapeDtypeStruct((t, NA_WIDTH), BF16),
        compiler_params=pltpu.CompilerParams(dimension_semantics=("arbitrary",),
                                             vmem_limit_bytes=VMEM_LIMIT),
        name="natten",
    )(jnp.asarray(tbl), rpb.astype(F32).reshape(-1), pn, pn, pn)


def _ffn_tail(x1, gf_ref, wi_ref, wo_ref, gfin_ref, o_ref, final):
    h = _rms(x1, gf_ref[...]).astype(BF16)
    acc = x1
    for c in range(0, D_FF, FF_CHUNK):
        g = _dot(h, wi_ref[:, c:c + FF_CHUNK])
        u = _dot(h, wi_ref[:, D_FF + c:D_FF + c + FF_CHUNK])
        a = (g * _sigmoid(g) * u).astype(BF16)
        acc = acc + _dot(a, wo_ref[c:c + FF_CHUNK, :])
    if final:
        acc = _rms(acc, gfin_ref[...])
    o_ref[...] = acc


def _post_even_body(final, x_ref, of_ref, ob_ref, ga_ref, n_ref, gain_ref, wm_ref, gf_ref, wi_ref,
                    wo_ref, gfin_ref, o_ref):
    o = of_ref[...] + ob_ref[...]
    parts = []
    for h in range(A_HEADS):
        oh = o[:, h * A_DK:(h + 1) * A_DK]
        parts.append(oh * lax.rsqrt(jnp.mean(oh * oh, axis=-1, keepdims=True) + EPS))
    on = jnp.concatenate(parts, axis=1) * gain_ref[...]
    g = ga_ref[...]
    oa = (on * (g * _sigmoid(g))).astype(BF16)
    mix = _dot(oa, wm_ref[0:A_WIDTH, :]) + _dot(n_ref[...], wm_ref[A_WIDTH:A_WIDTH + NA_WIDTH, :])
    _ffn_tail(x_ref[...] + mix, gf_ref, wi_ref, wo_ref, gfin_ref, o_ref, final)


def _post_even(x, o_f, o_b, pa, o_n, gain, w_mix, g_ffn, w_ffn_in, w_ffn_out, g_final, final):
    t = x.shape[0]
    tok = lambda w, col=0: pl.BlockSpec((TM, w), lambda i: (i, col))
    return pl.pallas_call(
        functools.partial(_post_even_body, final),
        grid=(t // TM,),
        in_specs=[tok(D_MODEL), tok(A_WIDTH), tok(A_WIDTH), tok(A_WIDTH, 4), tok(NA_WIDTH),
                  _resident((1, A_WIDTH)), _resident((A_WIDTH + NA_WIDTH, D_MODEL)),
                  _resident((1, D_MODEL)), _resident((D_MODEL, 2 * D_FF)), _resident((D_FF, D_MODEL)),
                  _resident((1, D_MODEL))],
        out_specs=tok(D_MODEL),
        out_shape=jax.ShapeDtypeStruct((t, D_MODEL), F32),
        compiler_params=pltpu.CompilerParams(dimension_semantics=("arbitrary",),
                                             vmem_limit_bytes=VMEM_LIMIT),
        name="post_even",
    )(x, o_f, o_b, pa, o_n, gain, w_mix, g_ffn, w_ffn_in, w_ffn_out, g_final)


def _odd_body(final, tbl_ref, x_ref, xp_ref, xn_ref, gm_ref, wc_ref, cw_ref, wco_ref, gf_ref, wi_ref,
              wo_ref, gfin_ref, o_ref):
    step = pl.program_id(0)
    x = x_ref[...]
    gm = gm_ref[...]
    h = _rms(x, gm).astype(BF16)
    z = _dot(h, wc_ref[:, D_MODEL:2 * D_MODEL]) * _dot(h, wc_ref[:, 2 * D_MODEL:3 * D_MODEL])
    hh = _rms(jnp.concatenate([xp_ref[...], xn_ref[...]], axis=0), gm).astype(BF16)
    zh = _dot(hh, wc_ref[:, D_MODEL:2 * D_MODEL]) * _dot(hh, wc_ref[:, 2 * D_MODEL:3 * D_MODEL])
    z_prev = jnp.where(tbl_ref[2, step] == 1, 0.0, zh[HALO - 1:HALO, :])
    z_next = jnp.where(tbl_ref[3, step] == 1, 0.0, zh[HALO:HALO + 1, :])
    row = lax.broadcasted_iota(jnp.int32, (TM, 1), 0)
    z_dn = jnp.where(row == 0, z_prev, pltpu.roll(z, 1, 0))
    z_up = jnp.where(row == TM - 1, z_next, pltpu.roll(z, TM - 1, 0))
    conv = z_dn * cw_ref[0:1, :] + z * cw_ref[1:2, :] + z_up * cw_ref[2:3, :]
    y = (_dot(h, wc_ref[:, 0:D_MODEL]) * conv).astype(BF16)
    _ffn_tail(x + _dot(y, wco_ref[...]), gf_ref, wi_ref, wo_ref, gfin_ref, o_ref, final)


def _odd_table(segs, t):
    rows = []
    starts = set()
    ends = set()
    for off, batch, seq in segs:
        for bi in range(batch):
            starts.add(off + bi * seq)
            ends.add(off + (bi + 1) * seq)
    for i in range(t // TM):
        t0 = i * TM
        first = int(t0 in starts)
        last = int(t0 + TM in ends)
        rows.append((max(t0 // HALO - 1, 0), min((t0 + TM) // HALO, t // HALO - 1), first, last))
    return np.asarray(rows, np.int32).T.copy()


def _odd(x, g_mix, w_conv_in, conv_w, w_conv_out, g_ffn, w_ffn_in, w_ffn_out, g_final, final, segs):
    t = x.shape[0]
    tbl = _odd_table(segs, t)
    res = lambda shape: pl.BlockSpec(shape, lambda i, tb: (0,) * len(shape), pipeline_mode=pl.Buffered(1))
    grid_spec = pltpu.PrefetchScalarGridSpec(
        num_scalar_prefetch=1,
        grid=(t // TM,),
        in_specs=[pl.BlockSpec((TM, D_MODEL), lambda i, tb: (i, 0)),
                  pl.BlockSpec((HALO, D_MODEL), lambda i, tb: (tb[0, i], 0)),
                  pl.BlockSpec((HALO, D_MODEL), lambda i, tb: (tb[1, i], 0)),
                  res((1, D_MODEL)), res((D_MODEL, 3 * D_MODEL)), res((CONV_W, D_MODEL)),
                  res((D_MODEL, D_MODEL)), res((1, D_MODEL)), res((D_MODEL, 2 * D_FF)),
                  res((D_FF, D_MODEL)), res((1, D_MODEL))],
        out_specs=pl.BlockSpec((TM, D_MODEL), lambda i, tb: (i, 0)),
    )
    return pl.pallas_call(
        functools.partial(_odd_body, final),
        grid_spec=grid_spec,
        out_shape=jax.ShapeDtypeStruct((t, D_MODEL), F32),
        compiler_params=pltpu.CompilerParams(dimension_semantics=("arbitrary",),
                                             vmem_limit_bytes=VMEM_LIMIT),
        name="odd_layer",
    )(jnp.asarray(tbl), x, x, x, g_mix, w_conv_in, conv_w, w_conv_out, g_ffn, w_ffn_in, w_ffn_out, g_final)


def _trunk(xs, norm_mix, norm_ffn, norm_final, even_w_in, even_w_out, hgrn_lb_logits, hgrn_norm,
           na_rpb, conv_w_in, conv_w, conv_w_out, ffn_w_in, ffn_w_out):
    depth = norm_mix.shape[0]
    segs = []
    off = 0
    for a in xs:
        segs.append((off, a.shape[0], a.shape[1]))
        off += a.shape[0] * a.shape[1]
    x = jnp.concatenate([a.reshape(-1, D_MODEL) for a in xs], axis=0)
    row = lambda v: v.reshape(1, -1).astype(F32)
    g_final = row(norm_final)
    for l in range(depth):
        final = l == depth - 1
        w_fi = ffn_w_in[l].astype(BF16)
        w_fo = ffn_w_out[l].astype(BF16)
        if l % 2 == 0:
            e = l // 2
            pa, pn = _in_even(x, row(norm_mix[l]), even_w_in[e].astype(BF16))
            o_f, o_b = _hgrn(pa, hgrn_lb_logits.astype(F32), e, segs)
            o_n = _na(pn, na_rpb[e], segs)
            x = _post_even(x, o_f, o_b, pa, o_n, row(hgrn_norm[e]), even_w_out[e].astype(BF16),
                           row(norm_ffn[l]), w_fi, w_fo, g_final, final)
        else:
            o = l // 2
            x = _odd(x, row(norm_mix[l]), conv_w_in[o].astype(BF16), conv_w[o].astype(F32),
                     conv_w_out[o].astype(BF16), row(norm_ffn[l]), w_fi, w_fo, g_final, final, segs)
    outs = []
    for (off, batch, seq) in segs:
        outs.append(x[off:off + batch * seq].reshape(batch, seq, D_MODEL))
    return tuple(outs)


def kernel(x_prompt, x_sample, norm_mix, norm_ffn, norm_final, even_w_in, even_w_out, hgrn_lb_logits,
           hgrn_norm, na_rpb, conv_w_in, conv_w, conv_w_out, ffn_w_in, ffn_w_out):
    return _trunk([x_prompt, x_sample], norm_mix, norm_ffn, norm_final, even_w_in, even_w_out,
                  hgrn_lb_logits, hgrn_norm, na_rpb, conv_w_in, conv_w, conv_w_out, ffn_w_in, ffn_w_out)
```

```python
import functools

import numpy as np
import jax
import jax.numpy as jnp
from jax import lax
from jax.experimental import pallas as pl
from jax.experimental.pallas import tpu as pltpu

F32 = jnp.float32
BF16 = jnp.bfloat16

SUBLANES = 8
D_MODEL = 1024
EPS = 1e-6
NEG = -1e30
LOG2E = 1.4426950408889634
GRID_W = 64
A_WIDTH = 512
A_HEADS = 4
A_DK = 128
CHUNK = 64
LEVELS = (2, 4, 8, 16, 32, 64)
NA_WIDTH = 512
NA_HEADS = 8
NA_HD = 64
NA_KR = 8
NA_KC = 16
CONV_W = 3
D_FF = 2816
FF_CHUNK = 256
PA_W = 5 * A_WIDTH
PN_W = 3 * NA_WIDTH
EVEN_IN = PA_W + PN_W

TM = 512
HGRN_BLOCK = 512
NA_ROWS = 8
NA_WIN = 3 * NA_ROWS
HALO = 8
VMEM_LIMIT = 56 * 1024 * 1024

NT_DIMS = (((1,), (1,)), ((), ()))
TN_DIMS = (((0,), (0,)), ((), ()))


def _resident(shape):
    nd = len(shape)
    return pl.BlockSpec(shape, lambda *_: (0,) * nd, pipeline_mode=pl.Buffered(1))


def _rms(x, g):
    return x * lax.rsqrt(jnp.mean(x * x, axis=-1, keepdims=True) + EPS) * g


def _sigmoid(x):
    return jax.nn.sigmoid(x)


def _dot(a, b):
    return jnp.dot(a, b, preferred_element_type=F32)


def _dot_nt(a, b):
    return lax.dot_general(a, b, NT_DIMS, preferred_element_type=F32)


def _in_even_body(x_ref, g_ref, w_ref, pa_ref, pn_ref):
    h = _rms(x_ref[...], g_ref[...]).astype(BF16)
    for c in range(0, PA_W, 512):
        pa_ref[:, c:c + 512] = _dot(h, w_ref[:, c:c + 512])
    for c in range(0, PN_W, 512):
        pn_ref[:, c:c + 512] = _dot(h, w_ref[:, PA_W + c:PA_W + c + 512]).astype(BF16)


def _in_even(x, g, w):
    t = x.shape[0]
    return pl.pallas_call(
        _in_even_body,
        grid=(t // TM,),
        in_specs=[pl.BlockSpec((TM, D_MODEL), lambda i: (i, 0)),
                  _resident((1, D_MODEL)),
                  _resident((D_MODEL, EVEN_IN))],
        out_specs=[pl.BlockSpec((TM, PA_W), lambda i: (i, 0)),
                   pl.BlockSpec((TM, PN_W), lambda i: (i, 0))],
        out_shape=[jax.ShapeDtypeStruct((t, PA_W), F32),
                   jax.ShapeDtypeStruct((t, PN_W), BF16)],
        compiler_params=pltpu.CompilerParams(dimension_semantics=("arbitrary",),
                                             vmem_limit_bytes=VMEM_LIMIT),
        name="in_even",
    )(x, g, w)


def _hgrn_masks():
    t = np.arange(CHUNK)[:, None]
    s = np.arange(CHUNK)[None, :]
    out = np.zeros((2, 1 + len(LEVELS), CHUNK, CHUNK), np.float32)
    for d, rev in enumerate((False, True)):
        out[d, 0] = (t == s)
        for li, L in enumerate(LEVELS):
            half = L // 2
            same = (t // L) == (s // L)
            if rev:
                m = same & ((t % L) < half) & ((s % L) >= half)
            else:
                m = same & ((t % L) >= half) & ((s % L) < half)
            out[d, 1 + li] = m
    return np.tile(out, (1, 1, 1, 2))


def _hgrn_signs():
    t = np.arange(CHUNK)
    out = np.zeros((2, len(LEVELS), CHUNK), np.float32)
    for d, rev in enumerate((False, True)):
        for li, L in enumerate(LEVELS):
            second = (t % L) >= L // 2
            out[d, li] = np.where(second != rev, 1.0, -1.0)
    return np.ascontiguousarray(np.broadcast_to(out[..., None], out.shape + (A_DK,)))


def _tri(rev):
    t = lax.broadcasted_iota(jnp.int32, (CHUNK, CHUNK), 0)
    s = lax.broadcasted_iota(jnp.int32, (CHUNK, CHUNK), 1)
    return jnp.where((s >= t) if rev else (s <= t), 1.0, 0.0).astype(BF16)


def _level_ref(b3, L, rev):
    g, sub, w = b3.shape
    half = L // 2
    if L >= 2 * SUBLANES:
        n = L // SUBLANES
        pieces = []
        for blk in range(CHUNK // L):
            r = blk * L + (half if rev else half - 1)
            pieces.append(jnp.broadcast_to(b3[r // SUBLANES:r // SUBLANES + 1, r % SUBLANES:r % SUBLANES + 1, :],
                                           (n, sub, w)))
        return pieces[0] if len(pieces) == 1 else jnp.concatenate(pieces, axis=0)
    srow = lax.broadcasted_iota(jnp.int32, b3.shape, 1)
    if L == SUBLANES:
        r = half if rev else half - 1
        return jnp.broadcast_to(b3[:, r:r + 1, :], b3.shape)
    if L == 4:
        r = half if rev else half - 1
        lo = jnp.broadcast_to(b3[:, r:r + 1, :], b3.shape)
        hi = jnp.broadcast_to(b3[:, r + 4:r + 5, :], b3.shape)
        return jnp.where(srow < 4, lo, hi)
    odd = (srow % 2) == 1
    if rev:
        return jnp.where(odd, b3, pltpu.roll(b3, SUBLANES - 1, 1))
    return jnp.where(odd, pltpu.roll(b3, 1, 1), b3)


def _hgrn_prepare(d, rev, rows, in_refs, lb, tri, sgn_ref, lf_ref, qk_ref, qs_ref, kbd_ref, vbd_ref,
                  q0_ref, k2_ref, dec_ref):
    q_ref, z_ref, v_ref = in_refs
    heads = [slice(h * A_DK, (h + 1) * A_DK) for h in range(A_HEADS)]
    for h, sl in enumerate(heads):
        qa = q_ref[rows, sl]
        lbh = lb[:, sl]
        f = lbh + (1.0 - lbh) * _sigmoid(z_ref[rows, sl])
        qk_ref[d, 0, :, sl] = qa * _sigmoid(qa)
        qk_ref[d, 1, :, sl] = 1.0 - f
        lf = jnp.log(f)
        hi = lf.astype(BF16)
        r1 = lf - hi.astype(F32)
        mid = r1.astype(BF16)
        lf_ref[d, :, sl] = hi
        lf_ref[d, :, A_WIDTH + h * A_DK:A_WIDTH + (h + 1) * A_DK] = mid
        lf_ref[d, :, 2 * A_WIDTH + h * A_DK:2 * A_WIDTH + (h + 1) * A_DK] = (r1 - mid.astype(F32)).astype(BF16)
    cs = _dot(tri, lf_ref[d])
    b_all = (cs[:, :A_WIDTH] + cs[:, A_WIDTH:2 * A_WIDTH] + cs[:, 2 * A_WIDTH:]) * LOG2E
    for h, sl in enumerate(heads):
        pair = h // 2
        blk = slice((h % 2) * CHUNK, (h % 2 + 1) * CHUNK)
        psl = slice((h % 2) * A_DK, (h % 2 + 1) * A_DK)
        q = qk_ref[d, 0, :, sl]
        k = qk_ref[d, 1, :, sl]
        b = b_all[:, sl]
        tot = b[0:1, :] if rev else b[CHUNK - 1:CHUNK, :]
        q0_ref[d, :, sl] = (q * jnp.exp2(b)).astype(BF16)
        k2_ref[d, :, sl] = (k * jnp.exp2(tot - b)).astype(BF16)
        dec_ref[d, :, sl] = jnp.broadcast_to(jnp.exp2(tot), (SUBLANES, A_DK))
        vbd_ref[d, pair, blk, psl] = v_ref[rows, sl].astype(BF16)
        qs_ref[d, 0, :, sl] = q.astype(BF16)
        kbd_ref[d, 0, pair, blk, psl] = k.astype(BF16)
        b3 = b.reshape(CHUNK // SUBLANES, SUBLANES, A_DK)
        for li, L in enumerate(LEVELS):
            sgn = sgn_ref[d, li].reshape(b3.shape)
            e = jnp.exp2((b3 - _level_ref(b3, L, rev)) * sgn).reshape(CHUNK, A_DK)
            qs_ref[d, 1 + li, :, sl] = (q * e).astype(BF16)
            kbd_ref[d, 1 + li, pair, blk, psl] = (k * e).astype(BF16)


def _hgrn_consume(d, rows, o_ref, m_ref, s_ref, qs_ref, kbd_ref, vbd_ref, q0_ref, k2_ref, dec_ref):
    inter = []
    for h in range(A_HEADS):
        sl = slice(h * A_DK, (h + 1) * A_DK)
        blk = slice((h % 2) * CHUNK, (h % 2 + 1) * CHUNK)
        psl = slice((h % 2) * A_DK, (h % 2 + 1) * A_DK)
        s_t = s_ref[d, h]
        inter.append(_dot_nt(q0_ref[d, :, sl], s_t.astype(BF16)))
        ds = lax.dot_general(vbd_ref[d, h // 2, blk, psl], k2_ref[d, :, sl], TN_DIMS,
                             preferred_element_type=F32)
        s_ref[d, h] = dec_ref[d, 0:1, sl] * s_t + ds
    intra = []
    for pair in range(A_HEADS // 2):
        psl = slice(pair * 2 * A_DK, (pair + 1) * 2 * A_DK)
        a = _dot_nt(qs_ref[d, 0, :, psl], kbd_ref[d, 0, pair]) * m_ref[d, 0]
        for li in range(len(LEVELS)):
            a = a + _dot_nt(qs_ref[d, 1 + li, :, psl], kbd_ref[d, 1 + li, pair]) * m_ref[d, 1 + li]
        intra.append(_dot(a.astype(BF16), vbd_ref[d, pair]))
    o_ref[rows, :] = jnp.concatenate(intra, axis=1) + jnp.concatenate(inter, axis=1)


def _hgrn_body(layer, tbl_ref, qf_ref, zf_ref, vf_ref, qb_ref, zb_ref, vb_ref, lbl_ref, m_ref, sgn_ref,
               of_ref, ob_ref, s_ref, lf_ref, qk_ref, qs_ref, kbd_ref, vbd_ref, q0_ref, k2_ref, dec_ref):
    step = pl.program_id(0)

    @pl.when(step == 0)
    def _():
        kbd_ref[...] = jnp.zeros_like(kbd_ref)
        vbd_ref[...] = jnp.zeros_like(vbd_ref)

    @pl.when(tbl_ref[2, step] == 1)
    def _():
        s_ref[...] = jnp.zeros_like(s_ref)

    n_even = lbl_ref.shape[0]
    logits = [lbl_ref[i] for i in range(n_even)]
    mx = functools.reduce(jnp.maximum, logits)
    ex = [jnp.exp(l - mx) for l in logits]
    den = functools.reduce(lambda x, y: x + y, ex)
    ps = [e / den for e in ex]
    lbs = functools.reduce(lambda x, y: x + y, ps[:layer + 1]) - ps[0]

    nc = HGRN_BLOCK // CHUNK
    ins = ((qf_ref, zf_ref, vf_ref), (qb_ref, zb_ref, vb_ref))
    outs = (of_ref, ob_ref)
    tris = (_tri(False), _tri(True))
    ops = (qs_ref, kbd_ref, vbd_ref, q0_ref, k2_ref, dec_ref)

    def rows_of(c, rev):
        return pl.ds(pl.multiple_of(((nc - 1 - c) if rev else c) * CHUNK, CHUNK), CHUNK)

    def prepare(c, d):
        rev = d == 1
        _hgrn_prepare(d, rev, rows_of(c, rev), ins[d], lbs[d:d + 1, :], tris[d], sgn_ref, lf_ref, qk_ref, *ops)

    def consume(c, d):
        _hgrn_consume(d, rows_of(c, d == 1), outs[d], m_ref, s_ref, *ops)

    prepare(0, 0)

    def trip(c, carry):
        prepare(c, 1)
        consume(c, 0)

        @pl.when(c >= 0)
        def _():
            prepare(jnp.minimum(c + 1, nc - 1), 0)
            consume(c, 1)
        return carry

    lax.fori_loop(0, nc, trip, 0)


def _hgrn_table(segs):
    rows = []
    for off, batch, seq in segs:
        nblk = seq // HGRN_BLOCK
        for bi in range(batch):
            base = (off + bi * seq) // HGRN_BLOCK
            for i in range(nblk):
                rows.append((base + i, base + nblk - 1 - i, int(i == 0)))
    return np.asarray(rows, np.int32).T.copy()


def _hgrn(pa, lb_logits, layer, segs):
    t = pa.shape[0]
    tbl = _hgrn_table(segs)
    steps = tbl.shape[1]
    masks = _hgrn_masks()
    signs = _hgrn_signs()
    n_lv = 1 + len(LEVELS)

    def spec(col, which):
        return pl.BlockSpec((HGRN_BLOCK, A_WIDTH), lambda i, tb: (tb[which, i], col))

    grid_spec = pltpu.PrefetchScalarGridSpec(
        num_scalar_prefetch=1,
        grid=(steps,),
        in_specs=[spec(0, 0), spec(1, 0), spec(3, 0), spec(0, 1), spec(2, 1), spec(3, 1),
                  pl.BlockSpec(lb_logits.shape, lambda i, tb: (0, 0, 0)),
                  pl.BlockSpec(masks.shape, lambda i, tb: (0, 0, 0, 0)),
                  pl.BlockSpec(signs.shape, lambda i, tb: (0, 0, 0, 0))],
        out_specs=[pl.BlockSpec((HGRN_BLOCK, A_WIDTH), lambda i, tb: (tb[0, i], 0)),
                   pl.BlockSpec((HGRN_BLOCK, A_WIDTH), lambda i, tb: (tb[1, i], 0))],
        scratch_shapes=[pltpu.VMEM((2, A_HEADS, A_DK, A_DK), F32),
                        pltpu.VMEM((2, CHUNK, 3 * A_WIDTH), BF16),
                        pltpu.VMEM((2, 2, CHUNK, A_WIDTH), F32),
                        pltpu.VMEM((2, n_lv, CHUNK, A_WIDTH), BF16),
                        pltpu.VMEM((2, n_lv, 2, 2 * CHUNK, 2 * A_DK), BF16),
                        pltpu.VMEM((2, 2, 2 * CHUNK, 2 * A_DK), BF16),
                        pltpu.VMEM((2, CHUNK, A_WIDTH), BF16),
                        pltpu.VMEM((2, CHUNK, A_WIDTH), BF16),
                        pltpu.VMEM((2, SUBLANES, A_WIDTH), F32)],
    )
    return pl.pallas_call(
        functools.partial(_hgrn_body, layer),
        grid_spec=grid_spec,
        out_shape=[jax.ShapeDtypeStruct((t, A_WIDTH), F32),
                   jax.ShapeDtypeStruct((t, A_WIDTH), F32)],
        compiler_params=pltpu.CompilerParams(dimension_semantics=("arbitrary",),
                                             vmem_limit_bytes=VMEM_LIMIT),
        name="hgrn",
    )(jnp.asarray(tbl), pa, pa, pa, pa, pa, pa, lb_logits, jnp.asarray(masks), jnp.asarray(signs))


def _na_build_bias(rpb_ref, bias_ref):
    n_ro = 2 * NA_KR - 1
    n_co = 2 * NA_KC - 1
    qc = lax.broadcasted_iota(jnp.int32, (GRID_W, GRID_W), 0)
    kc = lax.broadcasted_iota(jnp.int32, (GRID_W, GRID_W), 1)
    co = jnp.clip(kc - qc, -(NA_KC - 1), NA_KC - 1) + NA_KC - 1
    ws = jnp.clip(qc - NA_KC // 2, 0, GRID_W - NA_KC)
    valid = (kc >= ws) & (kc < ws + NA_KC)

    def build(hr, carry):
        h = hr // n_ro
        ro = hr % n_ro
        t = jnp.full((GRID_W, GRID_W), NEG, F32)
        for j in range(n_co):
            t = jnp.where(co == j, rpb_ref[hr * n_co + j], t)
        t = jnp.where(valid, t, NEG)
        for i in range(NA_KR):
            dd = ro - i

            @pl.when((dd >= 0) & (dd < NA_KR))
            def _():
                bias_ref[h, dd, :, i * GRID_W:(i + 1) * GRID_W] = t
        return carry

    lax.fori_loop(0, NA_HEADS * n_ro, build, 0)


def _na_body(tbl_ref, rpb_ref, q_ref, k_ref, v_ref, o_ref, bias_ref):
    step = pl.program_id(0)

    @pl.when(step == 0)
    def _():
        _na_build_bias(rpb_ref, bias_ref)

    r0 = tbl_ref[2, step]
    w0 = tbl_ref[3, step]
    n_rows = tbl_ref[4, step]
    lane = lax.broadcasted_iota(jnp.int32, (GRID_W, 2 * NA_HD), 1)
    low = lane < NA_HD
    nkeys = NA_KR * GRID_W
    scale = jnp.asarray(NA_HD ** -0.5, BF16)

    def row(rr, carry):
        r = r0 + rr
        row_start = jnp.clip(r - NA_KR // 2, 0, n_rows - NA_KR)
        dd = row_start - r + (NA_KR - 1)
        k0 = pl.multiple_of((row_start - w0) * GRID_W, GRID_W)
        q0 = pl.multiple_of(rr * GRID_W, GRID_W)
        pairs = [slice(hp * 2 * NA_HD, (hp + 1) * 2 * NA_HD) for hp in range(NA_HEADS // 2)]
        scores = []
        for lanes in pairs:
            qp = q_ref[pl.ds(q0, GRID_W), lanes] * scale
            kw = k_ref[pl.ds(k0, nkeys), lanes]
            zero = jnp.zeros_like(qp)
            q2 = jnp.concatenate([jnp.where(low, qp, zero), jnp.where(low, zero, qp)], axis=0)
            scores.append(_dot_nt(q2, kw))
        probs = []
        for hp, s in enumerate(scores):
            s = s + jnp.concatenate([bias_ref[2 * hp, dd], bias_ref[2 * hp + 1, dd]], axis=0)
            m = jnp.max(s, axis=-1, keepdims=True)
            p = jnp.exp(s - m)
            probs.append((p.astype(BF16), jnp.sum(p, axis=-1, keepdims=True)))
        for lanes, (p, l) in zip(pairs, probs):
            pv = _dot(p, v_ref[pl.ds(k0, nkeys), lanes]) / l
            o = jnp.where(low, pv[:GRID_W], pv[GRID_W:])
            o_ref[pl.ds(q0, GRID_W), lanes] = o.astype(o_ref.dtype)
        return carry

    lax.fori_loop(0, NA_ROWS, row, 0)


def _na_table(segs):
    blk = NA_ROWS * GRID_W
    rows = []
    for off, batch, seq in segs:
        n_rows = seq // GRID_W
        assert n_rows >= NA_WIN
        for bi in range(batch):
            base = off + bi * seq
            for i in range(seq // blk):
                r0 = i * NA_ROWS
                w0 = min(max(r0 - NA_ROWS, 0), n_rows - NA_WIN)
                rows.append((base // blk + i, base // GRID_W + w0, r0, w0, n_rows))
    return np.asarray(rows, np.int32).T.copy()


def _na(pn, rpb, segs):
    t = pn.shape[0]
    tbl = _na_table(segs)
    steps = tbl.shape[1]
    blk = NA_ROWS * GRID_W

    def window(col):
        return pl.BlockSpec((pl.Element(NA_WIN * GRID_W), pl.Element(NA_WIDTH)),
                            lambda i, tb: (tb[1, i] * GRID_W, col * NA_WIDTH))

    grid_spec = pltpu.PrefetchScalarGridSpec(
        num_scalar_prefetch=1,
        grid=(steps,),
        in_specs=[pl.BlockSpec(memory_space=pltpu.SMEM),
                  pl.BlockSpec((blk, NA_WIDTH), lambda i, tb: (tb[0, i], 0)),
                  window(1), window(2)],
        out_specs=pl.BlockSpec((blk, NA_WIDTH), lambda i, tb: (tb[0, i], 0)),
        scratch_shapes=[pltpu.VMEM((NA_HEADS, NA_KR, GRID_W, NA_KR * GRID_W), F32)],
    )
    return pl.pallas_call(
        _na_body,
        grid_spec=grid_spec,
        out_shape=jax.ShapeDtypeStruct((t, NA_WIDTH), BF16),
        compiler_params=pltpu.CompilerParams(dimension_semantics=("arbitrary",),
                                             vmem_limit_bytes=VMEM_LIMIT),
        name="natten",
    )(jnp.asarray(tbl), rpb.astype(F32).reshape(-1), pn, pn, pn)


def _ffn_tail(x1, gf_ref, wi_ref, wo_ref, gfin_ref, o_ref, final):
    h = _rms(x1, gf_ref[...]).astype(BF16)
    acc = x1
    for c in range(0, D_FF, FF_CHUNK):
        g = _dot(h, wi_ref[:, c:c + FF_CHUNK])
        u = _dot(h, wi_ref[:, D_FF + c:D_FF + c + FF_CHUNK])
        a = (g * _sigmoid(g) * u).astype(BF16)
        acc = acc + _dot(a, wo_ref[c:c + FF_CHUNK, :])
    if final:
        acc = _rms(acc, gfin_ref[...])
    o_ref[...] = acc


def _post_even_body(final, x_ref, of_ref, ob_ref, ga_ref, n_ref, gain_ref, wm_ref, gf_ref, wi_ref,
                    wo_ref, gfin_ref, o_ref):
    o = of_ref[...] + ob_ref[...]
    parts = []
    for h in range(A_HEADS):
        oh = o[:, h * A_DK:(h + 1) * A_DK]
        parts.append(oh * lax.rsqrt(jnp.mean(oh * oh, axis=-1, keepdims=True) + EPS))
    on = jnp.concatenate(parts, axis=1) * gain_ref[...]
    g = ga_ref[...]
    oa = (on * (g * _sigmoid(g))).astype(BF16)
    mix = _dot(oa, wm_ref[0:A_WIDTH, :]) + _dot(n_ref[...], wm_ref[A_WIDTH:A_WIDTH + NA_WIDTH, :])
    _ffn_tail(x_ref[...] + mix, gf_ref, wi_ref, wo_ref, gfin_ref, o_ref, final)


def _post_even(x, o_f, o_b, pa, o_n, gain, w_mix, g_ffn, w_ffn_in, w_ffn_out, g_final, final):
    t = x.shape[0]
    tok = lambda w, col=0: pl.BlockSpec((TM, w), lambda i: (i, col))
    return pl.pallas_call(
        functools.partial(_post_even_body, final),
        grid=(t // TM,),
        in_specs=[tok(D_MODEL), tok(A_WIDTH), tok(A_WIDTH), tok(A_WIDTH, 4), tok(NA_WIDTH),
                  _resident((1, A_WIDTH)), _resident((A_WIDTH + NA_WIDTH, D_MODEL)),
                  _resident((1, D_MODEL)), _resident((D_MODEL, 2 * D_FF)), _resident((D_FF, D_MODEL)),
                  _resident((1, D_MODEL))],
        out_specs=tok(D_MODEL),
        out_shape=jax.ShapeDtypeStruct((t, D_MODEL), F32),
        compiler_params=pltpu.CompilerParams(dimension_semantics=("arbitrary",),
                                             vmem_limit_bytes=VMEM_LIMIT),
        name="post_even",
    )(x, o_f, o_b, pa, o_n, gain, w_mix, g_ffn, w_ffn_in, w_ffn_out, g_final)


def _odd_body(final, tbl_ref, x_ref, xp_ref, xn_ref, gm_ref, wc_ref, cw_ref, wco_ref, gf_ref, wi_ref,
              wo_ref, gfin_ref, o_ref):
    step = pl.program_id(0)
    x = x_ref[...]
    gm = gm_ref[...]
    h = _rms(x, gm).astype(BF16)
    z = _dot(h, wc_ref[:, D_MODEL:2 * D_MODEL]) * _dot(h, wc_ref[:, 2 * D_MODEL:3 * D_MODEL])
    hh = _rms(jnp.concatenate([xp_ref[...], xn_ref[...]], axis=0), gm).astype(BF16)
    zh = _dot(hh, wc_ref[:, D_MODEL:2 * D_MODEL]) * _dot(hh, wc_ref[:, 2 * D_MODEL:3 * D_MODEL])
    z_prev = jnp.where(tbl_ref[2, step] == 1, 0.0, zh[HALO - 1:HALO, :])
    z_next = jnp.where(tbl_ref[3, step] == 1, 0.0, zh[HALO:HALO + 1, :])
    row = lax.broadcasted_iota(jnp.int32, (TM, 1), 0)
    z_dn = jnp.where(row == 0, z_prev, pltpu.roll(z, 1, 0))
    z_up = jnp.where(row == TM - 1, z_next, pltpu.roll(z, TM - 1, 0))
    conv = z_dn * cw_ref[0:1, :] + z * cw_ref[1:2, :] + z_up * cw_ref[2:3, :]
    y = (_dot(h, wc_ref[:, 0:D_MODEL]) * conv).astype(BF16)
    _ffn_tail(x + _dot(y, wco_ref[...]), gf_ref, wi_ref, wo_ref, gfin_ref, o_ref, final)


def _odd_table(segs, t):
    rows = []
    starts = set()
    ends = set()
    for off, batch, seq in segs:
        for bi in range(batch):
            starts.add(off + bi * seq)
            ends.add(off + (bi + 1) * seq)
    for i in range(t // TM):
        t0 = i * TM
        first = int(t0 in starts)
        last = int(t0 + TM in ends)
        rows.append((max(t0 // HALO - 1, 0), min((t0 + TM) // HALO, t // HALO - 1), first, last))
    return np.asarray(rows, np.int32).T.copy()


def _odd(x, g_mix, w_conv_in, conv_w, w_conv_out, g_ffn, w_ffn_in, w_ffn_out, g_final, final, segs):
    t = x.shape[0]
    tbl = _odd_table(segs, t)
    res = lambda shape: pl.BlockSpec(shape, lambda i, tb: (0,) * len(shape), pipeline_mode=pl.Buffered(1))
    grid_spec = pltpu.PrefetchScalarGridSpec(
        num_scalar_prefetch=1,
        grid=(t // TM,),
        in_specs=[pl.BlockSpec((TM, D_MODEL), lambda i, tb: (i, 0)),
                  pl.BlockSpec((HALO, D_MODEL), lambda i, tb: (tb[0, i], 0)),
                  pl.BlockSpec((HALO, D_MODEL), lambda i, tb: (tb[1, i], 0)),
                  res((1, D_MODEL)), res((D_MODEL, 3 * D_MODEL)), res((CONV_W, D_MODEL)),
                  res((D_MODEL, D_MODEL)), res((1, D_MODEL)), res((D_MODEL, 2 * D_FF)),
                  res((D_FF, D_MODEL)), res((1, D_MODEL))],
        out_specs=pl.BlockSpec((TM, D_MODEL), lambda i, tb: (i, 0)),
    )
    return pl.pallas_call(
        functools.partial(_odd_body, final),
        grid_spec=grid_spec,
        out_shape=jax.ShapeDtypeStruct((t, D_MODEL), F32),
        compiler_params=pltpu.CompilerParams(dimension_semantics=("arbitrary",),
                                             vmem_limit_bytes=VMEM_LIMIT),
        name="odd_layer",
    )(jnp.asarray(tbl), x, x, x, g_mix, w_conv_in, conv_w, w_conv_out, g_ffn, w_ffn_in, w_ffn_out, g_final)


def _trunk(xs, norm_mix, norm_ffn, norm_final, even_w_in, even_w_out, hgrn_lb_logits, hgrn_norm,
           na_rpb, conv_w_in, conv_w, conv_w_out, ffn_w_in, ffn_w_out):
    depth = norm_mix.shape[0]
    segs = []
    off = 0
    for a in xs:
        segs.append((off, a.shape[0], a.shape[1]))
        off += a.shape[0] * a.shape[1]
    x = jnp.concatenate([a.reshape(-1, D_MODEL) for a in xs], axis=0)
    row = lambda v: v.reshape(1, -1).astype(F32)
    g_final = row(norm_final)
    for l in range(depth):
        final = l == depth - 1
        w_fi = ffn_w_in[l].astype(BF16)
        w_fo = ffn_w_out[l].astype(BF16)
        if l % 2 == 0:
            e = l // 2
            pa, pn = _in_even(x, row(norm_mix[l]), even_w_in[e].astype(BF16))
            o_f, o_b = _hgrn(pa, hgrn_lb_logits.astype(F32), e, segs)
            o_n = _na(pn, na_rpb[e], segs)
            x = _post_even(x, o_f, o_b, pa, o_n, row(hgrn_norm[e]), even_w_out[e].astype(BF16),
                           row(norm_ffn[l]), w_fi, w_fo, g_final, final)
        else:
            o = l // 2
            x = _odd(x, row(norm_mix[l]), conv_w_in[o].astype(BF16), conv_w[o].astype(F32),
                     conv_w_out[o].astype(BF16), row(norm_ffn[l]), w_fi, w_fo, g_final, final, segs)
    outs = []
    for (off, batch, seq) in segs:
        outs.append(x[off:off + batch * seq].reshape(batch, seq, D_MODEL))
    return tuple(outs)


def kernel(x_prompt, x_sample, norm_mix, norm_ffn, norm_final, even_w_in, even_w_out, hgrn_lb_logits,
           hgrn_norm, na_rpb, conv_w_in, conv_w, conv_w_out, ffn_w_in, ffn_w_out):
    return _trunk([x_prompt, x_sample], norm_mix, norm_ffn, norm_final, even_w_in, even_w_out,
                  hgrn_lb_logits, hgrn_norm, na_rpb, conv_w_in, conv_w, conv_w_out, ffn_w_in, ffn_w_out)
```

```python
import functools

import numpy as np
import jax
import jax.numpy as jnp
from jax import lax
from jax.experimental import pallas as pl
from jax.experimental.pallas import tpu as pltpu

F32 = jnp.float32
BF16 = jnp.bfloat16

SUBLANES = 8
D_MODEL = 1024
EPS = 1e-6
NEG = -1e30
LOG2E = 1.4426950408889634
GRID_W = 64
A_WIDTH = 512
A_HEADS = 4
A_DK = 128
CHUNK = 64
LEVELS = (2, 4, 8, 16, 32, 64)
NA_WIDTH = 512
NA_HEADS = 8
NA_HD = 64
NA_KR = 8
NA_KC = 16
CONV_W = 3
D_FF = 2816
FF_CHUNK = 256
PA_W = 5 * A_WIDTH
PN_W = 3 * NA_WIDTH
EVEN_IN = PA_W + PN_W

TM = 512
HGRN_BLOCK = 256
NA_ROWS = 8
NA_WIN = 3 * NA_ROWS
HALO = 8
VMEM_LIMIT = 56 * 1024 * 1024

NT_DIMS = (((1,), (1,)), ((), ()))
TN_DIMS = (((0,), (0,)), ((), ()))


def _resident(shape):
    nd = len(shape)
    return pl.BlockSpec(shape, lambda *_: (0,) * nd, pipeline_mode=pl.Buffered(1))


def _rms(x, g):
    return x * lax.rsqrt(jnp.mean(x * x, axis=-1, keepdims=True) + EPS) * g


def _sigmoid(x):
    return jax.nn.sigmoid(x)


def _dot(a, b):
    return jnp.dot(a, b, preferred_element_type=F32)


def _dot_nt(a, b):
    return lax.dot_general(a, b, NT_DIMS, preferred_element_type=F32)


def _in_even_body(x_ref, g_ref, w_ref, pa_ref, pn_ref):
    h = _rms(x_ref[...], g_ref[...]).astype(BF16)
    for c in range(0, PA_W, 512):
        pa_ref[:, c:c + 512] = _dot(h, w_ref[:, c:c + 512])
    for c in range(0, PN_W, 512):
        pn_ref[:, c:c + 512] = _dot(h, w_ref[:, PA_W + c:PA_W + c + 512]).astype(BF16)


def _in_even(x, g, w):
    t = x.shape[0]
    return pl.pallas_call(
        _in_even_body,
        grid=(t // TM,),
        in_specs=[pl.BlockSpec((TM, D_MODEL), lambda i: (i, 0)),
                  _resident((1, D_MODEL)),
                  _resident((D_MODEL, EVEN_IN))],
        out_specs=[pl.BlockSpec((TM, PA_W), lambda i: (i, 0)),
                   pl.BlockSpec((TM, PN_W), lambda i: (i, 0))],
        out_shape=[jax.ShapeDtypeStruct((t, PA_W), F32),
                   jax.ShapeDtypeStruct((t, PN_W), BF16)],
        compiler_params=pltpu.CompilerParams(dimension_semantics=("arbitrary",),
                                             vmem_limit_bytes=VMEM_LIMIT),
        name="in_even",
    )(x, g, w)


def _hgrn_masks():
    t = np.arange(CHUNK)[:, None]
    s = np.arange(CHUNK)[None, :]
    out = np.zeros((2, 1 + len(LEVELS), CHUNK, CHUNK), np.float32)
    for d, rev in enumerate((False, True)):
        out[d, 0] = (t == s)
        for li, L in enumerate(LEVELS):
            half = L // 2
            same = (t // L) == (s // L)
            if rev:
                m = same & ((t % L) < half) & ((s % L) >= half)
            else:
                m = same & ((t % L) >= half) & ((s % L) < half)
            out[d, 1 + li] = m
    return np.tile(out, (1, 1, 1, 2))


def _hgrn_signs():
    t = np.arange(CHUNK)
    out = np.zeros((2, len(LEVELS), CHUNK), np.float32)
    for d, rev in enumerate((False, True)):
        for li, L in enumerate(LEVELS):
            second = (t % L) >= L // 2
            out[d, li] = np.where(second != rev, 1.0, -1.0)
    return np.ascontiguousarray(np.broadcast_to(out[..., None], out.shape + (A_DK,)))


def _tri(rev):
    t = lax.broadcasted_iota(jnp.int32, (CHUNK, CHUNK), 0)
    s = lax.broadcasted_iota(jnp.int32, (CHUNK, CHUNK), 1)
    return jnp.where((s >= t) if rev else (s <= t), 1.0, 0.0).astype(BF16)


def _level_ref(b3, L, rev):
    g, sub, w = b3.shape
    half = L // 2
    if L >= 2 * SUBLANES:
        n = L // SUBLANES
        pieces = []
        for blk in range(CHUNK // L):
            r = blk * L + (half if rev else half - 1)
            pieces.append(jnp.broadcast_to(b3[r // SUBLANES:r // SUBLANES + 1, r % SUBLANES:r % SUBLANES + 1, :],
                                           (n, sub, w)))
        return pieces[0] if len(pieces) == 1 else jnp.concatenate(pieces, axis=0)
    srow = lax.broadcasted_iota(jnp.int32, b3.shape, 1)
    if L == SUBLANES:
        r = half if rev else half - 1
        return jnp.broadcast_to(b3[:, r:r + 1, :], b3.shape)
    if L == 4:
        r = half if rev else half - 1
        lo = jnp.broadcast_to(b3[:, r:r + 1, :], b3.shape)
        hi = jnp.broadcast_to(b3[:, r + 4:r + 5, :], b3.shape)
        return jnp.where(srow < 4, lo, hi)
    odd = (srow % 2) == 1
    if rev:
        return jnp.where(odd, b3, pltpu.roll(b3, SUBLANES - 1, 1))
    return jnp.where(odd, pltpu.roll(b3, 1, 1), b3)


def _pair_blockdiag(x):
    zero = jnp.zeros((CHUNK, A_DK), x.dtype)
    return jnp.concatenate([jnp.concatenate([x[:, :A_DK], zero], axis=1),
                            jnp.concatenate([zero, x[:, A_DK:]], axis=1)], axis=0)


def _hgrn_chunk(q, k, v, b, s_ref, d, rev, m_ref, sgn_ref):
    tot = b[0:1, :] if rev else b[CHUNK - 1:CHUNK, :]
    q0 = (q * jnp.exp2(b)).astype(BF16)
    k2 = (k * jnp.exp2(tot - b)).astype(BF16)
    vb = v.astype(BF16)
    dec = jnp.exp2(tot)
    inter = []
    for h in range(A_HEADS):
        sl = slice(h * A_DK, (h + 1) * A_DK)
        s_t = s_ref[d, h]
        inter.append(_dot_nt(q0[:, sl], s_t.astype(BF16)))
        ds = lax.dot_general(vb[:, sl], k2[:, sl], TN_DIMS, preferred_element_type=F32)
        s_ref[d, h] = dec[:, sl] * s_t + ds
    pairs = [slice(p * 2 * A_DK, (p + 1) * 2 * A_DK) for p in range(A_HEADS // 2)]
    b3 = b.reshape(CHUNK // SUBLANES, SUBLANES, A_WIDTH)
    qs = q.astype(BF16)
    ks = k.astype(BF16)
    acc = [_dot_nt(qs[:, p], _pair_blockdiag(ks[:, p])) * m_ref[d, 0] for p in pairs]
    for li, L in enumerate(LEVELS):
        sgn = jnp.concatenate([sgn_ref[d, li]] * A_HEADS, axis=1).reshape(b3.shape)
        e = jnp.exp2((b3 - _level_ref(b3, L, rev)) * sgn).reshape(CHUNK, A_WIDTH)
        qs = (q * e).astype(BF16)
        ks = (k * e).astype(BF16)
        acc = [a + _dot_nt(qs[:, p], _pair_blockdiag(ks[:, p])) * m_ref[d, 1 + li] for a, p in zip(acc, pairs)]
    intra = [_dot(a.astype(BF16), _pair_blockdiag(vb[:, p])) for a, p in zip(acc, pairs)]
    return jnp.concatenate(intra, axis=1) + jnp.concatenate(inter, axis=1)


def _hgrn_body(layer, tbl_ref, qf_ref, zf_ref, vf_ref, qb_ref, zb_ref, vb_ref, lbl_ref, m_ref, sgn_ref,
               of_ref, ob_ref, s_ref):
    step = pl.program_id(0)

    @pl.when(tbl_ref[2, step] == 1)
    def _():
        s_ref[...] = jnp.zeros_like(s_ref)

    n_even = lbl_ref.shape[0]
    logits = [lbl_ref[i] for i in range(n_even)]
    mx = functools.reduce(jnp.maximum, logits)
    ex = [jnp.exp(l - mx) for l in logits]
    den = functools.reduce(lambda x, y: x + y, ex)
    ps = [e / den for e in ex]
    lbs = functools.reduce(lambda x, y: x + y, ps[:layer + 1]) - ps[0]

    nc = HGRN_BLOCK // CHUNK
    refs = ((qf_ref, zf_ref, vf_ref, of_ref), (qb_ref, zb_ref, vb_ref, ob_ref))
    tris = (_tri(False), _tri(True))

    def chunk(c, carry):
        prep = []
        for d, rev in enumerate((False, True)):
            q_ref, z_ref, v_ref, _ = refs[d]
            r0 = pl.multiple_of(((nc - 1 - c) if rev else c) * CHUNK, CHUNK)
            rows = pl.ds(r0, CHUNK)
            qa = q_ref[rows, :]
            z = z_ref[rows, :]
            lb = lbs[d:d + 1, :]
            q = qa * _sigmoid(qa)
            f = lb + (1.0 - lb) * _sigmoid(z)
            k = 1.0 - f
            lf = jnp.log(f)
            hi = lf.astype(BF16)
            r1 = lf - hi.astype(F32)
            mid = r1.astype(BF16)
            lo = (r1 - mid.astype(F32)).astype(BF16)
            cs = _dot(tris[d], jnp.concatenate([hi, mid, lo], axis=1))
            b = (cs[:, :A_WIDTH] + cs[:, A_WIDTH:2 * A_WIDTH] + cs[:, 2 * A_WIDTH:]) * LOG2E
            prep.append((rows, q, k, v_ref[rows, :], b))
        for d, rev in enumerate((False, True)):
            rows, q, k, v, b = prep[d]
            refs[d][3][rows, :] = _hgrn_chunk(q, k, v, b, s_ref, d, rev, m_ref, sgn_ref)
        return carry

    lax.fori_loop(0, nc, chunk, 0)


def _hgrn_table(segs):
    rows = []
    for off, batch, seq in segs:
        nblk = seq // HGRN_BLOCK
        for bi in range(batch):
            base = (off + bi * seq) // HGRN_BLOCK
            for i in range(nblk):
                rows.append((base + i, base + nblk - 1 - i, int(i == 0)))
    return np.asarray(rows, np.int32).T.copy()


def _hgrn(pa, lb_logits, layer, segs):
    t = pa.shape[0]
    tbl = _hgrn_table(segs)
    steps = tbl.shape[1]
    masks = _hgrn_masks()
    signs = _hgrn_signs()

    def spec(col, which):
        return pl.BlockSpec((HGRN_BLOCK, A_WIDTH), lambda i, tb: (tb[which, i], col))

    grid_spec = pltpu.PrefetchScalarGridSpec(
        num_scalar_prefetch=1,
        grid=(steps,),
        in_specs=[spec(0, 0), spec(1, 0), spec(3, 0), spec(0, 1), spec(2, 1), spec(3, 1),
                  pl.BlockSpec(lb_logits.shape, lambda i, tb: (0, 0, 0)),
                  pl.BlockSpec(masks.shape, lambda i, tb: (0, 0, 0, 0)),
                  pl.BlockSpec(signs.shape, lambda i, tb: (0, 0, 0, 0))],
        out_specs=[pl.BlockSpec((HGRN_BLOCK, A_WIDTH), lambda i, tb: (tb[0, i], 0)),
                   pl.BlockSpec((HGRN_BLOCK, A_WIDTH), lambda i, tb: (tb[1, i], 0))],
        scratch_shapes=[pltpu.VMEM((2, A_HEADS, A_DK, A_DK), F32)],
    )
    return pl.pallas_call(
        functools.partial(_hgrn_body, layer),
        grid_spec=grid_spec,
        out_shape=[jax.ShapeDtypeStruct((t, A_WIDTH), F32),
                   jax.ShapeDtypeStruct((t, A_WIDTH), F32)],
        compiler_params=pltpu.CompilerParams(dimension_semantics=("arbitrary",),
                                             vmem_limit_bytes=VMEM_LIMIT),
        name="hgrn",
    )(jnp.asarray(tbl), pa, pa, pa, pa, pa, pa, lb_logits, jnp.asarray(masks), jnp.asarray(signs))


def _na_build_bias(rpb_ref, bias_ref):
    n_ro = 2 * NA_KR - 1
    n_co = 2 * NA_KC - 1
    qc = lax.broadcasted_iota(jnp.int32, (GRID_W, GRID_W), 0)
    kc = lax.broadcasted_iota(jnp.int32, (GRID_W, GRID_W), 1)
    co = jnp.clip(kc - qc, -(NA_KC - 1), NA_KC - 1) + NA_KC - 1
    ws = jnp.clip(qc - NA_KC // 2, 0, GRID_W - NA_KC)
    valid = (kc >= ws) & (kc < ws + NA_KC)

    def build(hr, carry):
        h = hr // n_ro
        ro = hr % n_ro
        t = jnp.full((GRID_W, GRID_W), NEG, F32)
        for j in range(n_co):
            t = jnp.where(co == j, rpb_ref[hr * n_co + j], t)
        t = jnp.where(valid, t, NEG)
        for i in range(NA_KR):
            dd = ro - i

            @pl.when((dd >= 0) & (dd < NA_KR))
            def _():
                bias_ref[h, dd, :, i * GRID_W:(i + 1) * GRID_W] = t
        return carry

    lax.fori_loop(0, NA_HEADS * n_ro, build, 0)


def _na_body(tbl_ref, rpb_ref, q_ref, k_ref, v_ref, o_ref, bias_ref):
    step = pl.program_id(0)

    @pl.when(step == 0)
    def _():
        _na_build_bias(rpb_ref, bias_ref)

    r0 = tbl_ref[2, step]
    w0 = tbl_ref[3, step]
    n_rows = tbl_ref[4, step]
    lane = lax.broadcasted_iota(jnp.int32, (GRID_W, 2 * NA_HD), 1)
    low = lane < NA_HD
    nkeys = NA_KR * GRID_W
    scale = jnp.asarray(NA_HD ** -0.5, BF16)

    def row(rr, carry):
        r = r0 + rr
        row_start = jnp.clip(r - NA_KR // 2, 0, n_rows - NA_KR)
        dd = row_start - r + (NA_KR - 1)
        k0 = pl.multiple_of((row_start - w0) * GRID_W, GRID_W)
        q0 = pl.multiple_of(rr * GRID_W, GRID_W)
        pairs = [slice(hp * 2 * NA_HD, (hp + 1) * 2 * NA_HD) for hp in range(NA_HEADS // 2)]
        scores = []
        for lanes in pairs:
            qp = q_ref[pl.ds(q0, GRID_W), lanes] * scale
            kw = k_ref[pl.ds(k0, nkeys), lanes]
            zero = jnp.zeros_like(qp)
            q2 = jnp.concatenate([jnp.where(low, qp, zero), jnp.where(low, zero, qp)], axis=0)
            scores.append(_dot_nt(q2, kw))
        probs = []
        for hp, s in enumerate(scores):
            s = s + jnp.concatenate([bias_ref[2 * hp, dd], bias_ref[2 * hp + 1, dd]], axis=0)
            m = jnp.max(s, axis=-1, keepdims=True)
            p = jnp.exp(s - m)
            probs.append((p.astype(BF16), jnp.sum(p, axis=-1, keepdims=True)))
        for lanes, (p, l) in zip(pairs, probs):
            pv = _dot(p, v_ref[pl.ds(k0, nkeys), lanes]) / l
            o = jnp.where(low, pv[:GRID_W], pv[GRID_W:])
            o_ref[pl.ds(q0, GRID_W), lanes] = o.astype(o_ref.dtype)
        return carry

    lax.fori_loop(0, NA_ROWS, row, 0)


def _na_table(segs):
    blk = NA_ROWS * GRID_W
    rows = []
    for off, batch, seq in segs:
        n_rows = seq // GRID_W
        assert n_rows >= NA_WIN
        for bi in range(batch):
            base = off + bi * seq
            for i in range(seq // blk):
                r0 = i * NA_ROWS
                w0 = min(max(r0 - NA_ROWS, 0), n_rows - NA_WIN)
                rows.append((base // blk + i, base // GRID_W + w0, r0, w0, n_rows))
    return np.asarray(rows, np.int32).T.copy()


def _na(pn, rpb, segs):
    t = pn.shape[0]
    tbl = _na_table(segs)
    steps = tbl.shape[1]
    blk = NA_ROWS * GRID_W

    def window(col):
        return pl.BlockSpec((pl.Element(NA_WIN * GRID_W), pl.Element(NA_WIDTH)),
                            lambda i, tb: (tb[1, i] * GRID_W, col * NA_WIDTH))

    grid_spec = pltpu.PrefetchScalarGridSpec(
        num_scalar_prefetch=1,
        grid=(steps,),
        in_specs=[pl.BlockSpec(memory_space=pltpu.SMEM),
                  pl.BlockSpec((blk, NA_WIDTH), lambda i, tb: (tb[0, i], 0)),
                  window(1), window(2)],
        out_specs=pl.BlockSpec((blk, NA_WIDTH), lambda i, tb: (tb[0, i], 0)),
        scratch_shapes=[pltpu.VMEM((NA_HEADS, NA_KR, GRID_W, NA_KR * GRID_W), F32)],
    )
    return pl.pallas_call(
        _na_body,
        grid_spec=grid_spec,
        out_shape=jax.ShapeDtypeStruct((t, NA_WIDTH), BF16),
        compiler_params=pltpu.CompilerParams(dimension_semantics=("arbitrary",),
                                             vmem_limit_bytes=VMEM_LIMIT),
        name="natten",
    )(jnp.asarray(tbl), rpb.astype(F32).reshape(-1), pn, pn, pn)


def _ffn_tail(x1, gf_ref, wi_ref, wo_ref, gfin_ref, o_ref, final):
    h = _rms(x1, gf_ref[...]).astype(BF16)
    acc = x1
    for c in range(0, D_FF, FF_CHUNK):
        g = _dot(h, wi_ref[:, c:c + FF_CHUNK])
        u = _dot(h, wi_ref[:, D_FF + c:D_FF + c + FF_CHUNK])
        a = (g * _sigmoid(g) * u).astype(BF16)
        acc = acc + _dot(a, wo_ref[c:c + FF_CHUNK, :])
    if final:
        acc = _rms(acc, gfin_ref[...])
    o_ref[...] = acc


def _post_even_body(final, x_ref, of_ref, ob_ref, ga_ref, n_ref, gain_ref, wm_ref, gf_ref, wi_ref,
                    wo_ref, gfin_ref, o_ref):
    o = of_ref[...] + ob_ref[...]
    parts = []
    for h in range(A_HEADS):
        oh = o[:, h * A_DK:(h + 1) * A_DK]
        parts.append(oh * lax.rsqrt(jnp.mean(oh * oh, axis=-1, keepdims=True) + EPS))
    on = jnp.concatenate(parts, axis=1) * gain_ref[...]
    g = ga_ref[...]
    oa = (on * (g * _sigmoid(g))).astype(BF16)
    mix = _dot(oa, wm_ref[0:A_WIDTH, :]) + _dot(n_ref[...], wm_ref[A_WIDTH:A_WIDTH + NA_WIDTH, :])
    _ffn_tail(x_ref[...] + mix, gf_ref, wi_ref, wo_ref, gfin_ref, o_ref, final)


def _post_even(x, o_f, o_b, pa, o_n, gain, w_mix, g_ffn, w_ffn_in, w_ffn_out, g_final, final):
    t = x.shape[0]
    tok = lambda w, col=0: pl.BlockSpec((TM, w), lambda i: (i, col))
    return pl.pallas_call(
        functools.partial(_post_even_body, final),
        grid=(t // TM,),
        in_specs=[tok(D_MODEL), tok(A_WIDTH), tok(A_WIDTH), tok(A_WIDTH, 4), tok(NA_WIDTH),
                  _resident((1, A_WIDTH)), _resident((A_WIDTH + NA_WIDTH, D_MODEL)),
                  _resident((1, D_MODEL)), _resident((D_MODEL, 2 * D_FF)), _resident((D_FF, D_MODEL)),
                  _resident((1, D_MODEL))],
        out_specs=tok(D_MODEL),
        out_shape=jax.ShapeDtypeStruct((t, D_MODEL), F32),
        compiler_params=pltpu.CompilerParams(dimension_semantics=("arbitrary",),
                                             vmem_limit_bytes=VMEM_LIMIT),
        name="post_even",
    )(x, o_f, o_b, pa, o_n, gain, w_mix, g_ffn, w_ffn_in, w_ffn_out, g_final)


def _odd_body(final, tbl_ref, x_ref, xp_ref, xn_ref, gm_ref, wc_ref, cw_ref, wco_ref, gf_ref, wi_ref,
              wo_ref, gfin_ref, o_ref):
    step = pl.program_id(0)
    x = x_ref[...]
    gm = gm_ref[...]
    h = _rms(x, gm).astype(BF16)
    z = _dot(h, wc_ref[:, D_MODEL:2 * D_MODEL]) * _dot(h, wc_ref[:, 2 * D_MODEL:3 * D_MODEL])
    hh = _rms(jnp.concatenate([xp_ref[...], xn_ref[...]], axis=0), gm).astype(BF16)
    zh = _dot(hh, wc_ref[:, D_MODEL:2 * D_MODEL]) * _dot(hh, wc_ref[:, 2 * D_MODEL:3 * D_MODEL])
    z_prev = jnp.where(tbl_ref[2, step] == 1, 0.0, zh[HALO - 1:HALO, :])
    z_next = jnp.where(tbl_ref[3, step] == 1, 0.0, zh[HALO:HALO + 1, :])
    row = lax.broadcasted_iota(jnp.int32, (TM, 1), 0)
    z_dn = jnp.where(row == 0, z_prev, pltpu.roll(z, 1, 0))
    z_up = jnp.where(row == TM - 1, z_next, pltpu.roll(z, TM - 1, 0))
    conv = z_dn * cw_ref[0:1, :] + z * cw_ref[1:2, :] + z_up * cw_ref[2:3, :]
    y = (_dot(h, wc_ref[:, 0:D_MODEL]) * conv).astype(BF16)
    _ffn_tail(x + _dot(y, wco_ref[...]), gf_ref, wi_ref, wo_ref, gfin_ref, o_ref, final)


def _odd_table(segs, t):
    rows = []
    starts = set()
    ends = set()
    for off, batch, seq in segs:
        for bi in range(batch):
            starts.add(off + bi * seq)
            ends.add(off + (bi + 1) * seq)
    for i in range(t // TM):
        t0 = i * TM
        first = int(t0 in starts)
        last = int(t0 + TM in ends)
        rows.append((max(t0 // HALO - 1, 0), min((t0 + TM) // HALO, t // HALO - 1), first, last))
    return np.asarray(rows, np.int32).T.copy()


def _odd(x, g_mix, w_conv_in, conv_w, w_conv_out, g_ffn, w_ffn_in, w_ffn_out, g_final, final, segs):
    t = x.shape[0]
    tbl = _odd_table(segs, t)
    res = lambda shape: pl.BlockSpec(shape, lambda i, tb: (0,) * len(shape), pipeline_mode=pl.Buffered(1))
    grid_spec = pltpu.PrefetchScalarGridSpec(
        num_scalar_prefetch=1,
        grid=(t // TM,),
        in_specs=[pl.BlockSpec((TM, D_MODEL), lambda i, tb: (i, 0)),
                  pl.BlockSpec((HALO, D_MODEL), lambda i, tb: (tb[0, i], 0)),
                  pl.BlockSpec((HALO, D_MODEL), lambda i, tb: (tb[1, i], 0)),
                  res((1, D_MODEL)), res((D_MODEL, 3 * D_MODEL)), res((CONV_W, D_MODEL)),
                  res((D_MODEL, D_MODEL)), res((1, D_MODEL)), res((D_MODEL, 2 * D_FF)),
                  res((D_FF, D_MODEL)), res((1, D_MODEL))],
        out_specs=pl.BlockSpec((TM, D_MODEL), lambda i, tb: (i, 0)),
    )
    return pl.pallas_call(
        functools.partial(_odd_body, final),
        grid_spec=grid_spec,
        out_shape=jax.ShapeDtypeStruct((t, D_MODEL), F32),
        compiler_params=pltpu.CompilerParams(dimension_semantics=("arbitrary",),
                                             vmem_limit_bytes=VMEM_LIMIT),
        name="odd_layer",
    )(jnp.asarray(tbl), x, x, x, g_mix, w_conv_in, conv_w, w_conv_out, g_ffn, w_ffn_in, w_ffn_out, g_final)


def _trunk(xs, norm_mix, norm_ffn, norm_final, even_w_in, even_w_out, hgrn_lb_logits, hgrn_norm,
           na_rpb, conv_w_in, conv_w, conv_w_out, ffn_w_in, ffn_w_out):
    depth = norm_mix.shape[0]
    segs = []
    off = 0
    for a in xs:
        segs.append((off, a.shape[0], a.shape[1]))
        off += a.shape[0] * a.shape[1]
    x = jnp.concatenate([a.reshape(-1, D_MODEL) for a in xs], axis=0)
    row = lambda v: v.reshape(1, -1).astype(F32)
    g_final = row(norm_final)
    for l in range(depth):
        final = l == depth - 1
        w_fi = ffn_w_in[l].astype(BF16)
        w_fo = ffn_w_out[l].astype(BF16)
        if l % 2 == 0:
            e = l // 2
            pa, pn = _in_even(x, row(norm_mix[l]), even_w_in[e].astype(BF16))
            o_f, o_b = _hgrn(pa, hgrn_lb_logits.astype(F32), e, segs)
            o_n = _na(pn, na_rpb[e], segs)
            x = _post_even(x, o_f, o_b, pa, o_n, row(hgrn_norm[e]), even_w_out[e].astype(BF16),
                           row(norm_ffn[l]), w_fi, w_fo, g_final, final)
        else:
            o = l // 2
            x = _odd(x, row(norm_mix[l]), conv_w_in[o].astype(BF16), conv_w[o].astype(F32),
                     conv_w_out[o].astype(BF16), row(norm_ffn[l]), w_fi, w_fo, g_final, final, segs)
    outs = []
    for (off, batch, seq) in segs:
        outs.append(x[off:off + batch * seq].reshape(batch, seq, D_MODEL))
    return tuple(outs)


def kernel(x_prompt, x_sample, norm_mix, norm_ffn, norm_final, even_w_in, even_w_out, hgrn_lb_logits,
           hgrn_norm, na_rpb, conv_w_in, conv_w, conv_w_out, ffn_w_in, ffn_w_out):
    return _trunk([x_prompt, x_sample], norm_mix, norm_ffn, norm_final, even_w_in, even_w_out,
                  hgrn_lb_logits, hgrn_norm, na_rpb, conv_w_in, conv_w, conv_w_out, ffn_w_in, ffn_w_out)
```

```python
import functools

import numpy as np
import jax
import jax.numpy as jnp
from jax import lax
from jax.experimental import pallas as pl
from jax.experimental.pallas import tpu as pltpu

F32 = jnp.float32
BF16 = jnp.bfloat16

SUBLANES = 8
D_MODEL = 1024
EPS = 1e-6
NEG = -1e30
LOG2E = 1.4426950408889634
GRID_W = 64
A_WIDTH = 512
A_HEADS = 4
A_DK = 128
CHUNK = 64
LEVELS = (2, 4, 8, 16, 32, 64)
NA_WIDTH = 512
NA_HEADS = 8
NA_HD = 64
NA_KR = 8
NA_KC = 16
CONV_W = 3
D_FF = 2816
FF_CHUNK = 256
PA_W = 5 * A_WIDTH
PN_W = 3 * NA_WIDTH
EVEN_IN = PA_W + PN_W

TM = 512
HGRN_BLOCK = 256
NA_ROWS = 8
NA_WIN = 3 * NA_ROWS
HALO = 8
VMEM_LIMIT = 56 * 1024 * 1024

NT_DIMS = (((1,), (1,)), ((), ()))
TN_DIMS = (((0,), (0,)), ((), ()))


def _resident(shape):
    nd = len(shape)
    return pl.BlockSpec(shape, lambda *_: (0,) * nd, pipeline_mode=pl.Buffered(1))


def _rms(x, g):
    return x * lax.rsqrt(jnp.mean(x * x, axis=-1, keepdims=True) + EPS) * g


def _sigmoid(x):
    return jax.nn.sigmoid(x)


def _dot(a, b):
    return jnp.dot(a, b, preferred_element_type=F32)


def _dot_nt(a, b):
    return lax.dot_general(a, b, NT_DIMS, preferred_element_type=F32)


def _tile_starts(parts):
    starts = [0]
    for a in parts:
        starts.append(starts[-1] + a.shape[0] // TM)
    return tuple(starts)


def _part_specs(parts):
    starts = _tile_starts(parts)
    return [pl.BlockSpec((TM, D_MODEL),
                         lambda i, *_, s=starts[k], n=starts[k + 1] - starts[k]: (jnp.clip(i - s, 0, n - 1), 0))
            for k in range(len(parts))]


def _select_part(starts, refs):
    step = pl.program_id(0)
    x = refs[-1][...]
    for k in range(len(refs) - 2, -1, -1):
        x = jnp.where(step < starts[k + 1], refs[k][...], x)
    return x


def _in_even_body(starts, *refs):
    n = len(starts) - 1
    g_ref, w_ref, pa_ref, pn_ref = refs[n:]
    h = _rms(_select_part(starts, refs[:n]), g_ref[...]).astype(BF16)
    for c in range(0, PA_W, 512):
        pa_ref[:, c:c + 512] = _dot(h, w_ref[:, c:c + 512])
    for c in range(0, PN_W, 512):
        pn_ref[:, c:c + 512] = _dot(h, w_ref[:, PA_W + c:PA_W + c + 512]).astype(BF16)


def _in_even(parts, g, w):
    starts = _tile_starts(parts)
    t = starts[-1] * TM
    return pl.pallas_call(
        functools.partial(_in_even_body, starts),
        grid=(starts[-1],),
        in_specs=_part_specs(parts) + [_resident((1, D_MODEL)), _resident((D_MODEL, EVEN_IN))],
        out_specs=[pl.BlockSpec((TM, PA_W), lambda i: (i, 0)),
                   pl.BlockSpec((TM, PN_W), lambda i: (i, 0))],
        out_shape=[jax.ShapeDtypeStruct((t, PA_W), F32),
                   jax.ShapeDtypeStruct((t, PN_W), BF16)],
        compiler_params=pltpu.CompilerParams(dimension_semantics=("arbitrary",),
                                             vmem_limit_bytes=VMEM_LIMIT),
        name="in_even",
    )(*parts, g, w)


def _hgrn_masks():
    t = np.arange(CHUNK)[:, None]
    s = np.arange(CHUNK)[None, :]
    out = np.zeros((2, 1 + len(LEVELS), CHUNK, CHUNK), np.float32)
    for d, rev in enumerate((False, True)):
        out[d, 0] = (t == s)
        for li, L in enumerate(LEVELS):
            half = L // 2
            same = (t // L) == (s // L)
            if rev:
                m = same & ((t % L) < half) & ((s % L) >= half)
            else:
                m = same & ((t % L) >= half) & ((s % L) < half)
            out[d, 1 + li] = m
    return np.tile(out, (1, 1, 1, 2))


def _hgrn_signs():
    t = np.arange(CHUNK)
    out = np.zeros((2, len(LEVELS), CHUNK), np.float32)
    for d, rev in enumerate((False, True)):
        for li, L in enumerate(LEVELS):
            second = (t % L) >= L // 2
            out[d, li] = np.where(second != rev, 1.0, -1.0)
    return np.ascontiguousarray(np.broadcast_to(out[..., None], out.shape + (A_DK,)))


def _tri(rev):
    t = lax.broadcasted_iota(jnp.int32, (CHUNK, CHUNK), 0)
    s = lax.broadcasted_iota(jnp.int32, (CHUNK, CHUNK), 1)
    return jnp.where((s >= t) if rev else (s <= t), 1.0, 0.0).astype(BF16)


def _level_ref(b3, L, rev):
    g, sub, w = b3.shape
    half = L // 2
    if L >= 2 * SUBLANES:
        n = L // SUBLANES
        pieces = []
        for blk in range(CHUNK // L):
            r = blk * L + (half if rev else half - 1)
            pieces.append(jnp.broadcast_to(b3[r // SUBLANES:r // SUBLANES + 1, r % SUBLANES:r % SUBLANES + 1, :],
                                           (n, sub, w)))
        return pieces[0] if len(pieces) == 1 else jnp.concatenate(pieces, axis=0)
    srow = lax.broadcasted_iota(jnp.int32, b3.shape, 1)
    if L == SUBLANES:
        r = half if rev else half - 1
        return jnp.broadcast_to(b3[:, r:r + 1, :], b3.shape)
    if L == 4:
        r = half if rev else half - 1
        lo = jnp.broadcast_to(b3[:, r:r + 1, :], b3.shape)
        hi = jnp.broadcast_to(b3[:, r + 4:r + 5, :], b3.shape)
        return jnp.where(srow < 4, lo, hi)
    odd = (srow % 2) == 1
    if rev:
        return jnp.where(odd, b3, pltpu.roll(b3, SUBLANES - 1, 1))
    return jnp.where(odd, pltpu.roll(b3, 1, 1), b3)


def _pair_blockdiag(x):
    zero = jnp.zeros((CHUNK, A_DK), x.dtype)
    return jnp.concatenate([jnp.concatenate([x[:, :A_DK], zero], axis=1),
                            jnp.concatenate([zero, x[:, A_DK:]], axis=1)], axis=0)


def _hgrn_chunk(q, k, v, b, s_ref, d, rev, m_ref, sgn_ref):
    tot = b[0:1, :] if rev else b[CHUNK - 1:CHUNK, :]
    q0 = (q * jnp.exp2(b)).astype(BF16)
    k2 = (k * jnp.exp2(tot - b)).astype(BF16)
    vb = v.astype(BF16)
    dec = jnp.exp2(tot)
    inter = []
    for h in range(A_HEADS):
        sl = slice(h * A_DK, (h + 1) * A_DK)
        s_t = s_ref[d, h]
        inter.append(_dot_nt(q0[:, sl], s_t.astype(BF16)))
        ds = lax.dot_general(vb[:, sl], k2[:, sl], TN_DIMS, preferred_element_type=F32)
        s_ref[d, h] = dec[:, sl] * s_t + ds
    pairs = [slice(p * 2 * A_DK, (p + 1) * 2 * A_DK) for p in range(A_HEADS // 2)]
    b3 = b.reshape(CHUNK // SUBLANES, SUBLANES, A_WIDTH)
    qs = q.astype(BF16)
    ks = k.astype(BF16)
    acc = [_dot_nt(qs[:, p], _pair_blockdiag(ks[:, p])) * m_ref[d, 0] for p in pairs]
    for li, L in enumerate(LEVELS):
        sgn = jnp.concatenate([sgn_ref[d, li]] * A_HEADS, axis=1).reshape(b3.shape)
        e = jnp.exp2((b3 - _level_ref(b3, L, rev)) * sgn).reshape(CHUNK, A_WIDTH)
        qs = (q * e).astype(BF16)
        ks = (k * e).astype(BF16)
        acc = [a + _dot_nt(qs[:, p], _pair_blockdiag(ks[:, p])) * m_ref[d, 1 + li] for a, p in zip(acc, pairs)]
    intra = [_dot(a.astype(BF16), _pair_blockdiag(vb[:, p])) for a, p in zip(acc, pairs)]
    return jnp.concatenate(intra, axis=1) + jnp.concatenate(inter, axis=1)


def _hgrn_body(layer, tbl_ref, qf_ref, zf_ref, vf_ref, qb_ref, zb_ref, vb_ref, lbl_ref, m_ref, sgn_ref,
               of_ref, ob_ref, s_ref):
    step = pl.program_id(0)

    @pl.when(tbl_ref[2, step] == 1)
    def _():
        s_ref[...] = jnp.zeros_like(s_ref)

    n_even = lbl_ref.shape[0]
    logits = [lbl_ref[i] for i in range(n_even)]
    mx = functools.reduce(jnp.maximum, logits)
    ex = [jnp.exp(l - mx) for l in logits]
    den = functools.reduce(lambda x, y: x + y, ex)
    ps = [e / den for e in ex]
    lbs = functools.reduce(lambda x, y: x + y, ps[:layer + 1]) - ps[0]

    nc = HGRN_BLOCK // CHUNK
    refs = ((qf_ref, zf_ref, vf_ref, of_ref), (qb_ref, zb_ref, vb_ref, ob_ref))
    tris = (_tri(False), _tri(True))

    def chunk(c, carry):
        prep = []
        for d, rev in enumerate((False, True)):
            q_ref, z_ref, v_ref, _ = refs[d]
            r0 = pl.multiple_of(((nc - 1 - c) if rev else c) * CHUNK, CHUNK)
            rows = pl.ds(r0, CHUNK)
            qa = q_ref[rows, :]
            z = z_ref[rows, :]
            lb = lbs[d:d + 1, :]
            q = qa * _sigmoid(qa)
            f = lb + (1.0 - lb) * _sigmoid(z)
            k = 1.0 - f
            lf = jnp.log(f)
            hi = lf.astype(BF16)
            r1 = lf - hi.astype(F32)
            mid = r1.astype(BF16)
            lo = (r1 - mid.astype(F32)).astype(BF16)
            cs = _dot(tris[d], jnp.concatenate([hi, mid, lo], axis=1))
            b = (cs[:, :A_WIDTH] + cs[:, A_WIDTH:2 * A_WIDTH] + cs[:, 2 * A_WIDTH:]) * LOG2E
            prep.append((rows, q, k, v_ref[rows, :], b))
        for d, rev in enumerate((False, True)):
            rows, q, k, v, b = prep[d]
            refs[d][3][rows, :] = _hgrn_chunk(q, k, v, b, s_ref, d, rev, m_ref, sgn_ref)
        return carry

    lax.fori_loop(0, nc, chunk, 0)


def _hgrn_table(segs):
    rows = []
    for off, batch, seq in segs:
        nblk = seq // HGRN_BLOCK
        for bi in range(batch):
            base = (off + bi * seq) // HGRN_BLOCK
            for i in range(nblk):
                rows.append((base + i, base + nblk - 1 - i, int(i == 0)))
    return np.asarray(rows, np.int32).T.copy()


def _hgrn(pa, lb_logits, layer, segs):
    t = pa.shape[0]
    tbl = _hgrn_table(segs)
    steps = tbl.shape[1]
    masks = _hgrn_masks()
    signs = _hgrn_signs()

    def spec(col, which):
        return pl.BlockSpec((HGRN_BLOCK, A_WIDTH), lambda i, tb: (tb[which, i], col))

    grid_spec = pltpu.PrefetchScalarGridSpec(
        num_scalar_prefetch=1,
        grid=(steps,),
        in_specs=[spec(0, 0), spec(1, 0), spec(3, 0), spec(0, 1), spec(2, 1), spec(3, 1),
                  pl.BlockSpec(lb_logits.shape, lambda i, tb: (0, 0, 0)),
                  pl.BlockSpec(masks.shape, lambda i, tb: (0, 0, 0, 0)),
                  pl.BlockSpec(signs.shape, lambda i, tb: (0, 0, 0, 0))],
        out_specs=[pl.BlockSpec((HGRN_BLOCK, A_WIDTH), lambda i, tb: (tb[0, i], 0)),
                   pl.BlockSpec((HGRN_BLOCK, A_WIDTH), lambda i, tb: (tb[1, i], 0))],
        scratch_shapes=[pltpu.VMEM((2, A_HEADS, A_DK, A_DK), F32)],
    )
    return pl.pallas_call(
        functools.partial(_hgrn_body, layer),
        grid_spec=grid_spec,
        out_shape=[jax.ShapeDtypeStruct((t, A_WIDTH), F32),
                   jax.ShapeDtypeStruct((t, A_WIDTH), F32)],
        compiler_params=pltpu.CompilerParams(dimension_semantics=("arbitrary",),
                                             vmem_limit_bytes=VMEM_LIMIT),
        name="hgrn",
    )(jnp.asarray(tbl), pa, pa, pa, pa, pa, pa, lb_logits, jnp.asarray(masks), jnp.asarray(signs))


def _na_build_bias(rpb_ref, bias_ref):
    n_ro = 2 * NA_KR - 1
    n_co = 2 * NA_KC - 1
    qc = lax.broadcasted_iota(jnp.int32, (GRID_W, GRID_W), 0)
    kc = lax.broadcasted_iota(jnp.int32, (GRID_W, GRID_W), 1)
    co = jnp.clip(kc - qc, -(NA_KC - 1), NA_KC - 1) + NA_KC - 1
    ws = jnp.clip(qc - NA_KC // 2, 0, GRID_W - NA_KC)
    valid = (kc >= ws) & (kc < ws + NA_KC)

    def build(hr, carry):
        h = hr // n_ro
        ro = hr % n_ro
        t = jnp.full((GRID_W, GRID_W), NEG, F32)
        for j in range(n_co):
            t = jnp.where(co == j, rpb_ref[hr * n_co + j], t)
        t = jnp.where(valid, t, NEG)
        for i in range(NA_KR):
            dd = ro - i

            @pl.when((dd >= 0) & (dd < NA_KR))
            def _():
                bias_ref[h, dd, :, i * GRID_W:(i + 1) * GRID_W] = t
        return carry

    lax.fori_loop(0, NA_HEADS * n_ro, build, 0)


def _na_body(tbl_ref, rpb_ref, q_ref, k_ref, v_ref, o_ref, bias_ref):
    step = pl.program_id(0)

    @pl.when(step == 0)
    def _():
        _na_build_bias(rpb_ref, bias_ref)

    r0 = tbl_ref[2, step]
    w0 = tbl_ref[3, step]
    n_rows = tbl_ref[4, step]
    lane = lax.broadcasted_iota(jnp.int32, (GRID_W, 2 * NA_HD), 1)
    low = lane < NA_HD
    nkeys = NA_KR * GRID_W
    scale = jnp.asarray(NA_HD ** -0.5, BF16)

    def row(rr, carry):
        r = r0 + rr
        row_start = jnp.clip(r - NA_KR // 2, 0, n_rows - NA_KR)
        dd = row_start - r + (NA_KR - 1)
        k0 = pl.multiple_of((row_start - w0) * GRID_W, GRID_W)
        q0 = pl.multiple_of(rr * GRID_W, GRID_W)
        pairs = [slice(hp * 2 * NA_HD, (hp + 1) * 2 * NA_HD) for hp in range(NA_HEADS // 2)]
        scores = []
        for lanes in pairs:
            qp = q_ref[pl.ds(q0, GRID_W), lanes] * scale
            kw = k_ref[pl.ds(k0, nkeys), lanes]
            zero = jnp.zeros_like(qp)
            q2 = jnp.concatenate([jnp.where(low, qp, zero), jnp.where(low, zero, qp)], axis=0)
            scores.append(_dot_nt(q2, kw))
        probs = []
        for hp, s in enumerate(scores):
            s = s + jnp.concatenate([bias_ref[2 * hp, dd], bias_ref[2 * hp + 1, dd]], axis=0)
            m = jnp.max(s, axis=-1, keepdims=True)
            p = jnp.exp(s - m)
            probs.append((p.astype(BF16), jnp.sum(p, axis=-1, keepdims=True)))
        for lanes, (p, l) in zip(pairs, probs):
            pv = _dot(p, v_ref[pl.ds(k0, nkeys), lanes]) / l
            o = jnp.where(low, pv[:GRID_W], pv[GRID_W:])
            o_ref[pl.ds(q0, GRID_W), lanes] = o.astype(o_ref.dtype)
        return carry

    lax.fori_loop(0, NA_ROWS, row, 0)


def _na_table(segs):
    blk = NA_ROWS * GRID_W
    rows = []
    for off, batch, seq in segs:
        n_rows = seq // GRID_W
        assert n_rows >= NA_WIN
        for bi in range(batch):
            base = off + bi * seq
            for i in range(seq // blk):
                r0 = i * NA_ROWS
                w0 = min(max(r0 - NA_ROWS, 0), n_rows - NA_WIN)
                rows.append((base // blk + i, base // GRID_W + w0, r0, w0, n_rows))
    return np.asarray(rows, np.int32).T.copy()


def _na(pn, rpb, segs):
    t = pn.shape[0]
    tbl = _na_table(segs)
    steps = tbl.shape[1]
    blk = NA_ROWS * GRID_W

    def window(col):
        return pl.BlockSpec((pl.Element(NA_WIN * GRID_W), pl.Element(NA_WIDTH)),
                            lambda i, tb: (tb[1, i] * GRID_W, col * NA_WIDTH))

    grid_spec = pltpu.PrefetchScalarGridSpec(
        num_scalar_prefetch=1,
        grid=(steps,),
        in_specs=[pl.BlockSpec(memory_space=pltpu.SMEM),
                  pl.BlockSpec((blk, NA_WIDTH), lambda i, tb: (tb[0, i], 0)),
                  window(1), window(2)],
        out_specs=pl.BlockSpec((blk, NA_WIDTH), lambda i, tb: (tb[0, i], 0)),
        scratch_shapes=[pltpu.VMEM((NA_HEADS, NA_KR, GRID_W, NA_KR * GRID_W), F32)],
    )
    return pl.pallas_call(
        _na_body,
        grid_spec=grid_spec,
        out_shape=jax.ShapeDtypeStruct((t, NA_WIDTH), BF16),
        compiler_params=pltpu.CompilerParams(dimension_semantics=("arbitrary",),
                                             vmem_limit_bytes=VMEM_LIMIT),
        name="natten",
    )(jnp.asarray(tbl), rpb.astype(F32).reshape(-1), pn, pn, pn)


def _out_specs_shapes(out_tokens):
    starts = [0]
    for t in out_tokens:
        starts.append(starts[-1] + t // TM)
    specs = []
    for k in range(len(out_tokens)):
        s, n = starts[k], starts[k + 1] - starts[k]
        specs.append(pl.BlockSpec((TM, D_MODEL), lambda i, *_, s=s, n=n: (jnp.clip(i - s, 0, n - 1), 0)))
    shapes = [jax.ShapeDtypeStruct((t, D_MODEL), F32) for t in out_tokens]
    return tuple(starts), specs, shapes


def _store_parts(starts, out_refs, acc):
    if len(out_refs) == 1:
        out_refs[0][...] = acc
        return
    step = pl.program_id(0)
    for k, o_ref in enumerate(out_refs):
        @pl.when((step >= starts[k]) & (step < starts[k + 1]))
        def _(o_ref=o_ref):
            o_ref[...] = acc


def _ffn_tail(x1, gf_ref, wi_ref, wo_ref, gfin_ref, final):
    h = _rms(x1, gf_ref[...]).astype(BF16)
    acc = x1
    for c in range(0, D_FF, FF_CHUNK):
        g = _dot(h, wi_ref[:, c:c + FF_CHUNK])
        u = _dot(h, wi_ref[:, D_FF + c:D_FF + c + FF_CHUNK])
        a = (g * _sigmoid(g) * u).astype(BF16)
        acc = acc + _dot(a, wo_ref[c:c + FF_CHUNK, :])
    if final:
        acc = _rms(acc, gfin_ref[...])
    return acc


def _post_even_body(final, starts, out_starts, *refs):
    n = len(starts) - 1
    of_ref, ob_ref, ga_ref, n_ref, gain_ref, wm_ref, gf_ref, wi_ref, wo_ref, gfin_ref = refs[n:n + 10]
    o = of_ref[...] + ob_ref[...]
    parts = []
    for h in range(A_HEADS):
        oh = o[:, h * A_DK:(h + 1) * A_DK]
        parts.append(oh * lax.rsqrt(jnp.mean(oh * oh, axis=-1, keepdims=True) + EPS))
    on = jnp.concatenate(parts, axis=1) * gain_ref[...]
    g = ga_ref[...]
    oa = (on * (g * _sigmoid(g))).astype(BF16)
    mix = _dot(oa, wm_ref[0:A_WIDTH, :]) + _dot(n_ref[...], wm_ref[A_WIDTH:A_WIDTH + NA_WIDTH, :])
    acc = _ffn_tail(_select_part(starts, refs[:n]) + mix, gf_ref, wi_ref, wo_ref, gfin_ref, final)
    _store_parts(out_starts, refs[n + 10:], acc)


def _post_even(parts, o_f, o_b, pa, o_n, gain, w_mix, g_ffn, w_ffn_in, w_ffn_out, g_final, final, out_tokens):
    starts = _tile_starts(parts)
    out_starts, out_specs, out_shapes = _out_specs_shapes(out_tokens)
    tok = lambda w, col=0: pl.BlockSpec((TM, w), lambda i: (i, col))
    return pl.pallas_call(
        functools.partial(_post_even_body, final, starts, out_starts),
        grid=(starts[-1],),
        in_specs=_part_specs(parts) + [tok(A_WIDTH), tok(A_WIDTH), tok(A_WIDTH, 4), tok(NA_WIDTH),
                                       _resident((1, A_WIDTH)), _resident((A_WIDTH + NA_WIDTH, D_MODEL)),
                                       _resident((1, D_MODEL)), _resident((D_MODEL, 2 * D_FF)),
                                       _resident((D_FF, D_MODEL)), _resident((1, D_MODEL))],
        out_specs=out_specs,
        out_shape=out_shapes,
        compiler_params=pltpu.CompilerParams(dimension_semantics=("arbitrary",),
                                             vmem_limit_bytes=VMEM_LIMIT),
        name="post_even",
    )(*parts, o_f, o_b, pa, o_n, gain, w_mix, g_ffn, w_ffn_in, w_ffn_out, g_final)


def _odd_body(final, out_starts, tbl_ref, x_ref, xp_ref, xn_ref, gm_ref, wc_ref, cw_ref, wco_ref, gf_ref,
              wi_ref, wo_ref, gfin_ref, *out_refs):
    step = pl.program_id(0)
    x = x_ref[...]
    gm = gm_ref[...]
    h = _rms(x, gm).astype(BF16)
    z = _dot(h, wc_ref[:, D_MODEL:2 * D_MODEL]) * _dot(h, wc_ref[:, 2 * D_MODEL:3 * D_MODEL])
    hh = _rms(jnp.concatenate([xp_ref[...], xn_ref[...]], axis=0), gm).astype(BF16)
    zh = _dot(hh, wc_ref[:, D_MODEL:2 * D_MODEL]) * _dot(hh, wc_ref[:, 2 * D_MODEL:3 * D_MODEL])
    z_prev = jnp.where(tbl_ref[2, step] == 1, 0.0, zh[HALO - 1:HALO, :])
    z_next = jnp.where(tbl_ref[3, step] == 1, 0.0, zh[HALO:HALO + 1, :])
    row = lax.broadcasted_iota(jnp.int32, (TM, 1), 0)
    z_dn = jnp.where(row == 0, z_prev, pltpu.roll(z, 1, 0))
    z_up = jnp.where(row == TM - 1, z_next, pltpu.roll(z, TM - 1, 0))
    conv = z_dn * cw_ref[0:1, :] + z * cw_ref[1:2, :] + z_up * cw_ref[2:3, :]
    y = (_dot(h, wc_ref[:, 0:D_MODEL]) * conv).astype(BF16)
    acc = _ffn_tail(x + _dot(y, wco_ref[...]), gf_ref, wi_ref, wo_ref, gfin_ref, final)
    _store_parts(out_starts, out_refs, acc)


def _odd_table(segs, t):
    rows = []
    starts = set()
    ends = set()
    for off, batch, seq in segs:
        for bi in range(batch):
            starts.add(off + bi * seq)
            ends.add(off + (bi + 1) * seq)
    for i in range(t // TM):
        t0 = i * TM
        first = int(t0 in starts)
        last = int(t0 + TM in ends)
        rows.append((max(t0 // HALO - 1, 0), min((t0 + TM) // HALO, t // HALO - 1), first, last))
    return np.asarray(rows, np.int32).T.copy()


def _odd(x, g_mix, w_conv_in, conv_w, w_conv_out, g_ffn, w_ffn_in, w_ffn_out, g_final, final, segs, out_tokens):
    t = x.shape[0]
    tbl = _odd_table(segs, t)
    out_starts, out_specs, out_shapes = _out_specs_shapes(out_tokens)
    res = lambda shape: pl.BlockSpec(shape, lambda i, tb: (0,) * len(shape), pipeline_mode=pl.Buffered(1))
    grid_spec = pltpu.PrefetchScalarGridSpec(
        num_scalar_prefetch=1,
        grid=(t // TM,),
        in_specs=[pl.BlockSpec((TM, D_MODEL), lambda i, tb: (i, 0)),
                  pl.BlockSpec((HALO, D_MODEL), lambda i, tb: (tb[0, i], 0)),
                  pl.BlockSpec((HALO, D_MODEL), lambda i, tb: (tb[1, i], 0)),
                  res((1, D_MODEL)), res((D_MODEL, 3 * D_MODEL)), res((CONV_W, D_MODEL)),
                  res((D_MODEL, D_MODEL)), res((1, D_MODEL)), res((D_MODEL, 2 * D_FF)),
                  res((D_FF, D_MODEL)), res((1, D_MODEL))],
        out_specs=out_specs,
    )
    return pl.pallas_call(
        functools.partial(_odd_body, final, out_starts),
        grid_spec=grid_spec,
        out_shape=out_shapes,
        compiler_params=pltpu.CompilerParams(dimension_semantics=("arbitrary",),
                                             vmem_limit_bytes=VMEM_LIMIT),
        name="odd_layer",
    )(jnp.asarray(tbl), x, x, x, g_mix, w_conv_in, conv_w, w_conv_out, g_ffn, w_ffn_in, w_ffn_out, g_final)


def _trunk(xs, norm_mix, norm_ffn, norm_final, even_w_in, even_w_out, hgrn_lb_logits, hgrn_norm,
           na_rpb, conv_w_in, conv_w, conv_w_out, ffn_w_in, ffn_w_out):
    depth = norm_mix.shape[0]
    segs = []
    off = 0
    for a in xs:
        segs.append((off, a.shape[0], a.shape[1]))
        off += a.shape[0] * a.shape[1]
    parts = [a.reshape(-1, D_MODEL) for a in xs]
    part_tokens = [p.shape[0] for p in parts]
    total = off
    row = lambda v: v.reshape(1, -1).astype(F32)
    g_final = row(norm_final)
    for l in range(depth):
        final = l == depth - 1
        out_tokens = part_tokens if final else [total]
        w_fi = ffn_w_in[l].astype(BF16)
        w_fo = ffn_w_out[l].astype(BF16)
        if l % 2 == 0:
            e = l // 2
            pa, pn = _in_even(parts, row(norm_mix[l]), even_w_in[e].astype(BF16))
            o_f, o_b = _hgrn(pa, hgrn_lb_logits.astype(F32), e, segs)
            o_n = _na(pn, na_rpb[e], segs)
            outs = _post_even(parts, o_f, o_b, pa, o_n, row(hgrn_norm[e]), even_w_out[e].astype(BF16),
                              row(norm_ffn[l]), w_fi, w_fo, g_final, final, out_tokens)
        else:
            o = l // 2
            x = parts[0] if len(parts) == 1 else jnp.concatenate(parts, axis=0)
            outs = _odd(x, row(norm_mix[l]), conv_w_in[o].astype(BF16), conv_w[o].astype(F32),
                        conv_w_out[o].astype(BF16), row(norm_ffn[l]), w_fi, w_fo, g_final, final, segs,
                        out_tokens)
        parts = list(outs)
    return tuple(p.reshape(a.shape) for p, a in zip(parts, xs))


def kernel(x_prompt, x_sample, norm_mix, norm_ffn, norm_final, even_w_in, even_w_out, hgrn_lb_logits,
           hgrn_norm, na_rpb, conv_w_in, conv_w, conv_w_out, ffn_w_in, ffn_w_out):
    return _trunk([x_prompt, x_sample], norm_mix, norm_ffn, norm_final, even_w_in, even_w_out,
                  hgrn_lb_logits, hgrn_norm, na_rpb, conv_w_in, conv_w, conv_w_out, ffn_w_in, ffn_w_out)
```

```python
import functools

import numpy as np
import jax
import jax.numpy as jnp
from jax import lax
from jax.experimental import pallas as pl
from jax.experimental.pallas import tpu as pltpu

F32 = jnp.float32
BF16 = jnp.bfloat16

SUBLANES = 8
D_MODEL = 1024
EPS = 1e-6
NEG = -1e30
LOG2E = 1.4426950408889634
GRID_W = 64
A_WIDTH = 512
A_HEADS = 4
A_DK = 128
CHUNK = 64
LEVELS = (2, 4, 8, 16, 32, 64)
NA_WIDTH = 512
NA_HEADS = 8
NA_HD = 64
NA_KR = 8
NA_KC = 16
NA_GROUP = 2
NA_UNROLL = 2
CONV_W = 3
D_FF = 2816
FF_CHUNK = 256
PA_W = 5 * A_WIDTH
PN_W = 3 * NA_WIDTH
EVEN_IN = PA_W + PN_W

TM = 512
HGRN_BLOCK = 512
NA_ROWS = 8
NA_WIN = 3 * NA_ROWS
HALO = 8
VMEM_LIMIT = 56 * 1024 * 1024

NT_DIMS = (((1,), (1,)), ((), ()))
TN_DIMS = (((0,), (0,)), ((), ()))


def _resident(shape):
    nd = len(shape)
    return pl.BlockSpec(shape, lambda *_: (0,) * nd, pipeline_mode=pl.Buffered(1))


def _rms(x, g):
    return x * lax.rsqrt(jnp.mean(x * x, axis=-1, keepdims=True) + EPS) * g


def _sigmoid(x):
    return jax.nn.sigmoid(x)


def _dot(a, b):
    return jnp.dot(a, b, preferred_element_type=F32)


def _dot_nt(a, b):
    return lax.dot_general(a, b, NT_DIMS, preferred_element_type=F32)


def _tile_starts(parts):
    starts = [0]
    for a in parts:
        starts.append(starts[-1] + a.shape[0] // TM)
    return tuple(starts)


def _part_specs(parts):
    starts = _tile_starts(parts)
    return [pl.BlockSpec((TM, D_MODEL),
                         lambda i, *_, s=starts[k], n=starts[k + 1] - starts[k]: (jnp.clip(i - s, 0, n - 1), 0))
            for k in range(len(parts))]


def _select_part(starts, refs):
    step = pl.program_id(0)
    x = refs[-1][...]
    for k in range(len(refs) - 2, -1, -1):
        x = jnp.where(step < starts[k + 1], refs[k][...], x)
    return x


def _in_even_body(starts, *refs):
    n = len(starts) - 1
    g_ref, w_ref, pa_ref, pn_ref = refs[n:]
    h = _rms(_select_part(starts, refs[:n]), g_ref[...]).astype(BF16)
    for c in range(0, PA_W, 512):
        pa_ref[:, c:c + 512] = _dot(h, w_ref[:, c:c + 512])
    for c in range(0, PN_W, 512):
        pn_ref[:, c:c + 512] = _dot(h, w_ref[:, PA_W + c:PA_W + c + 512]).astype(BF16)


def _in_even(parts, g, w):
    starts = _tile_starts(parts)
    t = starts[-1] * TM
    return pl.pallas_call(
        functools.partial(_in_even_body, starts),
        grid=(starts[-1],),
        in_specs=_part_specs(parts) + [_resident((1, D_MODEL)), _resident((D_MODEL, EVEN_IN))],
        out_specs=[pl.BlockSpec((TM, PA_W), lambda i: (i, 0)),
                   pl.BlockSpec((TM, PN_W), lambda i: (i, 0))],
        out_shape=[jax.ShapeDtypeStruct((t, PA_W), F32),
                   jax.ShapeDtypeStruct((t, PN_W), BF16)],
        compiler_params=pltpu.CompilerParams(dimension_semantics=("arbitrary",),
                                             vmem_limit_bytes=VMEM_LIMIT),
        name="in_even",
    )(*parts, g, w)


def _hgrn_masks():
    t = np.arange(CHUNK)[:, None]
    s = np.arange(CHUNK)[None, :]
    out = np.zeros((2, 1 + len(LEVELS), CHUNK, CHUNK), np.float32)
    for d, rev in enumerate((False, True)):
        out[d, 0] = (t == s)
        for li, L in enumerate(LEVELS):
            half = L // 2
            same = (t // L) == (s // L)
            if rev:
                m = same & ((t % L) < half) & ((s % L) >= half)
            else:
                m = same & ((t % L) >= half) & ((s % L) < half)
            out[d, 1 + li] = m
    return np.tile(out, (1, 1, 1, 2))


def _hgrn_signs():
    t = np.arange(CHUNK)
    out = np.zeros((2, len(LEVELS), CHUNK), np.float32)
    for d, rev in enumerate((False, True)):
        for li, L in enumerate(LEVELS):
            second = (t % L) >= L // 2
            out[d, li] = np.where(second != rev, 1.0, -1.0)
    return np.ascontiguousarray(np.broadcast_to(out[..., None], out.shape + (A_DK,)))


def _tri(rev):
    t = lax.broadcasted_iota(jnp.int32, (CHUNK, CHUNK), 0)
    s = lax.broadcasted_iota(jnp.int32, (CHUNK, CHUNK), 1)
    return jnp.where((s >= t) if rev else (s <= t), 1.0, 0.0).astype(BF16)


def _level_ref(b3, L, rev):
    g, sub, w = b3.shape
    half = L // 2
    if L >= 2 * SUBLANES:
        n = L // SUBLANES
        pieces = []
        for blk in range(CHUNK // L):
            r = blk * L + (half if rev else half - 1)
            pieces.append(jnp.broadcast_to(b3[r // SUBLANES:r // SUBLANES + 1, r % SUBLANES:r % SUBLANES + 1, :],
                                           (n, sub, w)))
        return pieces[0] if len(pieces) == 1 else jnp.concatenate(pieces, axis=0)
    srow = lax.broadcasted_iota(jnp.int32, b3.shape, 1)
    if L == SUBLANES:
        r = half if rev else half - 1
        return jnp.broadcast_to(b3[:, r:r + 1, :], b3.shape)
    if L == 4:
        r = half if rev else half - 1
        lo = jnp.broadcast_to(b3[:, r:r + 1, :], b3.shape)
        hi = jnp.broadcast_to(b3[:, r + 4:r + 5, :], b3.shape)
        return jnp.where(srow < 4, lo, hi)
    odd = (srow % 2) == 1
    if rev:
        return jnp.where(odd, b3, pltpu.roll(b3, SUBLANES - 1, 1))
    return jnp.where(odd, pltpu.roll(b3, 1, 1), b3)


def _pair_blockdiag(x):
    zero = jnp.zeros((CHUNK, A_DK), x.dtype)
    return jnp.concatenate([jnp.concatenate([x[:, :A_DK], zero], axis=1),
                            jnp.concatenate([zero, x[:, A_DK:]], axis=1)], axis=0)


def _hgrn_chunk(q, k, v, b, s_ref, d, rev, m_ref, sgn_ref):
    tot = b[0:1, :] if rev else b[CHUNK - 1:CHUNK, :]
    qb = q.astype(BF16)
    kb = k.astype(BF16)
    q0 = qb * jnp.exp2(b).astype(BF16)
    k2 = kb * jnp.exp2(tot - b).astype(BF16)
    vb = v.astype(BF16)
    dec = jnp.exp2(tot)
    inter = []
    for h in range(A_HEADS):
        sl = slice(h * A_DK, (h + 1) * A_DK)
        s_t = s_ref[d, h]
        inter.append(_dot_nt(q0[:, sl], s_t.astype(BF16)))
        ds = lax.dot_general(vb[:, sl], k2[:, sl], TN_DIMS, preferred_element_type=F32)
        s_ref[d, h] = dec[:, sl] * s_t + ds
    pairs = [slice(p * 2 * A_DK, (p + 1) * 2 * A_DK) for p in range(A_HEADS // 2)]
    b3 = b.reshape(CHUNK // SUBLANES, SUBLANES, A_WIDTH)
    acc = [_dot_nt(qb[:, p], _pair_blockdiag(kb[:, p])) * m_ref[d, 0] for p in pairs]
    for li, L in enumerate(LEVELS):
        sgn = jnp.concatenate([sgn_ref[d, li]] * A_HEADS, axis=1).reshape(b3.shape)
        e = jnp.exp2((b3 - _level_ref(b3, L, rev)) * sgn).reshape(CHUNK, A_WIDTH).astype(BF16)
        qs = qb * e
        ks = kb * e
        acc = [a + _dot_nt(qs[:, p], _pair_blockdiag(ks[:, p])) * m_ref[d, 1 + li] for a, p in zip(acc, pairs)]
    intra = [_dot(a.astype(BF16), _pair_blockdiag(vb[:, p])) for a, p in zip(acc, pairs)]
    return jnp.concatenate(intra, axis=1) + jnp.concatenate(inter, axis=1)


def _hgrn_body(layer, tbl_ref, qf_ref, zf_ref, vf_ref, qb_ref, zb_ref, vb_ref, lbl_ref, m_ref, sgn_ref,
               of_ref, ob_ref, s_ref):
    step = pl.program_id(0)

    @pl.when(tbl_ref[2, step] == 1)
    def _():
        s_ref[...] = jnp.zeros_like(s_ref)

    n_even = lbl_ref.shape[0]
    logits = [lbl_ref[i] for i in range(n_even)]
    mx = functools.reduce(jnp.maximum, logits)
    ex = [jnp.exp(l - mx) for l in logits]
    den = functools.reduce(lambda x, y: x + y, ex)
    ps = [e / den for e in ex]
    lbs = functools.reduce(lambda x, y: x + y, ps[:layer + 1]) - ps[0]

    nc = HGRN_BLOCK // CHUNK
    refs = ((qf_ref, zf_ref, vf_ref, of_ref), (qb_ref, zb_ref, vb_ref, ob_ref))
    tris = (_tri(False), _tri(True))

    def chunk(c, carry):
        prep = []
        for d, rev in enumerate((False, True)):
            q_ref, z_ref, v_ref, _ = refs[d]
            r0 = pl.multiple_of(((nc - 1 - c) if rev else c) * CHUNK, CHUNK)
            rows = pl.ds(r0, CHUNK)
            qa = q_ref[rows, :]
            z = z_ref[rows, :]
            lb = lbs[d:d + 1, :]
            q = qa * _sigmoid(qa)
            f = lb + (1.0 - lb) * _sigmoid(z)
            k = 1.0 - f
            lf = jnp.log(f)
            hi = lf.astype(BF16)
            r1 = lf - hi.astype(F32)
            mid = r1.astype(BF16)
            lo = (r1 - mid.astype(F32)).astype(BF16)
            cs = _dot(tris[d], jnp.concatenate([hi, mid, lo], axis=1))
            b = (cs[:, :A_WIDTH] + cs[:, A_WIDTH:2 * A_WIDTH] + cs[:, 2 * A_WIDTH:]) * LOG2E
            prep.append((rows, q, k, v_ref[rows, :], b))
        for d, rev in enumerate((False, True)):
            rows, q, k, v, b = prep[d]
            refs[d][3][rows, :] = _hgrn_chunk(q, k, v, b, s_ref, d, rev, m_ref, sgn_ref)
        return carry

    lax.fori_loop(0, nc, chunk, 0)


def _hgrn_table(segs):
    rows = []
    for off, batch, seq in segs:
        nblk = seq // HGRN_BLOCK
        for bi in range(batch):
            base = (off + bi * seq) // HGRN_BLOCK
            for i in range(nblk):
                rows.append((base + i, base + nblk - 1 - i, int(i == 0)))
    return np.asarray(rows, np.int32).T.copy()


def _hgrn(pa, lb_logits, layer, segs):
    t = pa.shape[0]
    tbl = _hgrn_table(segs)
    steps = tbl.shape[1]
    masks = _hgrn_masks()
    signs = _hgrn_signs()

    def spec(col, which):
        return pl.BlockSpec((HGRN_BLOCK, A_WIDTH), lambda i, tb: (tb[which, i], col))

    grid_spec = pltpu.PrefetchScalarGridSpec(
        num_scalar_prefetch=1,
        grid=(steps,),
        in_specs=[spec(0, 0), spec(1, 0), spec(3, 0), spec(0, 1), spec(2, 1), spec(3, 1),
                  pl.BlockSpec(lb_logits.shape, lambda i, tb: (0, 0, 0)),
                  pl.BlockSpec(masks.shape, lambda i, tb: (0, 0, 0, 0)),
                  pl.BlockSpec(signs.shape, lambda i, tb: (0, 0, 0, 0))],
        out_specs=[pl.BlockSpec((HGRN_BLOCK, A_WIDTH), lambda i, tb: (tb[0, i], 0)),
                   pl.BlockSpec((HGRN_BLOCK, A_WIDTH), lambda i, tb: (tb[1, i], 0))],
        scratch_shapes=[pltpu.VMEM((2, A_HEADS, A_DK, A_DK), F32)],
    )
    return pl.pallas_call(
        functools.partial(_hgrn_body, layer),
        grid_spec=grid_spec,
        out_shape=[jax.ShapeDtypeStruct((t, A_WIDTH), F32),
                   jax.ShapeDtypeStruct((t, A_WIDTH), F32)],
        compiler_params=pltpu.CompilerParams(dimension_semantics=("arbitrary",),
                                             vmem_limit_bytes=VMEM_LIMIT),
        name="hgrn",
    )(jnp.asarray(tbl), pa, pa, pa, pa, pa, pa, lb_logits, jnp.asarray(masks), jnp.asarray(signs))


def _na_build_bias(rpb_ref, bias_ref):
    n_ro = 2 * NA_KR - 1
    n_co = 2 * NA_KC - 1
    qc = lax.broadcasted_iota(jnp.int32, (GRID_W, GRID_W), 0)
    kc = lax.broadcasted_iota(jnp.int32, (GRID_W, GRID_W), 1)
    co = jnp.clip(kc - qc, -(NA_KC - 1), NA_KC - 1) + NA_KC - 1
    ws = jnp.clip(qc - NA_KC // 2, 0, GRID_W - NA_KC)
    valid = (kc >= ws) & (kc < ws + NA_KC)

    def build(hr, carry):
        h = hr // n_ro
        ro = hr % n_ro
        t = jnp.full((GRID_W, GRID_W), NEG, F32)
        for j in range(n_co):
            t = jnp.where(co == j, rpb_ref[hr * n_co + j], t)
        t = jnp.where(valid, t, NEG)
        for i in range(NA_KR):
            dd = ro - i

            @pl.when((dd >= 0) & (dd < NA_KR))
            def _():
                bias_ref[h, dd, :, i * GRID_W:(i + 1) * GRID_W] = t
        return carry

    lax.fori_loop(0, NA_HEADS * n_ro, build, 0)


def _na_body(tbl_ref, rpb_ref, q_ref, k_ref, v_ref, o_ref, bias_ref):
    step = pl.program_id(0)

    @pl.when(step == 0)
    def _():
        _na_build_bias(rpb_ref, bias_ref)

    r0 = tbl_ref[2, step]
    w0 = tbl_ref[3, step]
    n_rows = tbl_ref[4, step]
    head = lax.broadcasted_iota(jnp.int32, (GRID_W, NA_GROUP * NA_HD), 1) // NA_HD
    nkeys = NA_KR * GRID_W
    scale = jnp.asarray(NA_HD ** -0.5, BF16)

    groups = [slice(g * NA_GROUP * NA_HD, (g + 1) * NA_GROUP * NA_HD) for g in range(NA_HEADS // NA_GROUP)]

    def rows(it, carry):
        units = []
        for u in range(NA_UNROLL):
            rr = it * NA_UNROLL + u
            r = r0 + rr
            row_start = jnp.clip(r - NA_KR // 2, 0, n_rows - NA_KR)
            dd = row_start - r + (NA_KR - 1)
            keys = pl.ds(pl.multiple_of((row_start - w0) * GRID_W, GRID_W), nkeys)
            qrows = pl.ds(pl.multiple_of(rr * GRID_W, GRID_W), GRID_W)
            units += [(dd, keys, qrows, g, lanes) for g, lanes in enumerate(groups)]
        scores = []
        for dd, keys, qrows, g, lanes in units:
            qp = q_ref[qrows, lanes] * scale
            zero = jnp.zeros_like(qp)
            qs = jnp.concatenate([jnp.where(head == h, qp, zero) for h in range(NA_GROUP)], axis=0)
            scores.append(_dot_nt(qs, k_ref[keys, lanes]))
        probs = []
        for (dd, keys, qrows, g, lanes), s in zip(units, scores):
            s = s + jnp.concatenate([bias_ref[NA_GROUP * g + h, dd] for h in range(NA_GROUP)], axis=0)
            m = jnp.max(s, axis=-1, keepdims=True)
            p = jnp.exp(s - m)
            probs.append((p.astype(BF16), jnp.sum(p, axis=-1, keepdims=True)))
        for (dd, keys, qrows, g, lanes), (p, l) in zip(units, probs):
            pv = _dot(p, v_ref[keys, lanes]) / l
            o = pv[:GRID_W]
            for h in range(1, NA_GROUP):
                o = jnp.where(head == h, pv[h * GRID_W:(h + 1) * GRID_W], o)
            o_ref[qrows, lanes] = o.astype(o_ref.dtype)
        return carry

    lax.fori_loop(0, NA_ROWS // NA_UNROLL, rows, 0)


def _na_table(segs):
    blk = NA_ROWS * GRID_W
    rows = []
    for off, batch, seq in segs:
        n_rows = seq // GRID_W
        assert n_rows >= NA_WIN
        for bi in range(batch):
            base = off + bi * seq
            for i in range(seq // blk):
                r0 = i * NA_ROWS
                w0 = min(max(r0 - NA_ROWS, 0), n_rows - NA_WIN)
                rows.append((base // blk + i, base // GRID_W + w0, r0, w0, n_rows))
    return np.asarray(rows, np.int32).T.copy()


def _na(pn, rpb, segs):
    t = pn.shape[0]
    tbl = _na_table(segs)
    steps = tbl.shape[1]
    blk = NA_ROWS * GRID_W

    def window(col):
        return pl.BlockSpec((pl.Element(NA_WIN * GRID_W), pl.Element(NA_WIDTH)),
                            lambda i, tb: (tb[1, i] * GRID_W, col * NA_WIDTH))

    grid_spec = pltpu.PrefetchScalarGridSpec(
        num_scalar_prefetch=1,
        grid=(steps,),
        in_specs=[pl.BlockSpec(memory_space=pltpu.SMEM),
                  pl.BlockSpec((blk, NA_WIDTH), lambda i, tb: (tb[0, i], 0)),
                  window(1), window(2)],
        out_specs=pl.BlockSpec((blk, NA_WIDTH), lambda i, tb: (tb[0, i], 0)),
        scratch_shapes=[pltpu.VMEM((NA_HEADS, NA_KR, GRID_W, NA_KR * GRID_W), F32)],
    )
    return pl.pallas_call(
        _na_body,
        grid_spec=grid_spec,
        out_shape=jax.ShapeDtypeStruct((t, NA_WIDTH), BF16),
        compiler_params=pltpu.CompilerParams(dimension_semantics=("arbitrary",),
                                             vmem_limit_bytes=VMEM_LIMIT),
        name="natten",
    )(jnp.asarray(tbl), rpb.astype(F32).reshape(-1), pn, pn, pn)


def _out_specs_shapes(out_tokens):
    starts = [0]
    for t in out_tokens:
        starts.append(starts[-1] + t // TM)
    specs = []
    for k in range(len(out_tokens)):
        s, n = starts[k], starts[k + 1] - starts[k]
        specs.append(pl.BlockSpec((TM, D_MODEL), lambda i, *_, s=s, n=n: (jnp.clip(i - s, 0, n - 1), 0)))
    shapes = [jax.ShapeDtypeStruct((t, D_MODEL), F32) for t in out_tokens]
    return tuple(starts), specs, shapes


def _store_parts(starts, out_refs, acc):
    if len(out_refs) == 1:
        out_refs[0][...] = acc
        return
    step = pl.program_id(0)
    for k, o_ref in enumerate(out_refs):
        @pl.when((step >= starts[k]) & (step < starts[k + 1]))
        def _(o_ref=o_ref):
            o_ref[...] = acc


def _ffn_tail(x1, gf_ref, wi_ref, wo_ref, gfin_ref, final):
    h = _rms(x1, gf_ref[...]).astype(BF16)
    acc = x1
    for c in range(0, D_FF, FF_CHUNK):
        g = _dot(h, wi_ref[:, c:c + FF_CHUNK])
        u = _dot(h, wi_ref[:, D_FF + c:D_FF + c + FF_CHUNK])
        a = (g * _sigmoid(g) * u).astype(BF16)
        acc = acc + _dot(a, wo_ref[c:c + FF_CHUNK, :])
    if final:
        acc = _rms(acc, gfin_ref[...])
    return acc


def _post_even_body(final, starts, out_starts, *refs):
    n = len(starts) - 1
    of_ref, ob_ref, ga_ref, n_ref, gain_ref, wm_ref, gf_ref, wi_ref, wo_ref, gfin_ref = refs[n:n + 10]
    o = of_ref[...] + ob_ref[...]
    parts = []
    for h in range(A_HEADS):
        oh = o[:, h * A_DK:(h + 1) * A_DK]
        parts.append(oh * lax.rsqrt(jnp.mean(oh * oh, axis=-1, keepdims=True) + EPS))
    on = jnp.concatenate(parts, axis=1) * gain_ref[...]
    g = ga_ref[...]
    oa = (on * (g * _sigmoid(g))).astype(BF16)
    mix = _dot(oa, wm_ref[0:A_WIDTH, :]) + _dot(n_ref[...], wm_ref[A_WIDTH:A_WIDTH + NA_WIDTH, :])
    acc = _ffn_tail(_select_part(starts, refs[:n]) + mix, gf_ref, wi_ref, wo_ref, gfin_ref, final)
    _store_parts(out_starts, refs[n + 10:], acc)


def _post_even(parts, o_f, o_b, pa, o_n, gain, w_mix, g_ffn, w_ffn_in, w_ffn_out, g_final, final, out_tokens):
    starts = _tile_starts(parts)
    out_starts, out_specs, out_shapes = _out_specs_shapes(out_tokens)
    tok = lambda w, col=0: pl.BlockSpec((TM, w), lambda i: (i, col))
    return pl.pallas_call(
        functools.partial(_post_even_body, final, starts, out_starts),
        grid=(starts[-1],),
        in_specs=_part_specs(parts) + [tok(A_WIDTH), tok(A_WIDTH), tok(A_WIDTH, 4), tok(NA_WIDTH),
                                       _resident((1, A_WIDTH)), _resident((A_WIDTH + NA_WIDTH, D_MODEL)),
                                       _resident((1, D_MODEL)), _resident((D_MODEL, 2 * D_FF)),
                                       _resident((D_FF, D_MODEL)), _resident((1, D_MODEL))],
        out_specs=out_specs,
        out_shape=out_shapes,
        compiler_params=pltpu.CompilerParams(dimension_semantics=("arbitrary",),
                                             vmem_limit_bytes=VMEM_LIMIT),
        name="post_even",
    )(*parts, o_f, o_b, pa, o_n, gain, w_mix, g_ffn, w_ffn_in, w_ffn_out, g_final)


def _odd_body(final, out_starts, tbl_ref, x_ref, xp_ref, xn_ref, gm_ref, wc_ref, cw_ref, wco_ref, gf_ref,
              wi_ref, wo_ref, gfin_ref, *out_refs):
    step = pl.program_id(0)
    x = x_ref[...]
    gm = gm_ref[...]
    h = _rms(x, gm).astype(BF16)
    z = _dot(h, wc_ref[:, D_MODEL:2 * D_MODEL]) * _dot(h, wc_ref[:, 2 * D_MODEL:3 * D_MODEL])
    hh = _rms(jnp.concatenate([xp_ref[...], xn_ref[...]], axis=0), gm).astype(BF16)
    zh = _dot(hh, wc_ref[:, D_MODEL:2 * D_MODEL]) * _dot(hh, wc_ref[:, 2 * D_MODEL:3 * D_MODEL])
    z_prev = jnp.where(tbl_ref[2, step] == 1, 0.0, zh[HALO - 1:HALO, :])
    z_next = jnp.where(tbl_ref[3, step] == 1, 0.0, zh[HALO:HALO + 1, :])
    row = lax.broadcasted_iota(jnp.int32, (TM, 1), 0)
    z_dn = jnp.where(row == 0, z_prev, pltpu.roll(z, 1, 0))
    z_up = jnp.where(row == TM - 1, z_next, pltpu.roll(z, TM - 1, 0))
    conv = z_dn * cw_ref[0:1, :] + z * cw_ref[1:2, :] + z_up * cw_ref[2:3, :]
    y = (_dot(h, wc_ref[:, 0:D_MODEL]) * conv).astype(BF16)
    acc = _ffn_tail(x + _dot(y, wco_ref[...]), gf_ref, wi_ref, wo_ref, gfin_ref, final)
    _store_parts(out_starts, out_refs, acc)


def _odd_table(segs, t):
    rows = []
    starts = set()
    ends = set()
    for off, batch, seq in segs:
        for bi in range(batch):
            starts.add(off + bi * seq)
            ends.add(off + (bi + 1) * seq)
    for i in range(t // TM):
        t0 = i * TM
        first = int(t0 in starts)
        last = int(t0 + TM in ends)
        rows.append((max(t0 // HALO - 1, 0), min((t0 + TM) // HALO, t // HALO - 1), first, last))
    return np.asarray(rows, np.int32).T.copy()


def _odd(x, g_mix, w_conv_in, conv_w, w_conv_out, g_ffn, w_ffn_in, w_ffn_out, g_final, final, segs, out_tokens):
    t = x.shape[0]
    tbl = _odd_table(segs, t)
    out_starts, out_specs, out_shapes = _out_specs_shapes(out_tokens)
    res = lambda shape: pl.BlockSpec(shape, lambda i, tb: (0,) * len(shape), pipeline_mode=pl.Buffered(1))
    grid_spec = pltpu.PrefetchScalarGridSpec(
        num_scalar_prefetch=1,
        grid=(t // TM,),
        in_specs=[pl.BlockSpec((TM, D_MODEL), lambda i, tb: (i, 0)),
                  pl.BlockSpec((HALO, D_MODEL), lambda i, tb: (tb[0, i], 0)),
                  pl.BlockSpec((HALO, D_MODEL), lambda i, tb: (tb[1, i], 0)),
                  res((1, D_MODEL)), res((D_MODEL, 3 * D_MODEL)), res((CONV_W, D_MODEL)),
                  res((D_MODEL, D_MODEL)), res((1, D_MODEL)), res((D_MODEL, 2 * D_FF)),
                  res((D_FF, D_MODEL)), res((1, D_MODEL))],
        out_specs=out_specs,
    )
    return pl.pallas_call(
        functools.partial(_odd_body, final, out_starts),
        grid_spec=grid_spec,
        out_shape=out_shapes,
        compiler_params=pltpu.CompilerParams(dimension_semantics=("arbitrary",),
                                             vmem_limit_bytes=VMEM_LIMIT),
        name="odd_layer",
    )(jnp.asarray(tbl), x, x, x, g_mix, w_conv_in, conv_w, w_conv_out, g_ffn, w_ffn_in, w_ffn_out, g_final)


def _trunk(xs, norm_mix, norm_ffn, norm_final, even_w_in, even_w_out, hgrn_lb_logits, hgrn_norm,
           na_rpb, conv_w_in, conv_w, conv_w_out, ffn_w_in, ffn_w_out):
    depth = norm_mix.shape[0]
    segs = []
    off = 0
    for a in xs:
        segs.append((off, a.shape[0], a.shape[1]))
        off += a.shape[0] * a.shape[1]
    parts = [a.reshape(-1, D_MODEL) for a in xs]
    part_tokens = [p.shape[0] for p in parts]
    total = off
    row = lambda v: v.reshape(1, -1).astype(F32)
    g_final = row(norm_final)
    for l in range(depth):
        final = l == depth - 1
        out_tokens = part_tokens if final else [total]
        w_fi = ffn_w_in[l].astype(BF16)
        w_fo = ffn_w_out[l].astype(BF16)
        if l % 2 == 0:
            e = l // 2
            pa, pn = _in_even(parts, row(norm_mix[l]), even_w_in[e].astype(BF16))
            o_f, o_b = _hgrn(pa, hgrn_lb_logits.astype(F32), e, segs)
            o_n = _na(pn, na_rpb[e], segs)
            outs = _post_even(parts, o_f, o_b, pa, o_n, row(hgrn_norm[e]), even_w_out[e].astype(BF16),
                              row(norm_ffn[l]), w_fi, w_fo, g_final, final, out_tokens)
        else:
            o = l // 2
            x = parts[0] if len(parts) == 1 else jnp.concatenate(parts, axis=0)
            outs = _odd(x, row(norm_mix[l]), conv_w_in[o].astype(BF16), conv_w[o].astype(F32),
                        conv_w_out[o].astype(BF16), row(norm_ffn[l]), w_fi, w_fo, g_final, final, segs,
                        out_tokens)
        parts = list(outs)
    return tuple(p.reshape(a.shape) for p, a in zip(parts, xs))


def kernel(x_prompt, x_sample, norm_mix, norm_ffn, norm_final, even_w_in, even_w_out, hgrn_lb_logits,
           hgrn_norm, na_rpb, conv_w_in, conv_w, conv_w_out, ffn_w_in, ffn_w_out):
    return _trunk([x_prompt, x_sample], norm_mix, norm_ffn, norm_final, even_w_in, even_w_out,
                  hgrn_lb_logits, hgrn_norm, na_rpb, conv_w_in, conv_w, conv_w_out, ffn_w_in, ffn_w_out)
```

```python
import functools

import numpy as np
import jax
import jax.numpy as jnp
from jax import lax
from jax.experimental import pallas as pl
from jax.experimental.pallas import tpu as pltpu

F32 = jnp.float32
BF16 = jnp.bfloat16

SUBLANES = 8
D_MODEL = 1024
EPS = 1e-6
NEG = -1e30
LOG2E = 1.4426950408889634
GRID_W = 64
A_WIDTH = 512
A_HEADS = 4
A_DK = 128
CHUNK = 64
LEVELS = (2, 4, 8, 16, 32, 64)
NA_WIDTH = 512
NA_HEADS = 8
NA_HD = 64
NA_KR = 8
NA_KC = 16
NA_GROUP = 2
NA_UNROLL = 4
CONV_W = 3
D_FF = 2816
FF_CHUNK = 256
PA_W = 5 * A_WIDTH
PN_W = 3 * NA_WIDTH
EVEN_IN = PA_W + PN_W

TM = 512
HGRN_BLOCK = 512
NA_ROWS = 8
NA_WIN = 3 * NA_ROWS
HALO = 8
VMEM_LIMIT = 56 * 1024 * 1024

NT_DIMS = (((1,), (1,)), ((), ()))
TN_DIMS = (((0,), (0,)), ((), ()))


def _resident(shape):
    nd = len(shape)
    return pl.BlockSpec(shape, lambda *_: (0,) * nd, pipeline_mode=pl.Buffered(1))


def _rms(x, g):
    return x * lax.rsqrt(jnp.mean(x * x, axis=-1, keepdims=True) + EPS) * g


def _sigmoid(x):
    return jax.nn.sigmoid(x)


def _dot(a, b):
    return jnp.dot(a, b, preferred_element_type=F32)


def _dot_nt(a, b):
    return lax.dot_general(a, b, NT_DIMS, preferred_element_type=F32)


def _tile_starts(parts):
    starts = [0]
    for a in parts:
        starts.append(starts[-1] + a.shape[0] // TM)
    return tuple(starts)


def _part_specs(parts):
    starts = _tile_starts(parts)
    return [pl.BlockSpec((TM, D_MODEL),
                         lambda i, *_, s=starts[k], n=starts[k + 1] - starts[k]: (jnp.clip(i - s, 0, n - 1), 0))
            for k in range(len(parts))]


def _select_part(starts, refs):
    step = pl.program_id(0)
    x = refs[-1][...]
    for k in range(len(refs) - 2, -1, -1):
        x = jnp.where(step < starts[k + 1], refs[k][...], x)
    return x


def _in_even_body(starts, *refs):
    n = len(starts) - 1
    g_ref, w_ref, pa_ref, pn_ref = refs[n:]
    h = _rms(_select_part(starts, refs[:n]), g_ref[...]).astype(BF16)
    for c in range(0, PA_W, 512):
        pa_ref[:, c:c + 512] = _dot(h, w_ref[:, c:c + 512])
    for c in range(0, PN_W, 512):
        pn_ref[:, c:c + 512] = _dot(h, w_ref[:, PA_W + c:PA_W + c + 512]).astype(BF16)


def _in_even(parts, g, w):
    starts = _tile_starts(parts)
    t = starts[-1] * TM
    return pl.pallas_call(
        functools.partial(_in_even_body, starts),
        grid=(starts[-1],),
        in_specs=_part_specs(parts) + [_resident((1, D_MODEL)), _resident((D_MODEL, EVEN_IN))],
        out_specs=[pl.BlockSpec((TM, PA_W), lambda i: (i, 0)),
                   pl.BlockSpec((TM, PN_W), lambda i: (i, 0))],
        out_shape=[jax.ShapeDtypeStruct((t, PA_W), F32),
                   jax.ShapeDtypeStruct((t, PN_W), BF16)],
        compiler_params=pltpu.CompilerParams(dimension_semantics=("arbitrary",),
                                             vmem_limit_bytes=VMEM_LIMIT),
        name="in_even",
    )(*parts, g, w)


def _hgrn_masks():
    t = np.arange(CHUNK)[:, None]
    s = np.arange(CHUNK)[None, :]
    out = np.zeros((2, 1 + len(LEVELS), CHUNK, CHUNK), np.float32)
    for d, rev in enumerate((False, True)):
        out[d, 0] = (t == s)
        for li, L in enumerate(LEVELS):
            half = L // 2
            same = (t // L) == (s // L)
            if rev:
                m = same & ((t % L) < half) & ((s % L) >= half)
            else:
                m = same & ((t % L) >= half) & ((s % L) < half)
            out[d, 1 + li] = m
    return np.tile(out, (1, 1, 1, 2))


def _hgrn_signs():
    t = np.arange(CHUNK)
    out = np.zeros((2, len(LEVELS), CHUNK), np.float32)
    for d, rev in enumerate((False, True)):
        for li, L in enumerate(LEVELS):
            second = (t % L) >= L // 2
            out[d, li] = np.where(second != rev, 1.0, -1.0)
    return np.ascontiguousarray(np.broadcast_to(out[..., None], out.shape + (A_DK,)))


def _tri(rev):
    t = lax.broadcasted_iota(jnp.int32, (CHUNK, CHUNK), 0)
    s = lax.broadcasted_iota(jnp.int32, (CHUNK, CHUNK), 1)
    return jnp.where((s >= t) if rev else (s <= t), 1.0, 0.0).astype(BF16)


def _level_ref(b3, L, rev):
    g, sub, w = b3.shape
    half = L // 2
    if L >= 2 * SUBLANES:
        n = L // SUBLANES
        pieces = []
        for blk in range(CHUNK // L):
            r = blk * L + (half if rev else half - 1)
            pieces.append(jnp.broadcast_to(b3[r // SUBLANES:r // SUBLANES + 1, r % SUBLANES:r % SUBLANES + 1, :],
                                           (n, sub, w)))
        return pieces[0] if len(pieces) == 1 else jnp.concatenate(pieces, axis=0)
    srow = lax.broadcasted_iota(jnp.int32, b3.shape, 1)
    if L == SUBLANES:
        r = half if rev else half - 1
        return jnp.broadcast_to(b3[:, r:r + 1, :], b3.shape)
    if L == 4:
        r = half if rev else half - 1
        lo = jnp.broadcast_to(b3[:, r:r + 1, :], b3.shape)
        hi = jnp.broadcast_to(b3[:, r + 4:r + 5, :], b3.shape)
        return jnp.where(srow < 4, lo, hi)
    odd = (srow % 2) == 1
    if rev:
        return jnp.where(odd, b3, pltpu.roll(b3, SUBLANES - 1, 1))
    return jnp.where(odd, pltpu.roll(b3, 1, 1), b3)


def _pair_blockdiag(x):
    zero = jnp.zeros((CHUNK, A_DK), x.dtype)
    return jnp.concatenate([jnp.concatenate([x[:, :A_DK], zero], axis=1),
                            jnp.concatenate([zero, x[:, A_DK:]], axis=1)], axis=0)


def _hgrn_chunk(q, k, v, b, s_ref, d, rev, m_ref, sgn_ref):
    tot = b[0:1, :] if rev else b[CHUNK - 1:CHUNK, :]
    qb = q.astype(BF16)
    kb = k.astype(BF16)
    q0 = qb * jnp.exp2(b).astype(BF16)
    k2 = kb * jnp.exp2(tot - b).astype(BF16)
    vb = v.astype(BF16)
    dec = jnp.exp2(tot)
    inter = []
    for h in range(A_HEADS):
        sl = slice(h * A_DK, (h + 1) * A_DK)
        s_t = s_ref[d, h]
        inter.append(_dot_nt(q0[:, sl], s_t.astype(BF16)))
        ds = lax.dot_general(vb[:, sl], k2[:, sl], TN_DIMS, preferred_element_type=F32)
        s_ref[d, h] = dec[:, sl] * s_t + ds
    pairs = [slice(p * 2 * A_DK, (p + 1) * 2 * A_DK) for p in range(A_HEADS // 2)]
    b3 = b.reshape(CHUNK // SUBLANES, SUBLANES, A_WIDTH)
    acc = [_dot_nt(qb[:, p], _pair_blockdiag(kb[:, p])) * m_ref[d, 0] for p in pairs]
    for li, L in enumerate(LEVELS):
        sgn = jnp.concatenate([sgn_ref[d, li]] * A_HEADS, axis=1).reshape(b3.shape)
        e = jnp.exp2((b3 - _level_ref(b3, L, rev)) * sgn).reshape(CHUNK, A_WIDTH).astype(BF16)
        qs = qb * e
        ks = kb * e
        acc = [a + _dot_nt(qs[:, p], _pair_blockdiag(ks[:, p])) * m_ref[d, 1 + li] for a, p in zip(acc, pairs)]
    intra = [_dot(a.astype(BF16), _pair_blockdiag(vb[:, p])) for a, p in zip(acc, pairs)]
    return jnp.concatenate(intra, axis=1) + jnp.concatenate(inter, axis=1)


def _hgrn_body(layer, tbl_ref, qf_ref, zf_ref, vf_ref, qb_ref, zb_ref, vb_ref, lbl_ref, m_ref, sgn_ref,
               of_ref, ob_ref, s_ref):
    step = pl.program_id(0)

    @pl.when(tbl_ref[2, step] == 1)
    def _():
        s_ref[...] = jnp.zeros_like(s_ref)

    n_even = lbl_ref.shape[0]
    logits = [lbl_ref[i] for i in range(n_even)]
    mx = functools.reduce(jnp.maximum, logits)
    ex = [jnp.exp(l - mx) for l in logits]
    den = functools.reduce(lambda x, y: x + y, ex)
    ps = [e / den for e in ex]
    lbs = functools.reduce(lambda x, y: x + y, ps[:layer + 1]) - ps[0]

    nc = HGRN_BLOCK // CHUNK
    refs = ((qf_ref, zf_ref, vf_ref, of_ref), (qb_ref, zb_ref, vb_ref, ob_ref))
    tris = (_tri(False), _tri(True))

    def chunk(c, carry):
        prep = []
        for d, rev in enumerate((False, True)):
            q_ref, z_ref, v_ref, _ = refs[d]
            r0 = pl.multiple_of(((nc - 1 - c) if rev else c) * CHUNK, CHUNK)
            rows = pl.ds(r0, CHUNK)
            qa = q_ref[rows, :]
            z = z_ref[rows, :]
            lb = lbs[d:d + 1, :]
            q = qa * _sigmoid(qa)
            f = lb + (1.0 - lb) * _sigmoid(z)
            k = 1.0 - f
            lf = jnp.log(f)
            hi = lf.astype(BF16)
            r1 = lf - hi.astype(F32)
            mid = r1.astype(BF16)
            lo = (r1 - mid.astype(F32)).astype(BF16)
            cs = _dot(tris[d], jnp.concatenate([hi, mid, lo], axis=1))
            b = (cs[:, :A_WIDTH] + cs[:, A_WIDTH:2 * A_WIDTH] + cs[:, 2 * A_WIDTH:]) * LOG2E
            prep.append((rows, q, k, v_ref[rows, :], b))
        for d, rev in enumerate((False, True)):
            rows, q, k, v, b = prep[d]
            refs[d][3][rows, :] = _hgrn_chunk(q, k, v, b, s_ref, d, rev, m_ref, sgn_ref)
        return carry

    lax.fori_loop(0, nc, chunk, 0, unroll=4)


def _hgrn_table(segs):
    rows = []
    for off, batch, seq in segs:
        nblk = seq // HGRN_BLOCK
        for bi in range(batch):
            base = (off + bi * seq) // HGRN_BLOCK
            for i in range(nblk):
                rows.append((base + i, base + nblk - 1 - i, int(i == 0)))
    return np.asarray(rows, np.int32).T.copy()


def _hgrn(pa, lb_logits, layer, segs):
    t = pa.shape[0]
    tbl = _hgrn_table(segs)
    steps = tbl.shape[1]
    masks = _hgrn_masks()
    signs = _hgrn_signs()

    def spec(col, which):
        return pl.BlockSpec((HGRN_BLOCK, A_WIDTH), lambda i, tb: (tb[which, i], col))

    grid_spec = pltpu.PrefetchScalarGridSpec(
        num_scalar_prefetch=1,
        grid=(steps,),
        in_specs=[spec(0, 0), spec(1, 0), spec(3, 0), spec(0, 1), spec(2, 1), spec(3, 1),
                  pl.BlockSpec(lb_logits.shape, lambda i, tb: (0, 0, 0)),
                  pl.BlockSpec(masks.shape, lambda i, tb: (0, 0, 0, 0)),
                  pl.BlockSpec(signs.shape, lambda i, tb: (0, 0, 0, 0))],
        out_specs=[pl.BlockSpec((HGRN_BLOCK, A_WIDTH), lambda i, tb: (tb[0, i], 0)),
                   pl.BlockSpec((HGRN_BLOCK, A_WIDTH), lambda i, tb: (tb[1, i], 0))],
        scratch_shapes=[pltpu.VMEM((2, A_HEADS, A_DK, A_DK), F32)],
    )
    return pl.pallas_call(
        functools.partial(_hgrn_body, layer),
        grid_spec=grid_spec,
        out_shape=[jax.ShapeDtypeStruct((t, A_WIDTH), F32),
                   jax.ShapeDtypeStruct((t, A_WIDTH), F32)],
        compiler_params=pltpu.CompilerParams(dimension_semantics=("arbitrary",),
                                             vmem_limit_bytes=VMEM_LIMIT),
        name="hgrn",
    )(jnp.asarray(tbl), pa, pa, pa, pa, pa, pa, lb_logits, jnp.asarray(masks), jnp.asarray(signs))


def _na_build_bias(rpb_ref, bias_ref):
    n_ro = 2 * NA_KR - 1
    n_co = 2 * NA_KC - 1
    qc = lax.broadcasted_iota(jnp.int32, (GRID_W, GRID_W), 0)
    kc = lax.broadcasted_iota(jnp.int32, (GRID_W, GRID_W), 1)
    co = jnp.clip(kc - qc, -(NA_KC - 1), NA_KC - 1) + NA_KC - 1
    ws = jnp.clip(qc - NA_KC // 2, 0, GRID_W - NA_KC)
    valid = (kc >= ws) & (kc < ws + NA_KC)

    def build(hr, carry):
        h = hr // n_ro
        ro = hr % n_ro
        t = jnp.full((GRID_W, GRID_W), NEG, F32)
        for j in range(n_co):
            t = jnp.where(co == j, rpb_ref[hr * n_co + j], t)
        t = jnp.where(valid, t, NEG)
        for i in range(NA_KR):
            dd = ro - i

            @pl.when((dd >= 0) & (dd < NA_KR))
            def _():
                bias_ref[h, dd, :, i * GRID_W:(i + 1) * GRID_W] = t
        return carry

    lax.fori_loop(0, NA_HEADS * n_ro, build, 0)


def _na_body(tbl_ref, rpb_ref, q_ref, k_ref, v_ref, o_ref, bias_ref):
    step = pl.program_id(0)

    @pl.when(step == 0)
    def _():
        _na_build_bias(rpb_ref, bias_ref)

    r0 = tbl_ref[2, step]
    w0 = tbl_ref[3, step]
    n_rows = tbl_ref[4, step]
    head = lax.broadcasted_iota(jnp.int32, (GRID_W, NA_GROUP * NA_HD), 1) // NA_HD
    nkeys = NA_KR * GRID_W
    scale = jnp.asarray(NA_HD ** -0.5, BF16)

    groups = [slice(g * NA_GROUP * NA_HD, (g + 1) * NA_GROUP * NA_HD) for g in range(NA_HEADS // NA_GROUP)]

    def rows(it, carry):
        units = []
        for u in range(NA_UNROLL):
            rr = it * NA_UNROLL + u
            r = r0 + rr
            row_start = jnp.clip(r - NA_KR // 2, 0, n_rows - NA_KR)
            dd = row_start - r + (NA_KR - 1)
            keys = pl.ds(pl.multiple_of((row_start - w0) * GRID_W, GRID_W), nkeys)
            qrows = pl.ds(pl.multiple_of(rr * GRID_W, GRID_W), GRID_W)
            units += [(dd, keys, qrows, g, lanes) for g, lanes in enumerate(groups)]
        scores = []
        for dd, keys, qrows, g, lanes in units:
            qp = q_ref[qrows, lanes] * scale
            zero = jnp.zeros_like(qp)
            qs = jnp.concatenate([jnp.where(head == h, qp, zero) for h in range(NA_GROUP)], axis=0)
            scores.append(_dot_nt(qs, k_ref[keys, lanes]))
        probs = []
        for (dd, keys, qrows, g, lanes), s in zip(units, scores):
            s = s + jnp.concatenate([bias_ref[NA_GROUP * g + h, dd] for h in range(NA_GROUP)], axis=0)
            m = jnp.max(s, axis=-1, keepdims=True)
            p = jnp.exp(s - m)
            probs.append((p.astype(BF16), jnp.sum(p, axis=-1, keepdims=True)))
        for (dd, keys, qrows, g, lanes), (p, l) in zip(units, probs):
            pv = _dot(p, v_ref[keys, lanes]) / l
            o = pv[:GRID_W]
            for h in range(1, NA_GROUP):
                o = jnp.where(head == h, pv[h * GRID_W:(h + 1) * GRID_W], o)
            o_ref[qrows, lanes] = o.astype(o_ref.dtype)
        return carry

    lax.fori_loop(0, NA_ROWS // NA_UNROLL, rows, 0)


def _na_table(segs):
    blk = NA_ROWS * GRID_W
    rows = []
    for off, batch, seq in segs:
        n_rows = seq // GRID_W
        assert n_rows >= NA_WIN
        for bi in range(batch):
            base = off + bi * seq
            for i in range(seq // blk):
                r0 = i * NA_ROWS
                w0 = min(max(r0 - NA_ROWS, 0), n_rows - NA_WIN)
                rows.append((base // blk + i, base // GRID_W + w0, r0, w0, n_rows))
    return np.asarray(rows, np.int32).T.copy()


def _na(pn, rpb, segs):
    t = pn.shape[0]
    tbl = _na_table(segs)
    steps = tbl.shape[1]
    blk = NA_ROWS * GRID_W

    def window(col):
        return pl.BlockSpec((pl.Element(NA_WIN * GRID_W), pl.Element(NA_WIDTH)),
                            lambda i, tb: (tb[1, i] * GRID_W, col * NA_WIDTH))

    grid_spec = pltpu.PrefetchScalarGridSpec(
        num_scalar_prefetch=1,
        grid=(steps,),
        in_specs=[pl.BlockSpec(memory_space=pltpu.SMEM),
                  pl.BlockSpec((blk, NA_WIDTH), lambda i, tb: (tb[0, i], 0)),
                  window(1), window(2)],
        out_specs=pl.BlockSpec((blk, NA_WIDTH), lambda i, tb: (tb[0, i], 0)),
        scratch_shapes=[pltpu.VMEM((NA_HEADS, NA_KR, GRID_W, NA_KR * GRID_W), F32)],
    )
    return pl.pallas_call(
        _na_body,
        grid_spec=grid_spec,
        out_shape=jax.ShapeDtypeStruct((t, NA_WIDTH), BF16),
        compiler_params=pltpu.CompilerParams(dimension_semantics=("arbitrary",),
                                             vmem_limit_bytes=VMEM_LIMIT),
        name="natten",
    )(jnp.asarray(tbl), rpb.astype(F32).reshape(-1), pn, pn, pn)


def _out_specs_shapes(out_tokens):
    starts = [0]
    for t in out_tokens:
        starts.append(starts[-1] + t // TM)
    specs = []
    for k in range(len(out_tokens)):
        s, n = starts[k], starts[k + 1] - starts[k]
        specs.append(pl.BlockSpec((TM, D_MODEL), lambda i, *_, s=s, n=n: (jnp.clip(i - s, 0, n - 1), 0)))
    shapes = [jax.ShapeDtypeStruct((t, D_MODEL), F32) for t in out_tokens]
    return tuple(starts), specs, shapes


def _store_parts(starts, out_refs, acc):
    if len(out_refs) == 1:
        out_refs[0][...] = acc
        return
    step = pl.program_id(0)
    for k, o_ref in enumerate(out_refs):
        @pl.when((step >= starts[k]) & (step < starts[k + 1]))
        def _(o_ref=o_ref):
            o_ref[...] = acc


def _ffn_tail(x1, gf_ref, wi_ref, wo_ref, gfin_ref, final):
    h = _rms(x1, gf_ref[...]).astype(BF16)
    acc = x1
    for c in range(0, D_FF, FF_CHUNK):
        g = _dot(h, wi_ref[:, c:c + FF_CHUNK])
        u = _dot(h, wi_ref[:, D_FF + c:D_FF + c + FF_CHUNK])
        a = (g * _sigmoid(g) * u).astype(BF16)
        acc = acc + _dot(a, wo_ref[c:c + FF_CHUNK, :])
    if final:
        acc = _rms(acc, gfin_ref[...])
    return acc


def _post_even_body(final, starts, out_starts, *refs):
    n = len(starts) - 1
    of_ref, ob_ref, ga_ref, n_ref, gain_ref, wm_ref, gf_ref, wi_ref, wo_ref, gfin_ref = refs[n:n + 10]
    o = of_ref[...] + ob_ref[...]
    parts = []
    for h in range(A_HEADS):
        oh = o[:, h * A_DK:(h + 1) * A_DK]
        parts.append(oh * lax.rsqrt(jnp.mean(oh * oh, axis=-1, keepdims=True) + EPS))
    on = jnp.concatenate(parts, axis=1) * gain_ref[...]
    g = ga_ref[...]
    oa = (on * (g * _sigmoid(g))).astype(BF16)
    mix = _dot(oa, wm_ref[0:A_WIDTH, :]) + _dot(n_ref[...], wm_ref[A_WIDTH:A_WIDTH + NA_WIDTH, :])
    acc = _ffn_tail(_select_part(starts, refs[:n]) + mix, gf_ref, wi_ref, wo_ref, gfin_ref, final)
    _store_parts(out_starts, refs[n + 10:], acc)


def _post_even(parts, o_f, o_b, pa, o_n, gain, w_mix, g_ffn, w_ffn_in, w_ffn_out, g_final, final, out_tokens):
    starts = _tile_starts(parts)
    out_starts, out_specs, out_shapes = _out_specs_shapes(out_tokens)
    tok = lambda w, col=0: pl.BlockSpec((TM, w), lambda i: (i, col))
    return pl.pallas_call(
        functools.partial(_post_even_body, final, starts, out_starts),
        grid=(starts[-1],),
        in_specs=_part_specs(parts) + [tok(A_WIDTH), tok(A_WIDTH), tok(A_WIDTH, 4), tok(NA_WIDTH),
                                       _resident((1, A_WIDTH)), _resident((A_WIDTH + NA_WIDTH, D_MODEL)),
                                       _resident((1, D_MODEL)), _resident((D_MODEL, 2 * D_FF)),
                                       _resident((D_FF, D_MODEL)), _resident((1, D_MODEL))],
        out_specs=out_specs,
        out_shape=out_shapes,
        compiler_params=pltpu.CompilerParams(dimension_semantics=("arbitrary",),
                                             vmem_limit_bytes=VMEM_LIMIT),
        name="post_even",
    )(*parts, o_f, o_b, pa, o_n, gain, w_mix, g_ffn, w_ffn_in, w_ffn_out, g_final)


def _odd_body(final, out_starts, tbl_ref, x_ref, xp_ref, xn_ref, gm_ref, wc_ref, cw_ref, wco_ref, gf_ref,
              wi_ref, wo_ref, gfin_ref, *out_refs):
    step = pl.program_id(0)
    x = x_ref[...]
    gm = gm_ref[...]
    h = _rms(x, gm).astype(BF16)
    z = _dot(h, wc_ref[:, D_MODEL:2 * D_MODEL]) * _dot(h, wc_ref[:, 2 * D_MODEL:3 * D_MODEL])
    hh = _rms(jnp.concatenate([xp_ref[...], xn_ref[...]], axis=0), gm).astype(BF16)
    zh = _dot(hh, wc_ref[:, D_MODEL:2 * D_MODEL]) * _dot(hh, wc_ref[:, 2 * D_MODEL:3 * D_MODEL])
    z_prev = jnp.where(tbl_ref[2, step] == 1, 0.0, zh[HALO - 1:HALO, :])
    z_next = jnp.where(tbl_ref[3, step] == 1, 0.0, zh[HALO:HALO + 1, :])
    row = lax.broadcasted_iota(jnp.int32, (TM, 1), 0)
    z_dn = jnp.where(row == 0, z_prev, pltpu.roll(z, 1, 0))
    z_up = jnp.where(row == TM - 1, z_next, pltpu.roll(z, TM - 1, 0))
    conv = z_dn * cw_ref[0:1, :] + z * cw_ref[1:2, :] + z_up * cw_ref[2:3, :]
    y = (_dot(h, wc_ref[:, 0:D_MODEL]) * conv).astype(BF16)
    acc = _ffn_tail(x + _dot(y, wco_ref[...]), gf_ref, wi_ref, wo_ref, gfin_ref, final)
    _store_parts(out_starts, out_refs, acc)


def _odd_table(segs, t):
    rows = []
    starts = set()
    ends = set()
    for off, batch, seq in segs:
        for bi in range(batch):
            starts.add(off + bi * seq)
            ends.add(off + (bi + 1) * seq)
    for i in range(t // TM):
        t0 = i * TM
        first = int(t0 in starts)
        last = int(t0 + TM in ends)
        rows.append((max(t0 // HALO - 1, 0), min((t0 + TM) // HALO, t // HALO - 1), first, last))
    return np.asarray(rows, np.int32).T.copy()


def _odd(x, g_mix, w_conv_in, conv_w, w_conv_out, g_ffn, w_ffn_in, w_ffn_out, g_final, final, segs, out_tokens):
    t = x.shape[0]
    tbl = _odd_table(segs, t)
    out_starts, out_specs, out_shapes = _out_specs_shapes(out_tokens)
    res = lambda shape: pl.BlockSpec(shape, lambda i, tb: (0,) * len(shape), pipeline_mode=pl.Buffered(1))
    grid_spec = pltpu.PrefetchScalarGridSpec(
        num_scalar_prefetch=1,
        grid=(t // TM,),
        in_specs=[pl.BlockSpec((TM, D_MODEL), lambda i, tb: (i, 0)),
                  pl.BlockSpec((HALO, D_MODEL), lambda i, tb: (tb[0, i], 0)),
                  pl.BlockSpec((HALO, D_MODEL), lambda i, tb: (tb[1, i], 0)),
                  res((1, D_MODEL)), res((D_MODEL, 3 * D_MODEL)), res((CONV_W, D_MODEL)),
                  res((D_MODEL, D_MODEL)), res((1, D_MODEL)), res((D_MODEL, 2 * D_FF)),
                  res((D_FF, D_MODEL)), res((1, D_MODEL))],
        out_specs=out_specs,
    )
    return pl.pallas_call(
        functools.partial(_odd_body, final, out_starts),
        grid_spec=grid_spec,
        out_shape=out_shapes,
        compiler_params=pltpu.CompilerParams(dimension_semantics=("arbitrary",),
                                             vmem_limit_bytes=VMEM_LIMIT),
        name="odd_layer",
    )(jnp.asarray(tbl), x, x, x, g_mix, w_conv_in, conv_w, w_conv_out, g_ffn, w_ffn_in, w_ffn_out, g_final)


def _trunk(xs, norm_mix, norm_ffn, norm_final, even_w_in, even_w_out, hgrn_lb_logits, hgrn_norm,
           na_rpb, conv_w_in, conv_w, conv_w_out, ffn_w_in, ffn_w_out):
    depth = norm_mix.shape[0]
    segs = []
    off = 0
    for a in xs:
        segs.append((off, a.shape[0], a.shape[1]))
        off += a.shape[0] * a.shape[1]
    parts = [a.reshape(-1, D_MODEL) for a in xs]
    part_tokens = [p.shape[0] for p in parts]
    total = off
    row = lambda v: v.reshape(1, -1).astype(F32)
    g_final = row(norm_final)
    for l in range(depth):
        final = l == depth - 1
        out_tokens = part_tokens if final else [total]
        w_fi = ffn_w_in[l].astype(BF16)
        w_fo = ffn_w_out[l].astype(BF16)
        if l % 2 == 0:
            e = l // 2
            pa, pn = _in_even(parts, row(norm_mix[l]), even_w_in[e].astype(BF16))
            o_f, o_b = _hgrn(pa, hgrn_lb_logits.astype(F32), e, segs)
            o_n = _na(pn, na_rpb[e], segs)
            outs = _post_even(parts, o_f, o_b, pa, o_n, row(hgrn_norm[e]), even_w_out[e].astype(BF16),
                              row(norm_ffn[l]), w_fi, w_fo, g_final, final, out_tokens)
        else:
            o = l // 2
            x = parts[0] if len(parts) == 1 else jnp.concatenate(parts, axis=0)
            outs = _odd(x, row(norm_mix[l]), conv_w_in[o].astype(BF16), conv_w[o].astype(F32),
                        conv_w_out[o].astype(BF16), row(norm_ffn[l]), w_fi, w_fo, g_final, final, segs,
                        out_tokens)
        parts = list(outs)
    return tuple(p.reshape(a.shape) for p, a in zip(parts, xs))


def kernel(x_prompt, x_sample, norm_mix, norm_ffn, norm_final, even_w_in, even_w_out, hgrn_lb_logits,
           hgrn_norm, na_rpb, conv_w_in, conv_w, conv_w_out, ffn_w_in, ffn_w_out):
    return _trunk([x_prompt, x_sample], norm_mix, norm_ffn, norm_final, even_w_in, even_w_out,
                  hgrn_lb_logits, hgrn_norm, na_rpb, conv_w_in, conv_w, conv_w_out, ffn_w_in, ffn_w_out)
```

```python
import functools

import numpy as np
import jax
import jax.numpy as jnp
from jax import lax
from jax.experimental import pallas as pl
from jax.experimental.pallas import tpu as pltpu

F32 = jnp.float32
BF16 = jnp.bfloat16

SUBLANES = 8
D_MODEL = 1024
EPS = 1e-6
NEG = -1e30
LOG2E = 1.4426950408889634
GRID_W = 64
A_WIDTH = 512
A_HEADS = 4
A_DK = 128
CHUNK = 64
LEVELS = (2, 4, 8, 16, 32, 64)
NA_WIDTH = 512
NA_HEADS = 8
NA_HD = 64
NA_KR = 8
NA_KC = 16
NA_GROUP = 2
NA_UNROLL = 8
CONV_W = 3
D_FF = 2816
FF_CHUNK = 256
PA_W = 5 * A_WIDTH
PN_W = 3 * NA_WIDTH
EVEN_IN = PA_W + PN_W

TM = 512
SLABS = (slice(0, TM // 2), slice(TM // 2, TM))
HGRN_BLOCK = 512
NA_ROWS = 8
NA_WIN = 3 * NA_ROWS
HALO = 8
VMEM_LIMIT = 56 * 1024 * 1024

NT_DIMS = (((1,), (1,)), ((), ()))
TN_DIMS = (((0,), (0,)), ((), ()))


def _resident(shape):
    nd = len(shape)
    return pl.BlockSpec(shape, lambda *_: (0,) * nd, pipeline_mode=pl.Buffered(1))


def _rms(x, g):
    return x * lax.rsqrt(jnp.mean(x * x, axis=-1, keepdims=True) + EPS) * g


def _sigmoid(x):
    return jax.nn.sigmoid(x)


def _dot(a, b):
    return jnp.dot(a, b, preferred_element_type=F32)


def _dot_nt(a, b):
    return lax.dot_general(a, b, NT_DIMS, preferred_element_type=F32)


def _tile_starts(parts):
    starts = [0]
    for a in parts:
        starts.append(starts[-1] + a.shape[0] // TM)
    return tuple(starts)


def _part_specs(parts):
    starts = _tile_starts(parts)
    return [pl.BlockSpec((TM, D_MODEL),
                         lambda i, *_, s=starts[k], n=starts[k + 1] - starts[k]: (jnp.clip(i - s, 0, n - 1), 0))
            for k in range(len(parts))]


def _select_part(starts, refs):
    step = pl.program_id(0)
    x = refs[-1][...]
    for k in range(len(refs) - 2, -1, -1):
        x = jnp.where(step < starts[k + 1], refs[k][...], x)
    return x


def _in_even_body(starts, *refs):
    n = len(starts) - 1
    g_ref, w_ref, pa_ref, pn_ref = refs[n:]
    h = _rms(_select_part(starts, refs[:n]), g_ref[...]).astype(BF16)
    for c in range(0, PA_W, 512):
        pa_ref[:, c:c + 512] = _dot(h, w_ref[:, c:c + 512])
    for c in range(0, PN_W, 512):
        pn_ref[:, c:c + 512] = _dot(h, w_ref[:, PA_W + c:PA_W + c + 512]).astype(BF16)


def _in_even(parts, g, w):
    starts = _tile_starts(parts)
    t = starts[-1] * TM
    return pl.pallas_call(
        functools.partial(_in_even_body, starts),
        grid=(starts[-1],),
        in_specs=_part_specs(parts) + [_resident((1, D_MODEL)), _resident((D_MODEL, EVEN_IN))],
        out_specs=[pl.BlockSpec((TM, PA_W), lambda i: (i, 0)),
                   pl.BlockSpec((TM, PN_W), lambda i: (i, 0))],
        out_shape=[jax.ShapeDtypeStruct((t, PA_W), F32),
                   jax.ShapeDtypeStruct((t, PN_W), BF16)],
        compiler_params=pltpu.CompilerParams(dimension_semantics=("arbitrary",),
                                             vmem_limit_bytes=VMEM_LIMIT),
        name="in_even",
    )(*parts, g, w)


def _hgrn_masks():
    t = np.arange(CHUNK)[:, None]
    s = np.arange(CHUNK)[None, :]
    out = np.zeros((2, 1 + len(LEVELS), CHUNK, CHUNK), np.float32)
    for d, rev in enumerate((False, True)):
        out[d, 0] = (t == s)
        for li, L in enumerate(LEVELS):
            half = L // 2
            same = (t // L) == (s // L)
            if rev:
                m = same & ((t % L) < half) & ((s % L) >= half)
            else:
                m = same & ((t % L) >= half) & ((s % L) < half)
            out[d, 1 + li] = m
    return np.tile(out, (1, 1, 1, 2))


def _hgrn_signs():
    t = np.arange(CHUNK)
    out = np.zeros((2, len(LEVELS), CHUNK), np.float32)
    for d, rev in enumerate((False, True)):
        for li, L in enumerate(LEVELS):
            second = (t % L) >= L // 2
            out[d, li] = np.where(second != rev, 1.0, -1.0)
    return np.ascontiguousarray(np.broadcast_to(out[..., None], out.shape + (A_DK,)))


def _tri(rev):
    t = lax.broadcasted_iota(jnp.int32, (CHUNK, CHUNK), 0)
    s = lax.broadcasted_iota(jnp.int32, (CHUNK, CHUNK), 1)
    return jnp.where((s >= t) if rev else (s <= t), 1.0, 0.0).astype(BF16)


def _level_ref(b3, L, rev):
    g, sub, w = b3.shape
    half = L // 2
    if L >= 2 * SUBLANES:
        n = L // SUBLANES
        pieces = []
        for blk in range(CHUNK // L):
            r = blk * L + (half if rev else half - 1)
            pieces.append(jnp.broadcast_to(b3[r // SUBLANES:r // SUBLANES + 1, r % SUBLANES:r % SUBLANES + 1, :],
                                           (n, sub, w)))
        return pieces[0] if len(pieces) == 1 else jnp.concatenate(pieces, axis=0)
    srow = lax.broadcasted_iota(jnp.int32, b3.shape, 1)
    if L == SUBLANES:
        r = half if rev else half - 1
        return jnp.broadcast_to(b3[:, r:r + 1, :], b3.shape)
    if L == 4:
        r = half if rev else half - 1
        lo = jnp.broadcast_to(b3[:, r:r + 1, :], b3.shape)
        hi = jnp.broadcast_to(b3[:, r + 4:r + 5, :], b3.shape)
        return jnp.where(srow < 4, lo, hi)
    odd = (srow % 2) == 1
    if rev:
        return jnp.where(odd, b3, pltpu.roll(b3, SUBLANES - 1, 1))
    return jnp.where(odd, pltpu.roll(b3, 1, 1), b3)


def _pair_blockdiag(x):
    zero = jnp.zeros((CHUNK, A_DK), x.dtype)
    return jnp.concatenate([jnp.concatenate([x[:, :A_DK], zero], axis=1),
                            jnp.concatenate([zero, x[:, A_DK:]], axis=1)], axis=0)


def _hgrn_chunk(q, k, v, b, s_ref, d, rev, m_ref, sgn_ref):
    tot = b[0:1, :] if rev else b[CHUNK - 1:CHUNK, :]
    qb = q.astype(BF16)
    kb = k.astype(BF16)
    q0 = qb * jnp.exp2(b).astype(BF16)
    k2 = kb * jnp.exp2(tot - b).astype(BF16)
    vb = v.astype(BF16)
    dec = jnp.exp2(tot)
    inter = []
    for h in range(A_HEADS):
        sl = slice(h * A_DK, (h + 1) * A_DK)
        s_t = s_ref[d, h]
        inter.append(_dot_nt(q0[:, sl], s_t.astype(BF16)))
        ds = lax.dot_general(vb[:, sl], k2[:, sl], TN_DIMS, preferred_element_type=F32)
        s_ref[d, h] = dec[:, sl] * s_t + ds
    pairs = [slice(p * 2 * A_DK, (p + 1) * 2 * A_DK) for p in range(A_HEADS // 2)]
    b3 = b.reshape(CHUNK // SUBLANES, SUBLANES, A_WIDTH)
    acc = [_dot_nt(qb[:, p], _pair_blockdiag(kb[:, p])) * m_ref[d, 0] for p in pairs]
    for li, L in enumerate(LEVELS):
        sgn = jnp.concatenate([sgn_ref[d, li]] * A_HEADS, axis=1).reshape(b3.shape)
        e = jnp.exp2((b3 - _level_ref(b3, L, rev)) * sgn).reshape(CHUNK, A_WIDTH).astype(BF16)
        qs = qb * e
        ks = kb * e
        acc = [a + _dot_nt(qs[:, p], _pair_blockdiag(ks[:, p])) * m_ref[d, 1 + li] for a, p in zip(acc, pairs)]
    intra = [_dot(a.astype(BF16), _pair_blockdiag(vb[:, p])) for a, p in zip(acc, pairs)]
    return jnp.concatenate(intra, axis=1) + jnp.concatenate(inter, axis=1)


def _hgrn_body(layer, tbl_ref, qf_ref, zf_ref, vf_ref, qb_ref, zb_ref, vb_ref, lbl_ref, m_ref, sgn_ref,
               of_ref, ob_ref, s_ref):
    step = pl.program_id(0)

    @pl.when(tbl_ref[2, step] == 1)
    def _():
        s_ref[...] = jnp.zeros_like(s_ref)

    n_even = lbl_ref.shape[0]
    logits = [lbl_ref[i] for i in range(n_even)]
    mx = functools.reduce(jnp.maximum, logits)
    ex = [jnp.exp(l - mx) for l in logits]
    den = functools.reduce(lambda x, y: x + y, ex)
    ps = [e / den for e in ex]
    lbs = functools.reduce(lambda x, y: x + y, ps[:layer + 1]) - ps[0]

    nc = HGRN_BLOCK // CHUNK
    refs = ((qf_ref, zf_ref, vf_ref, of_ref), (qb_ref, zb_ref, vb_ref, ob_ref))
    tris = (_tri(False), _tri(True))

    def chunk(c, carry):
        prep = []
        for d, rev in enumerate((False, True)):
            q_ref, z_ref, v_ref, _ = refs[d]
            r0 = pl.multiple_of(((nc - 1 - c) if rev else c) * CHUNK, CHUNK)
            rows = pl.ds(r0, CHUNK)
            qa = q_ref[rows, :]
            z = z_ref[rows, :]
            lb = lbs[d:d + 1, :]
            q = qa * _sigmoid(qa)
            f = lb + (1.0 - lb) * _sigmoid(z)
            k = 1.0 - f
            lf = jnp.log(f)
            hi = lf.astype(BF16)
            r1 = lf - hi.astype(F32)
            mid = r1.astype(BF16)
            lo = (r1 - mid.astype(F32)).astype(BF16)
            cs = _dot(tris[d], jnp.concatenate([hi, mid, lo], axis=1))
            b = (cs[:, :A_WIDTH] + cs[:, A_WIDTH:2 * A_WIDTH] + cs[:, 2 * A_WIDTH:]) * LOG2E
            prep.append((rows, q, k, v_ref[rows, :], b))
        for d, rev in enumerate((False, True)):
            rows, q, k, v, b = prep[d]
            refs[d][3][rows, :] = _hgrn_chunk(q, k, v, b, s_ref, d, rev, m_ref, sgn_ref)
        return carry

    lax.fori_loop(0, nc, chunk, 0, unroll=True)


def _hgrn_table(segs):
    rows = []
    for off, batch, seq in segs:
        nblk = seq // HGRN_BLOCK
        for bi in range(batch):
            base = (off + bi * seq) // HGRN_BLOCK
            for i in range(nblk):
                rows.append((base + i, base + nblk - 1 - i, int(i == 0)))
    return np.asarray(rows, np.int32).T.copy()


def _hgrn(pa, lb_logits, layer, segs):
    t = pa.shape[0]
    tbl = _hgrn_table(segs)
    steps = tbl.shape[1]
    masks = _hgrn_masks()
    signs = _hgrn_signs()

    def spec(col, which):
        return pl.BlockSpec((HGRN_BLOCK, A_WIDTH), lambda i, tb: (tb[which, i], col))

    grid_spec = pltpu.PrefetchScalarGridSpec(
        num_scalar_prefetch=1,
        grid=(steps,),
        in_specs=[spec(0, 0), spec(1, 0), spec(3, 0), spec(0, 1), spec(2, 1), spec(3, 1),
                  pl.BlockSpec(lb_logits.shape, lambda i, tb: (0, 0, 0)),
                  pl.BlockSpec(masks.shape, lambda i, tb: (0, 0, 0, 0)),
                  pl.BlockSpec(signs.shape, lambda i, tb: (0, 0, 0, 0))],
        out_specs=[pl.BlockSpec((HGRN_BLOCK, A_WIDTH), lambda i, tb: (tb[0, i], 0)),
                   pl.BlockSpec((HGRN_BLOCK, A_WIDTH), lambda i, tb: (tb[1, i], 0))],
        scratch_shapes=[pltpu.VMEM((2, A_HEADS, A_DK, A_DK), F32)],
    )
    return pl.pallas_call(
        functools.partial(_hgrn_body, layer),
        grid_spec=grid_spec,
        out_shape=[jax.ShapeDtypeStruct((t, A_WIDTH), F32),
                   jax.ShapeDtypeStruct((t, A_WIDTH), F32)],
        compiler_params=pltpu.CompilerParams(dimension_semantics=("arbitrary",),
                                             vmem_limit_bytes=VMEM_LIMIT),
        name="hgrn",
    )(jnp.asarray(tbl), pa, pa, pa, pa, pa, pa, lb_logits, jnp.asarray(masks), jnp.asarray(signs))


def _na_build_bias(rpb_ref, bias_ref):
    n_ro = 2 * NA_KR - 1
    n_co = 2 * NA_KC - 1
    qc = lax.broadcasted_iota(jnp.int32, (GRID_W, GRID_W), 0)
    kc = lax.broadcasted_iota(jnp.int32, (GRID_W, GRID_W), 1)
    co = jnp.clip(kc - qc, -(NA_KC - 1), NA_KC - 1) + NA_KC - 1
    ws = jnp.clip(qc - NA_KC // 2, 0, GRID_W - NA_KC)
    valid = (kc >= ws) & (kc < ws + NA_KC)

    def build(hr, carry):
        h = hr // n_ro
        ro = hr % n_ro
        t = jnp.full((GRID_W, GRID_W), NEG, F32)
        for j in range(n_co):
            t = jnp.where(co == j, rpb_ref[hr * n_co + j], t)
        t = jnp.where(valid, t, NEG)
        for i in range(NA_KR):
            dd = ro - i

            @pl.when((dd >= 0) & (dd < NA_KR))
            def _():
                bias_ref[h, dd, :, i * GRID_W:(i + 1) * GRID_W] = t
        return carry

    lax.fori_loop(0, NA_HEADS * n_ro, build, 0)


def _na_body(tbl_ref, rpb_ref, q_ref, k_ref, v_ref, o_ref, bias_ref):
    step = pl.program_id(0)

    @pl.when(step == 0)
    def _():
        _na_build_bias(rpb_ref, bias_ref)

    r0 = tbl_ref[2, step]
    w0 = tbl_ref[3, step]
    n_rows = tbl_ref[4, step]
    head = lax.broadcasted_iota(jnp.int32, (GRID_W, NA_GROUP * NA_HD), 1) // NA_HD
    nkeys = NA_KR * GRID_W
    scale = jnp.asarray(NA_HD ** -0.5, BF16)

    groups = [slice(g * NA_GROUP * NA_HD, (g + 1) * NA_GROUP * NA_HD) for g in range(NA_HEADS // NA_GROUP)]

    def rows(it, carry):
        units = []
        for u in range(NA_UNROLL):
            rr = it * NA_UNROLL + u
            r = r0 + rr
            row_start = jnp.clip(r - NA_KR // 2, 0, n_rows - NA_KR)
            dd = row_start - r + (NA_KR - 1)
            keys = pl.ds(pl.multiple_of((row_start - w0) * GRID_W, GRID_W), nkeys)
            qrows = pl.ds(pl.multiple_of(rr * GRID_W, GRID_W), GRID_W)
            units += [(dd, keys, qrows, g, lanes) for g, lanes in enumerate(groups)]
        scores = []
        for dd, keys, qrows, g, lanes in units:
            qp = q_ref[qrows, lanes] * scale
            zero = jnp.zeros_like(qp)
            qs = jnp.concatenate([jnp.where(head == h, qp, zero) for h in range(NA_GROUP)], axis=0)
            scores.append(_dot_nt(qs, k_ref[keys, lanes]))
        probs = []
        for (dd, keys, qrows, g, lanes), s in zip(units, scores):
            s = s + jnp.concatenate([bias_ref[NA_GROUP * g + h, dd] for h in range(NA_GROUP)], axis=0)
            m = jnp.max(s, axis=-1, keepdims=True)
            p = jnp.exp(s - m)
            probs.append((p.astype(BF16), jnp.sum(p, axis=-1, keepdims=True)))
        for (dd, keys, qrows, g, lanes), (p, l) in zip(units, probs):
            pv = _dot(p, v_ref[keys, lanes]) / l
            o = pv[:GRID_W]
            for h in range(1, NA_GROUP):
                o = jnp.where(head == h, pv[h * GRID_W:(h + 1) * GRID_W], o)
            o_ref[qrows, lanes] = o.astype(o_ref.dtype)
        return carry

    lax.fori_loop(0, NA_ROWS // NA_UNROLL, rows, 0)


def _na_table(segs):
    blk = NA_ROWS * GRID_W
    rows = []
    for off, batch, seq in segs:
        n_rows = seq // GRID_W
        assert n_rows >= NA_WIN
        for bi in range(batch):
            base = off + bi * seq
            for i in range(seq // blk):
                r0 = i * NA_ROWS
                w0 = min(max(r0 - NA_ROWS, 0), n_rows - NA_WIN)
                rows.append((base // blk + i, base // GRID_W + w0, r0, w0, n_rows))
    return np.asarray(rows, np.int32).T.copy()


def _na(pn, rpb, segs):
    t = pn.shape[0]
    tbl = _na_table(segs)
    steps = tbl.shape[1]
    blk = NA_ROWS * GRID_W

    def window(col):
        return pl.BlockSpec((pl.Element(NA_WIN * GRID_W), pl.Element(NA_WIDTH)),
                            lambda i, tb: (tb[1, i] * GRID_W, col * NA_WIDTH))

    grid_spec = pltpu.PrefetchScalarGridSpec(
        num_scalar_prefetch=1,
        grid=(steps,),
        in_specs=[pl.BlockSpec(memory_space=pltpu.SMEM),
                  pl.BlockSpec((blk, NA_WIDTH), lambda i, tb: (tb[0, i], 0)),
                  window(1), window(2)],
        out_specs=pl.BlockSpec((blk, NA_WIDTH), lambda i, tb: (tb[0, i], 0)),
        scratch_shapes=[pltpu.VMEM((NA_HEADS, NA_KR, GRID_W, NA_KR * GRID_W), F32)],
    )
    return pl.pallas_call(
        _na_body,
        grid_spec=grid_spec,
        out_shape=jax.ShapeDtypeStruct((t, NA_WIDTH), BF16),
        compiler_params=pltpu.CompilerParams(dimension_semantics=("arbitrary",),
                                             vmem_limit_bytes=VMEM_LIMIT),
        name="natten",
    )(jnp.asarray(tbl), rpb.astype(F32).reshape(-1), pn, pn, pn)


def _out_specs_shapes(out_tokens):
    starts = [0]
    for t in out_tokens:
        starts.append(starts[-1] + t // TM)
    specs = []
    for k in range(len(out_tokens)):
        s, n = starts[k], starts[k + 1] - starts[k]
        specs.append(pl.BlockSpec((TM, D_MODEL), lambda i, *_, s=s, n=n: (jnp.clip(i - s, 0, n - 1), 0)))
    shapes = [jax.ShapeDtypeStruct((t, D_MODEL), F32) for t in out_tokens]
    return tuple(starts), specs, shapes


def _store_parts(starts, out_refs, accs):
    def store(o_ref):
        for sl, acc in zip(SLABS, accs):
            o_ref[sl, :] = acc

    if len(out_refs) == 1:
        store(out_refs[0])
        return
    step = pl.program_id(0)
    for k, o_ref in enumerate(out_refs):
        @pl.when((step >= starts[k]) & (step < starts[k + 1]))
        def _(o_ref=o_ref):
            store(o_ref)


def _ffn_tail(x1s, gf_ref, wi_ref, wo_ref, gfin_ref, final):
    hs = [_rms(x1, gf_ref[...]).astype(BF16) for x1 in x1s]
    accs = list(x1s)
    for c in range(0, D_FF, FF_CHUNK):
        gus = [(_dot(h, wi_ref[:, c:c + FF_CHUNK]), _dot(h, wi_ref[:, D_FF + c:D_FF + c + FF_CHUNK])) for h in hs]
        acts = [(g * _sigmoid(g) * u).astype(BF16) for g, u in gus]
        accs = [acc + _dot(a, wo_ref[c:c + FF_CHUNK, :]) for acc, a in zip(accs, acts)]
    if final:
        accs = [_rms(acc, gfin_ref[...]) for acc in accs]
    return accs


def _post_even_body(final, starts, out_starts, *refs):
    n = len(starts) - 1
    of_ref, ob_ref, ga_ref, n_ref, gain_ref, wm_ref, gf_ref, wi_ref, wo_ref, gfin_ref = refs[n:n + 10]
    x = _select_part(starts, refs[:n])
    x1s = []
    for sl in SLABS:
        o = of_ref[sl, :] + ob_ref[sl, :]
        parts = []
        for h in range(A_HEADS):
            oh = o[:, h * A_DK:(h + 1) * A_DK]
            parts.append(oh * lax.rsqrt(jnp.mean(oh * oh, axis=-1, keepdims=True) + EPS))
        on = jnp.concatenate(parts, axis=1) * gain_ref[...]
        g = ga_ref[sl, :]
        oa = (on * (g * _sigmoid(g))).astype(BF16)
        mix = _dot(oa, wm_ref[0:A_WIDTH, :]) + _dot(n_ref[sl, :], wm_ref[A_WIDTH:A_WIDTH + NA_WIDTH, :])
        x1s.append(x[sl, :] + mix)
    accs = _ffn_tail(x1s, gf_ref, wi_ref, wo_ref, gfin_ref, final)
    _store_parts(out_starts, refs[n + 10:], accs)


def _post_even(parts, o_f, o_b, pa, o_n, gain, w_mix, g_ffn, w_ffn_in, w_ffn_out, g_final, final, out_tokens):
    starts = _tile_starts(parts)
    out_starts, out_specs, out_shapes = _out_specs_shapes(out_tokens)
    tok = lambda w, col=0: pl.BlockSpec((TM, w), lambda i: (i, col))
    return pl.pallas_call(
        functools.partial(_post_even_body, final, starts, out_starts),
        grid=(starts[-1],),
        in_specs=_part_specs(parts) + [tok(A_WIDTH), tok(A_WIDTH), tok(A_WIDTH, 4), tok(NA_WIDTH),
                                       _resident((1, A_WIDTH)), _resident((A_WIDTH + NA_WIDTH, D_MODEL)),
                                       _resident((1, D_MODEL)), _resident((D_MODEL, 2 * D_FF)),
                                       _resident((D_FF, D_MODEL)), _resident((1, D_MODEL))],
        out_specs=out_specs,
        out_shape=out_shapes,
        compiler_params=pltpu.CompilerParams(dimension_semantics=("arbitrary",),
                                             vmem_limit_bytes=VMEM_LIMIT),
        name="post_even",
    )(*parts, o_f, o_b, pa, o_n, gain, w_mix, g_ffn, w_ffn_in, w_ffn_out, g_final)


def _odd_body(final, out_starts, tbl_ref, x_ref, xp_ref, xn_ref, gm_ref, wc_ref, cw_ref, wco_ref, gf_ref,
              wi_ref, wo_ref, gfin_ref, *out_refs):
    step = pl.program_id(0)
    x = x_ref[...]
    gm = gm_ref[...]
    hs = [_rms(x[sl, :], gm).astype(BF16) for sl in SLABS]
    zs = [_dot(h, wc_ref[:, D_MODEL:2 * D_MODEL]) * _dot(h, wc_ref[:, 2 * D_MODEL:3 * D_MODEL]) for h in hs]
    hh = _rms(jnp.concatenate([xp_ref[...], xn_ref[...]], axis=0), gm).astype(BF16)
    zh = _dot(hh, wc_ref[:, D_MODEL:2 * D_MODEL]) * _dot(hh, wc_ref[:, 2 * D_MODEL:3 * D_MODEL])
    bgs = [_dot(h, wc_ref[:, 0:D_MODEL]) for h in hs]
    z_prev = jnp.where(tbl_ref[2, step] == 1, 0.0, zh[HALO - 1:HALO, :])
    z_next = jnp.where(tbl_ref[3, step] == 1, 0.0, zh[HALO:HALO + 1, :])
    rows = TM // len(SLABS)
    row = lax.broadcasted_iota(jnp.int32, (rows, 1), 0)
    x1s = []
    for k, (sl, z, bg) in enumerate(zip(SLABS, zs, bgs)):
        above = z_prev if k == 0 else zs[k - 1][rows - 1:rows, :]
        below = z_next if k == len(SLABS) - 1 else zs[k + 1][0:1, :]
        z_dn = jnp.where(row == 0, above, pltpu.roll(z, 1, 0))
        z_up = jnp.where(row == rows - 1, below, pltpu.roll(z, rows - 1, 0))
        conv = z_dn * cw_ref[0:1, :] + z * cw_ref[1:2, :] + z_up * cw_ref[2:3, :]
        x1s.append(x[sl, :] + _dot((bg * conv).astype(BF16), wco_ref[...]))
    accs = _ffn_tail(x1s, gf_ref, wi_ref, wo_ref, gfin_ref, final)
    _store_parts(out_starts, out_refs, accs)


def _odd_table(segs, t):
    rows = []
    starts = set()
    ends = set()
    for off, batch, seq in segs:
        for bi in range(batch):
            starts.add(off + bi * seq)
            ends.add(off + (bi + 1) * seq)
    for i in range(t // TM):
        t0 = i * TM
        first = int(t0 in starts)
        last = int(t0 + TM in ends)
        rows.append((max(t0 // HALO - 1, 0), min((t0 + TM) // HALO, t // HALO - 1), first, last))
    return np.asarray(rows, np.int32).T.copy()


def _odd(x, g_mix, w_conv_in, conv_w, w_conv_out, g_ffn, w_ffn_in, w_ffn_out, g_final, final, segs, out_tokens):
    t = x.shape[0]
    tbl = _odd_table(segs, t)
    out_starts, out_specs, out_shapes = _out_specs_shapes(out_tokens)
    res = lambda shape: pl.BlockSpec(shape, lambda i, tb: (0,) * len(shape), pipeline_mode=pl.Buffered(1))
    grid_spec = pltpu.PrefetchScalarGridSpec(
        num_scalar_prefetch=1,
        grid=(t // TM,),
        in_specs=[pl.BlockSpec((TM, D_MODEL), lambda i, tb: (i, 0)),
                  pl.BlockSpec((HALO, D_MODEL), lambda i, tb: (tb[0, i], 0)),
                  pl.BlockSpec((HALO, D_MODEL), lambda i, tb: (tb[1, i], 0)),
                  res((1, D_MODEL)), res((D_MODEL, 3 * D_MODEL)), res((CONV_W, D_MODEL)),
                  res((D_MODEL, D_MODEL)), res((1, D_MODEL)), res((D_MODEL, 2 * D_FF)),
                  res((D_FF, D_MODEL)), res((1, D_MODEL))],
        out_specs=out_specs,
    )
    return pl.pallas_call(
        functools.partial(_odd_body, final, out_starts),
        grid_spec=grid_spec,
        out_shape=out_shapes,
        compiler_params=pltpu.CompilerParams(dimension_semantics=("arbitrary",),
                                             vmem_limit_bytes=VMEM_LIMIT),
        name="odd_layer",
    )(jnp.asarray(tbl), x, x, x, g_mix, w_conv_in, conv_w, w_conv_out, g_ffn, w_ffn_in, w_ffn_out, g_final)


def _trunk(xs, norm_mix, norm_ffn, norm_final, even_w_in, even_w_out, hgrn_lb_logits, hgrn_norm,
           na_rpb, conv_w_in, conv_w, conv_w_out, ffn_w_in, ffn_w_out):
    depth = norm_mix.shape[0]
    segs = []
    off = 0
    for a in xs:
        segs.append((off, a.shape[0], a.shape[1]))
        off += a.shape[0] * a.shape[1]
    parts = [a.reshape(-1, D_MODEL) for a in xs]
    part_tokens = [p.shape[0] for p in parts]
    total = off
    row = lambda v: v.reshape(1, -1).astype(F32)
    g_final = row(norm_final)
    for l in range(depth):
        final = l == depth - 1
        out_tokens = part_tokens if final else [total]
        w_fi = ffn_w_in[l].astype(BF16)
        w_fo = ffn_w_out[l].astype(BF16)
        if l % 2 == 0:
            e = l // 2
            pa, pn = _in_even(parts, row(norm_mix[l]), even_w_in[e].astype(BF16))
            o_f, o_b = _hgrn(pa, hgrn_lb_logits.astype(F32), e, segs)
            o_n = _na(pn, na_rpb[e], segs)
            outs = _post_even(parts, o_f, o_b, pa, o_n, row(hgrn_norm[e]), even_w_out[e].astype(BF16),
                              row(norm_ffn[l]), w_fi, w_fo, g_final, final, out_tokens)
        else:
            o = l // 2
            x = parts[0] if len(parts) == 1 else jnp.concatenate(parts, axis=0)
            outs = _odd(x, row(norm_mix[l]), conv_w_in[o].astype(BF16), conv_w[o].astype(F32),
                        conv_w_out[o].astype(BF16), row(norm_ffn[l]), w_fi, w_fo, g_final, final, segs,
                        out_tokens)
        parts = list(outs)
    return tuple(p.reshape(a.shape) for p, a in zip(parts, xs))


def kernel(x_prompt, x_sample, norm_mix, norm_ffn, norm_final, even_w_in, even_w_out, hgrn_lb_logits,
           hgrn_norm, na_rpb, conv_w_in, conv_w, conv_w_out, ffn_w_in, ffn_w_out):
    return _trunk([x_prompt, x_sample], norm_mix, norm_ffn, norm_final, even_w_in, even_w_out,
                  hgrn_lb_logits, hgrn_norm, na_rpb, conv_w_in, conv_w, conv_w_out, ffn_w_in, ffn_w_out)
```

```python
import functools

import numpy as np
import jax
import jax.numpy as jnp
from jax import lax
from jax.experimental import pallas as pl
from jax.experimental.pallas import tpu as pltpu

F32 = jnp.float32
BF16 = jnp.bfloat16

SUBLANES = 8
D_MODEL = 1024
EPS = 1e-6
NEG = -1e30
LOG2E = 1.4426950408889634
GRID_W = 64
A_WIDTH = 512
A_HEADS = 4
A_DK = 128
CHUNK = 64
LEVELS = (2, 4, 8, 16, 32, 64)
NA_WIDTH = 512
NA_HEADS = 8
NA_HD = 64
NA_KR = 8
NA_KC = 16
NA_GROUP = 2
NA_UNROLL = 8
CONV_W = 3
D_FF = 2816
FF_CHUNK = 256
PA_W = 5 * A_WIDTH
PN_W = 3 * NA_WIDTH
EVEN_IN = PA_W + PN_W

TM = 512
SLABS = (slice(0, TM // 2), slice(TM // 2, TM))
HGRN_BLOCK = 512
NA_ROWS = 8
NA_WIN = 3 * NA_ROWS
HALO = 8
VMEM_LIMIT = 56 * 1024 * 1024

NT_DIMS = (((1,), (1,)), ((), ()))
TN_DIMS = (((0,), (0,)), ((), ()))


def _resident(shape):
    nd = len(shape)
    return pl.BlockSpec(shape, lambda *_: (0,) * nd, pipeline_mode=pl.Buffered(1))


def _rms(x, g):
    return x * lax.rsqrt(jnp.mean(x * x, axis=-1, keepdims=True) + EPS) * g


def _sigmoid(x):
    return jax.nn.sigmoid(x)


def _dot(a, b):
    return jnp.dot(a, b, preferred_element_type=F32)


def _dot_nt(a, b):
    return lax.dot_general(a, b, NT_DIMS, preferred_element_type=F32)


def _tile_starts(parts):
    starts = [0]
    for a in parts:
        starts.append(starts[-1] + a.shape[0] // TM)
    return tuple(starts)


def _part_specs(parts):
    starts = _tile_starts(parts)
    return [pl.BlockSpec((TM, D_MODEL),
                         lambda i, *_, s=starts[k], n=starts[k + 1] - starts[k]: (jnp.clip(i - s, 0, n - 1), 0))
            for k in range(len(parts))]


def _select_part(starts, refs):
    step = pl.program_id(0)
    x = refs[-1][...]
    for k in range(len(refs) - 2, -1, -1):
        x = jnp.where(step < starts[k + 1], refs[k][...], x)
    return x


def _in_even_body(starts, *refs):
    n = len(starts) - 1
    g_ref, w_ref, pa_ref, pn_ref = refs[n:]
    h = _rms(_select_part(starts, refs[:n]), g_ref[...]).astype(BF16)
    for c in range(0, PA_W, 512):
        pa_ref[:, c:c + 512] = _dot(h, w_ref[:, c:c + 512])
    for c in range(0, PN_W, 512):
        pn_ref[:, c:c + 512] = _dot(h, w_ref[:, PA_W + c:PA_W + c + 512]).astype(BF16)


def _in_even(parts, g, w):
    starts = _tile_starts(parts)
    t = starts[-1] * TM
    return pl.pallas_call(
        functools.partial(_in_even_body, starts),
        grid=(starts[-1],),
        in_specs=_part_specs(parts) + [_resident((1, D_MODEL)), _resident((D_MODEL, EVEN_IN))],
        out_specs=[pl.BlockSpec((TM, PA_W), lambda i: (i, 0)),
                   pl.BlockSpec((TM, PN_W), lambda i: (i, 0))],
        out_shape=[jax.ShapeDtypeStruct((t, PA_W), F32),
                   jax.ShapeDtypeStruct((t, PN_W), BF16)],
        compiler_params=pltpu.CompilerParams(dimension_semantics=("arbitrary",),
                                             vmem_limit_bytes=VMEM_LIMIT),
        name="in_even",
    )(*parts, g, w)


def _hgrn_masks():
    t = np.arange(CHUNK)[:, None]
    s = np.arange(CHUNK)[None, :]
    out = np.zeros((2, len(LEVELS) - 1, CHUNK, CHUNK), np.float32)
    for d, rev in enumerate((False, True)):
        for li, L in enumerate(LEVELS[1:]):
            half = L // 2
            same = (t // L) == (s // L)
            if rev:
                m = same & ((t % L) < half) & ((s % L) >= half)
            else:
                m = same & ((t % L) >= half) & ((s % L) < half)
            out[d, li] = m
    return np.tile(out, (1, 1, 1, 2))


def _hgrn_signs():
    t = np.arange(CHUNK)
    out = np.zeros((2, len(LEVELS), CHUNK), np.float32)
    for d, rev in enumerate((False, True)):
        for li, L in enumerate(LEVELS):
            second = (t % L) >= L // 2
            out[d, li] = np.where(second != rev, 1.0, -1.0)
    return np.ascontiguousarray(np.broadcast_to(out[..., None], out.shape + (A_DK,)))


def _tri(rev):
    t = lax.broadcasted_iota(jnp.int32, (CHUNK, CHUNK), 0)
    s = lax.broadcasted_iota(jnp.int32, (CHUNK, CHUNK), 1)
    return jnp.where((s >= t) if rev else (s <= t), 1.0, 0.0).astype(BF16)


def _level_ref(b3, L, rev):
    g, sub, w = b3.shape
    half = L // 2
    if L >= 2 * SUBLANES:
        n = L // SUBLANES
        pieces = []
        for blk in range(CHUNK // L):
            r = blk * L + (half if rev else half - 1)
            pieces.append(jnp.broadcast_to(b3[r // SUBLANES:r // SUBLANES + 1, r % SUBLANES:r % SUBLANES + 1, :],
                                           (n, sub, w)))
        return pieces[0] if len(pieces) == 1 else jnp.concatenate(pieces, axis=0)
    srow = lax.broadcasted_iota(jnp.int32, b3.shape, 1)
    if L == SUBLANES:
        r = half if rev else half - 1
        return jnp.broadcast_to(b3[:, r:r + 1, :], b3.shape)
    if L == 4:
        r = half if rev else half - 1
        lo = jnp.broadcast_to(b3[:, r:r + 1, :], b3.shape)
        hi = jnp.broadcast_to(b3[:, r + 4:r + 5, :], b3.shape)
        return jnp.where(srow < 4, lo, hi)
    odd = (srow % 2) == 1
    if rev:
        return jnp.where(odd, b3, pltpu.roll(b3, SUBLANES - 1, 1))
    return jnp.where(odd, pltpu.roll(b3, 1, 1), b3)


def _pair_blockdiag(x):
    zero = jnp.zeros((CHUNK, A_DK), x.dtype)
    return jnp.concatenate([jnp.concatenate([x[:, :A_DK], zero], axis=1),
                            jnp.concatenate([zero, x[:, A_DK:]], axis=1)], axis=0)


def _hgrn_chunk(q, k, v, b, s_ref, d, rev, m_ref, sgn_ref):
    tot = b[0:1, :] if rev else b[CHUNK - 1:CHUNK, :]
    qb = q.astype(BF16)
    kb = k.astype(BF16)
    q0 = qb * jnp.exp2(b).astype(BF16)
    k2 = kb * jnp.exp2(tot - b).astype(BF16)
    vb = v.astype(BF16)
    dec = jnp.exp2(tot)
    inter = []
    for h in range(A_HEADS):
        sl = slice(h * A_DK, (h + 1) * A_DK)
        s_t = s_ref[d, h]
        inter.append(_dot_nt(q0[:, sl], s_t.astype(BF16)))
        ds = lax.dot_general(vb[:, sl], k2[:, sl], TN_DIMS, preferred_element_type=F32)
        s_ref[d, h] = dec[:, sl] * s_t + ds
    b3 = b.reshape(CHUNK // SUBLANES, SUBLANES, A_WIDTH)

    def level_decay(li):
        sgn = jnp.concatenate([sgn_ref[d, li]] * A_HEADS, axis=1).reshape(b3.shape)
        return jnp.exp2((b3 - _level_ref(b3, LEVELS[li], rev)) * sgn)

    shift = SUBLANES - 1 if rev else 1
    k_adj = pltpu.roll(k.reshape(b3.shape), shift, 1).reshape(CHUNK, A_WIDTH)
    v_adj = pltpu.roll(v.reshape(b3.shape), shift, 1).reshape(CHUNK, A_WIDTH)
    p_same = q * k
    p_adj = q * level_decay(0).reshape(CHUNK, A_WIDTH) * k_adj
    row = lax.broadcasted_iota(jnp.int32, (CHUNK, 1), 0)
    is_query = (row % 2 == 0) if rev else (row % 2 == 1)
    near = []
    for h in range(A_HEADS):
        sl = slice(h * A_DK, (h + 1) * A_DK)
        a_same = jnp.sum(p_same[:, sl], axis=-1, keepdims=True)
        a_adj = jnp.where(is_query, jnp.sum(p_adj[:, sl], axis=-1, keepdims=True), 0.0)
        near.append(a_same * v[:, sl] + a_adj * v_adj[:, sl])
    pairs = [slice(p * 2 * A_DK, (p + 1) * 2 * A_DK) for p in range(A_HEADS // 2)]
    acc = None
    for li in range(1, len(LEVELS)):
        e = level_decay(li).reshape(CHUNK, A_WIDTH).astype(BF16)
        qs = qb * e
        ks = kb * e
        sc = [_dot_nt(qs[:, p], _pair_blockdiag(ks[:, p])) * m_ref[d, li - 1] for p in pairs]
        acc = sc if acc is None else [a + s for a, s in zip(acc, sc)]
    intra = [_dot(a.astype(BF16), _pair_blockdiag(vb[:, p])) for a, p in zip(acc, pairs)]
    return jnp.concatenate(intra, axis=1) + jnp.concatenate(inter, axis=1) + jnp.concatenate(near, axis=1)


def _hgrn_body(layer, tbl_ref, qf_ref, zf_ref, vf_ref, qb_ref, zb_ref, vb_ref, lbl_ref, m_ref, sgn_ref,
               of_ref, ob_ref, s_ref):
    step = pl.program_id(0)

    @pl.when(tbl_ref[2, step] == 1)
    def _():
        s_ref[...] = jnp.zeros_like(s_ref)

    n_even = lbl_ref.shape[0]
    logits = [lbl_ref[i] for i in range(n_even)]
    mx = functools.reduce(jnp.maximum, logits)
    ex = [jnp.exp(l - mx) for l in logits]
    den = functools.reduce(lambda x, y: x + y, ex)
    ps = [e / den for e in ex]
    lbs = functools.reduce(lambda x, y: x + y, ps[:layer + 1]) - ps[0]

    nc = HGRN_BLOCK // CHUNK
    refs = ((qf_ref, zf_ref, vf_ref, of_ref), (qb_ref, zb_ref, vb_ref, ob_ref))
    tris = (_tri(False), _tri(True))

    def chunk(c, carry):
        prep = []
        for d, rev in enumerate((False, True)):
            q_ref, z_ref, v_ref, _ = refs[d]
            r0 = pl.multiple_of(((nc - 1 - c) if rev else c) * CHUNK, CHUNK)
            rows = pl.ds(r0, CHUNK)
            qa = q_ref[rows, :]
            z = z_ref[rows, :]
            lb = lbs[d:d + 1, :]
            q = qa * _sigmoid(qa)
            f = lb + (1.0 - lb) * _sigmoid(z)
            k = 1.0 - f
            lf = jnp.log(f)
            hi = lf.astype(BF16)
            r1 = lf - hi.astype(F32)
            mid = r1.astype(BF16)
            lo = (r1 - mid.astype(F32)).astype(BF16)
            cs = _dot(tris[d], jnp.concatenate([hi, mid, lo], axis=1))
            b = (cs[:, :A_WIDTH] + cs[:, A_WIDTH:2 * A_WIDTH] + cs[:, 2 * A_WIDTH:]) * LOG2E
            prep.append((rows, q, k, v_ref[rows, :], b))
        for d, rev in enumerate((False, True)):
            rows, q, k, v, b = prep[d]
            refs[d][3][rows, :] = _hgrn_chunk(q, k, v, b, s_ref, d, rev, m_ref, sgn_ref)
        return carry

    lax.fori_loop(0, nc, chunk, 0, unroll=True)


def _hgrn_table(segs):
    rows = []
    for off, batch, seq in segs:
        nblk = seq // HGRN_BLOCK
        for bi in range(batch):
            base = (off + bi * seq) // HGRN_BLOCK
            for i in range(nblk):
                rows.append((base + i, base + nblk - 1 - i, int(i == 0)))
    return np.asarray(rows, np.int32).T.copy()


def _hgrn(pa, lb_logits, layer, segs):
    t = pa.shape[0]
    tbl = _hgrn_table(segs)
    steps = tbl.shape[1]
    masks = _hgrn_masks()
    signs = _hgrn_signs()

    def spec(col, which):
        return pl.BlockSpec((HGRN_BLOCK, A_WIDTH), lambda i, tb: (tb[which, i], col))

    grid_spec = pltpu.PrefetchScalarGridSpec(
        num_scalar_prefetch=1,
        grid=(steps,),
        in_specs=[spec(0, 0), spec(1, 0), spec(3, 0), spec(0, 1), spec(2, 1), spec(3, 1),
                  pl.BlockSpec(lb_logits.shape, lambda i, tb: (0, 0, 0)),
                  pl.BlockSpec(masks.shape, lambda i, tb: (0, 0, 0, 0)),
                  pl.BlockSpec(signs.shape, lambda i, tb: (0, 0, 0, 0))],
        out_specs=[pl.BlockSpec((HGRN_BLOCK, A_WIDTH), lambda i, tb: (tb[0, i], 0)),
                   pl.BlockSpec((HGRN_BLOCK, A_WIDTH), lambda i, tb: (tb[1, i], 0))],
        scratch_shapes=[pltpu.VMEM((2, A_HEADS, A_DK, A_DK), F32)],
    )
    return pl.pallas_call(
        functools.partial(_hgrn_body, layer),
        grid_spec=grid_spec,
        out_shape=[jax.ShapeDtypeStruct((t, A_WIDTH), F32),
                   jax.ShapeDtypeStruct((t, A_WIDTH), F32)],
        compiler_params=pltpu.CompilerParams(dimension_semantics=("arbitrary",),
                                             vmem_limit_bytes=VMEM_LIMIT),
        name="hgrn",
    )(jnp.asarray(tbl), pa, pa, pa, pa, pa, pa, lb_logits, jnp.asarray(masks), jnp.asarray(signs))


def _na_build_bias(rpb_ref, bias_ref):
    n_ro = 2 * NA_KR - 1
    n_co = 2 * NA_KC - 1
    qc = lax.broadcasted_iota(jnp.int32, (GRID_W, GRID_W), 0)
    kc = lax.broadcasted_iota(jnp.int32, (GRID_W, GRID_W), 1)
    co = jnp.clip(kc - qc, -(NA_KC - 1), NA_KC - 1) + NA_KC - 1
    ws = jnp.clip(qc - NA_KC // 2, 0, GRID_W - NA_KC)
    valid = (kc >= ws) & (kc < ws + NA_KC)

    def build(hr, carry):
        h = hr // n_ro
        ro = hr % n_ro
        t = jnp.full((GRID_W, GRID_W), NEG, F32)
        for j in range(n_co):
            t = jnp.where(co == j, rpb_ref[hr * n_co + j], t)
        t = jnp.where(valid, t, NEG)
        for i in range(NA_KR):
            dd = ro - i

            @pl.when((dd >= 0) & (dd < NA_KR))
            def _():
                bias_ref[h, dd, :, i * GRID_W:(i + 1) * GRID_W] = t
        return carry

    lax.fori_loop(0, NA_HEADS * n_ro, build, 0)


def _na_body(tbl_ref, rpb_ref, q_ref, k_ref, v_ref, o_ref, bias_ref):
    step = pl.program_id(0)

    @pl.when(step == 0)
    def _():
        _na_build_bias(rpb_ref, bias_ref)

    r0 = tbl_ref[2, step]
    w0 = tbl_ref[3, step]
    n_rows = tbl_ref[4, step]
    head = lax.broadcasted_iota(jnp.int32, (GRID_W, NA_GROUP * NA_HD), 1) // NA_HD
    nkeys = NA_KR * GRID_W
    scale = jnp.asarray(NA_HD ** -0.5, BF16)

    groups = [slice(g * NA_GROUP * NA_HD, (g + 1) * NA_GROUP * NA_HD) for g in range(NA_HEADS // NA_GROUP)]

    def rows(it, carry):
        units = []
        for u in range(NA_UNROLL):
            rr = it * NA_UNROLL + u
            r = r0 + rr
            row_start = jnp.clip(r - NA_KR // 2, 0, n_rows - NA_KR)
            dd = row_start - r + (NA_KR - 1)
            keys = pl.ds(pl.multiple_of((row_start - w0) * GRID_W, GRID_W), nkeys)
            qrows = pl.ds(pl.multiple_of(rr * GRID_W, GRID_W), GRID_W)
            units += [(dd, keys, qrows, g, lanes) for g, lanes in enumerate(groups)]
        scores = []
        for dd, keys, qrows, g, lanes in units:
            qp = q_ref[qrows, lanes] * scale
            zero = jnp.zeros_like(qp)
            qs = jnp.concatenate([jnp.where(head == h, qp, zero) for h in range(NA_GROUP)], axis=0)
            scores.append(_dot_nt(qs, k_ref[keys, lanes]))
        probs = []
        for (dd, keys, qrows, g, lanes), s in zip(units, scores):
            s = s + jnp.concatenate([bias_ref[NA_GROUP * g + h, dd] for h in range(NA_GROUP)], axis=0)
            m = jnp.max(s, axis=-1, keepdims=True)
            p = jnp.exp(s - m)
            probs.append((p.astype(BF16), jnp.sum(p, axis=-1, keepdims=True)))
        for (dd, keys, qrows, g, lanes), (p, l) in zip(units, probs):
            pv = _dot(p, v_ref[keys, lanes]) / l
            o = pv[:GRID_W]
            for h in range(1, NA_GROUP):
                o = jnp.where(head == h, pv[h * GRID_W:(h + 1) * GRID_W], o)
            o_ref[qrows, lanes] = o.astype(o_ref.dtype)
        return carry

    lax.fori_loop(0, NA_ROWS // NA_UNROLL, rows, 0)


def _na_table(segs):
    blk = NA_ROWS * GRID_W
    rows = []
    for off, batch, seq in segs:
        n_rows = seq // GRID_W
        assert n_rows >= NA_WIN
        for bi in range(batch):
            base = off + bi * seq
            for i in range(seq // blk):
                r0 = i * NA_ROWS
                w0 = min(max(r0 - NA_ROWS, 0), n_rows - NA_WIN)
                rows.append((base // blk + i, base // GRID_W + w0, r0, w0, n_rows))
    return np.asarray(rows, np.int32).T.copy()


def _na(pn, rpb, segs):
    t = pn.shape[0]
    tbl = _na_table(segs)
    steps = tbl.shape[1]
    blk = NA_ROWS * GRID_W

    def window(col):
        return pl.BlockSpec((pl.Element(NA_WIN * GRID_W), pl.Element(NA_WIDTH)),
                            lambda i, tb: (tb[1, i] * GRID_W, col * NA_WIDTH))

    grid_spec = pltpu.PrefetchScalarGridSpec(
        num_scalar_prefetch=1,
        grid=(steps,),
        in_specs=[pl.BlockSpec(memory_space=pltpu.SMEM),
                  pl.BlockSpec((blk, NA_WIDTH), lambda i, tb: (tb[0, i], 0)),
                  window(1), window(2)],
        out_specs=pl.BlockSpec((blk, NA_WIDTH), lambda i, tb: (tb[0, i], 0)),
        scratch_shapes=[pltpu.VMEM((NA_HEADS, NA_KR, GRID_W, NA_KR * GRID_W), F32)],
    )
    return pl.pallas_call(
        _na_body,
        grid_spec=grid_spec,
        out_shape=jax.ShapeDtypeStruct((t, NA_WIDTH), BF16),
        compiler_params=pltpu.CompilerParams(dimension_semantics=("arbitrary",),
                                             vmem_limit_bytes=VMEM_LIMIT),
        name="natten",
    )(jnp.asarray(tbl), rpb.astype(F32).reshape(-1), pn, pn, pn)


def _out_specs_shapes(out_tokens):
    starts = [0]
    for t in out_tokens:
        starts.append(starts[-1] + t // TM)
    specs = []
    for k in range(len(out_tokens)):
        s, n = starts[k], starts[k + 1] - starts[k]
        specs.append(pl.BlockSpec((TM, D_MODEL), lambda i, *_, s=s, n=n: (jnp.clip(i - s, 0, n - 1), 0)))
    shapes = [jax.ShapeDtypeStruct((t, D_MODEL), F32) for t in out_tokens]
    return tuple(starts), specs, shapes


def _store_parts(starts, out_refs, accs):
    def store(o_ref):
        for sl, acc in zip(SLABS, accs):
            o_ref[sl, :] = acc

    if len(out_refs) == 1:
        store(out_refs[0])
        return
    step = pl.program_id(0)
    for k, o_ref in enumerate(out_refs):
        @pl.when((step >= starts[k]) & (step < starts[k + 1]))
        def _(o_ref=o_ref):
            store(o_ref)


def _ffn_tail(x1s, gf_ref, wi_ref, wo_ref, gfin_ref, final):
    hs = [_rms(x1, gf_ref[...]).astype(BF16) for x1 in x1s]
    accs = list(x1s)
    for c in range(0, D_FF, FF_CHUNK):
        gus = [(_dot(h, wi_ref[:, c:c + FF_CHUNK]), _dot(h, wi_ref[:, D_FF + c:D_FF + c + FF_CHUNK])) for h in hs]
        acts = [(g * _sigmoid(g) * u).astype(BF16) for g, u in gus]
        accs = [acc + _dot(a, wo_ref[c:c + FF_CHUNK, :]) for acc, a in zip(accs, acts)]
    if final:
        accs = [_rms(acc, gfin_ref[...]) for acc in accs]
    return accs


def _post_even_body(final, starts, out_starts, *refs):
    n = len(starts) - 1
    of_ref, ob_ref, ga_ref, n_ref, gain_ref, wm_ref, gf_ref, wi_ref, wo_ref, gfin_ref = refs[n:n + 10]
    x = _select_part(starts, refs[:n])
    x1s = []
    for sl in SLABS:
        o = of_ref[sl, :] + ob_ref[sl, :]
        parts = []
        for h in range(A_HEADS):
            oh = o[:, h * A_DK:(h + 1) * A_DK]
            parts.append(oh * lax.rsqrt(jnp.mean(oh * oh, axis=-1, keepdims=True) + EPS))
        on = jnp.concatenate(parts, axis=1) * gain_ref[...]
        g = ga_ref[sl, :]
        oa = (on * (g * _sigmoid(g))).astype(BF16)
        mix = _dot(oa, wm_ref[0:A_WIDTH, :]) + _dot(n_ref[sl, :], wm_ref[A_WIDTH:A_WIDTH + NA_WIDTH, :])
        x1s.append(x[sl, :] + mix)
    accs = _ffn_tail(x1s, gf_ref, wi_ref, wo_ref, gfin_ref, final)
    _store_parts(out_starts, refs[n + 10:], accs)


def _post_even(parts, o_f, o_b, pa, o_n, gain, w_mix, g_ffn, w_ffn_in, w_ffn_out, g_final, final, out_tokens):
    starts = _tile_starts(parts)
    out_starts, out_specs, out_shapes = _out_specs_shapes(out_tokens)
    tok = lambda w, col=0: pl.BlockSpec((TM, w), lambda i: (i, col))
    return pl.pallas_call(
        functools.partial(_post_even_body, final, starts, out_starts),
        grid=(starts[-1],),
        in_specs=_part_specs(parts) + [tok(A_WIDTH), tok(A_WIDTH), tok(A_WIDTH, 4), tok(NA_WIDTH),
                                       _resident((1, A_WIDTH)), _resident((A_WIDTH + NA_WIDTH, D_MODEL)),
                                       _resident((1, D_MODEL)), _resident((D_MODEL, 2 * D_FF)),
                                       _resident((D_FF, D_MODEL)), _resident((1, D_MODEL))],
        out_specs=out_specs,
        out_shape=out_shapes,
        compiler_params=pltpu.CompilerParams(dimension_semantics=("arbitrary",),
                                             vmem_limit_bytes=VMEM_LIMIT),
        name="post_even",
    )(*parts, o_f, o_b, pa, o_n, gain, w_mix, g_ffn, w_ffn_in, w_ffn_out, g_final)


def _odd_body(final, out_starts, tbl_ref, x_ref, xp_ref, xn_ref, gm_ref, wc_ref, cw_ref, wco_ref, gf_ref,
              wi_ref, wo_ref, gfin_ref, *out_refs):
    step = pl.program_id(0)
    x = x_ref[...]
    gm = gm_ref[...]
    hs = [_rms(x[sl, :], gm).astype(BF16) for sl in SLABS]
    zs = [_dot(h, wc_ref[:, D_MODEL:2 * D_MODEL]) * _dot(h, wc_ref[:, 2 * D_MODEL:3 * D_MODEL]) for h in hs]
    hh = _rms(jnp.concatenate([xp_ref[...], xn_ref[...]], axis=0), gm).astype(BF16)
    zh = _dot(hh, wc_ref[:, D_MODEL:2 * D_MODEL]) * _dot(hh, wc_ref[:, 2 * D_MODEL:3 * D_MODEL])
    bgs = [_dot(h, wc_ref[:, 0:D_MODEL]) for h in hs]
    z_prev = jnp.where(tbl_ref[2, step] == 1, 0.0, zh[HALO - 1:HALO, :])
    z_next = jnp.where(tbl_ref[3, step] == 1, 0.0, zh[HALO:HALO + 1, :])
    rows = TM // len(SLABS)
    row = lax.broadcasted_iota(jnp.int32, (rows, 1), 0)
    x1s = []
    for k, (sl, z, bg) in enumerate(zip(SLABS, zs, bgs)):
        above = z_prev if k == 0 else zs[k - 1][rows - 1:rows, :]
        below = z_next if k == len(SLABS) - 1 else zs[k + 1][0:1, :]
        z_dn = jnp.where(row == 0, above, pltpu.roll(z, 1, 0))
        z_up = jnp.where(row == rows - 1, below, pltpu.roll(z, rows - 1, 0))
        conv = z_dn * cw_ref[0:1, :] + z * cw_ref[1:2, :] + z_up * cw_ref[2:3, :]
        x1s.append(x[sl, :] + _dot((bg * conv).astype(BF16), wco_ref[...]))
    accs = _ffn_tail(x1s, gf_ref, wi_ref, wo_ref, gfin_ref, final)
    _store_parts(out_starts, out_refs, accs)


def _odd_table(segs, t):
    rows = []
    starts = set()
    ends = set()
    for off, batch, seq in segs:
        for bi in range(batch):
            starts.add(off + bi * seq)
            ends.add(off + (bi + 1) * seq)
    for i in range(t // TM):
        t0 = i * TM
        first = int(t0 in starts)
        last = int(t0 + TM in ends)
        rows.append((max(t0 // HALO - 1, 0), min((t0 + TM) // HALO, t // HALO - 1), first, last))
    return np.asarray(rows, np.int32).T.copy()


def _odd(x, g_mix, w_conv_in, conv_w, w_conv_out, g_ffn, w_ffn_in, w_ffn_out, g_final, final, segs, out_tokens):
    t = x.shape[0]
    tbl = _odd_table(segs, t)
    out_starts, out_specs, out_shapes = _out_specs_shapes(out_tokens)
    res = lambda shape: pl.BlockSpec(shape, lambda i, tb: (0,) * len(shape), pipeline_mode=pl.Buffered(1))
    grid_spec = pltpu.PrefetchScalarGridSpec(
        num_scalar_prefetch=1,
        grid=(t // TM,),
        in_specs=[pl.BlockSpec((TM, D_MODEL), lambda i, tb: (i, 0)),
                  pl.BlockSpec((HALO, D_MODEL), lambda i, tb: (tb[0, i], 0)),
                  pl.BlockSpec((HALO, D_MODEL), lambda i, tb: (tb[1, i], 0)),
                  res((1, D_MODEL)), res((D_MODEL, 3 * D_MODEL)), res((CONV_W, D_MODEL)),
                  res((D_MODEL, D_MODEL)), res((1, D_MODEL)), res((D_MODEL, 2 * D_FF)),
                  res((D_FF, D_MODEL)), res((1, D_MODEL))],
        out_specs=out_specs,
    )
    return pl.pallas_call(
        functools.partial(_odd_body, final, out_starts),
        grid_spec=grid_spec,
        out_shape=out_shapes,
        compiler_params=pltpu.CompilerParams(dimension_semantics=("arbitrary",),
                                             vmem_limit_bytes=VMEM_LIMIT),
        name="odd_layer",
    )(jnp.asarray(tbl), x, x, x, g_mix, w_conv_in, conv_w, w_conv_out, g_ffn, w_ffn_in, w_ffn_out, g_final)


def _trunk(xs, norm_mix, norm_ffn, norm_final, even_w_in, even_w_out, hgrn_lb_logits, hgrn_norm,
           na_rpb, conv_w_in, conv_w, conv_w_out, ffn_w_in, ffn_w_out):
    depth = norm_mix.shape[0]
    segs = []
    off = 0
    for a in xs:
        segs.append((off, a.shape[0], a.shape[1]))
        off += a.shape[0] * a.shape[1]
    parts = [a.reshape(-1, D_MODEL) for a in xs]
    part_tokens = [p.shape[0] for p in parts]
    total = off
    row = lambda v: v.reshape(1, -1).astype(F32)
    g_final = row(norm_final)
    for l in range(depth):
        final = l == depth - 1
        out_tokens = part_tokens if final else [total]
        w_fi = ffn_w_in[l].astype(BF16)
        w_fo = ffn_w_out[l].astype(BF16)
        if l % 2 == 0:
            e = l // 2
            pa, pn = _in_even(parts, row(norm_mix[l]), even_w_in[e].astype(BF16))
            o_f, o_b = _hgrn(pa, hgrn_lb_logits.astype(F32), e, segs)
            o_n = _na(pn, na_rpb[e], segs)
            outs = _post_even(parts, o_f, o_b, pa, o_n, row(hgrn_norm[e]), even_w_out[e].astype(BF16),
                              row(norm_ffn[l]), w_fi, w_fo, g_final, final, out_tokens)
        else:
            o = l // 2
            x = parts[0] if len(parts) == 1 else jnp.concatenate(parts, axis=0)
            outs = _odd(x, row(norm_mix[l]), conv_w_in[o].astype(BF16), conv_w[o].astype(F32),
                        conv_w_out[o].astype(BF16), row(norm_ffn[l]), w_fi, w_fo, g_final, final, segs,
                        out_tokens)
        parts = list(outs)
    return tuple(p.reshape(a.shape) for p, a in zip(parts, xs))


def kernel(x_prompt, x_sample, norm_mix, norm_ffn, norm_final, even_w_in, even_w_out, hgrn_lb_logits,
           hgrn_norm, na_rpb, conv_w_in, conv_w, conv_w_out, ffn_w_in, ffn_w_out):
    return _trunk([x_prompt, x_sample], norm_mix, norm_ffn, norm_final, even_w_in, even_w_out,
                  hgrn_lb_logits, hgrn_norm, na_rpb, conv_w_in, conv_w, conv_w_out, ffn_w_in, ffn_w_out)
```

```python
import functools

import numpy as np
import jax
import jax.numpy as jnp
from jax import lax
from jax.experimental import pallas as pl
from jax.experimental.pallas import tpu as pltpu

F32 = jnp.float32
BF16 = jnp.bfloat16

SUBLANES = 8
D_MODEL = 1024
EPS = 1e-6
NEG = -1e30
LOG2E = 1.4426950408889634
GRID_W = 64
A_WIDTH = 512
A_HEADS = 4
A_DK = 128
CHUNK = 64
LEVELS = (2, 4, 8, 16, 32, 64)
NA_WIDTH = 512
NA_HEADS = 8
NA_HD = 64
NA_KR = 8
NA_KC = 16
NA_GROUP = 2
NA_UNROLL = 8
CONV_W = 3
D_FF = 2816
FF_CHUNK = 256
PA_W = 5 * A_WIDTH
PN_W = 3 * NA_WIDTH
EVEN_IN = PA_W + PN_W

TM = 512
SLABS = (slice(0, TM // 2), slice(TM // 2, TM))
HGRN_BLOCK = 512
NA_ROWS = 8
NA_WIN = 3 * NA_ROWS
HALO = 8
VMEM_LIMIT = 56 * 1024 * 1024

NT_DIMS = (((1,), (1,)), ((), ()))
TN_DIMS = (((0,), (0,)), ((), ()))


def _resident(shape):
    nd = len(shape)
    return pl.BlockSpec(shape, lambda *_: (0,) * nd, pipeline_mode=pl.Buffered(1))


def _rms(x, g):
    return x * lax.rsqrt(jnp.mean(x * x, axis=-1, keepdims=True) + EPS) * g


def _sigmoid(x):
    return jax.nn.sigmoid(x)


def _dot(a, b):
    return jnp.dot(a, b, preferred_element_type=F32)


def _dot_nt(a, b):
    return lax.dot_general(a, b, NT_DIMS, preferred_element_type=F32)


def _tile_starts(parts):
    starts = [0]
    for a in parts:
        starts.append(starts[-1] + a.shape[0] // TM)
    return tuple(starts)


def _part_specs(parts):
    starts = _tile_starts(parts)
    return [pl.BlockSpec((TM, D_MODEL),
                         lambda i, *_, s=starts[k], n=starts[k + 1] - starts[k]: (jnp.clip(i - s, 0, n - 1), 0))
            for k in range(len(parts))]


def _select_part(starts, refs):
    step = pl.program_id(0)
    x = refs[-1][...]
    for k in range(len(refs) - 2, -1, -1):
        x = jnp.where(step < starts[k + 1], refs[k][...], x)
    return x


def _in_even_body(starts, *refs):
    n = len(starts) - 1
    g_ref, w_ref, pa_ref, pn_ref = refs[n:]
    h = _rms(_select_part(starts, refs[:n]), g_ref[...]).astype(BF16)
    for c in range(0, PA_W, 512):
        pa_ref[:, c:c + 512] = _dot(h, w_ref[:, c:c + 512])
    for c in range(0, PN_W, 512):
        pn_ref[:, c:c + 512] = _dot(h, w_ref[:, PA_W + c:PA_W + c + 512]).astype(BF16)


def _in_even(parts, g, w):
    starts = _tile_starts(parts)
    t = starts[-1] * TM
    return pl.pallas_call(
        functools.partial(_in_even_body, starts),
        grid=(starts[-1],),
        in_specs=_part_specs(parts) + [_resident((1, D_MODEL)), _resident((D_MODEL, EVEN_IN))],
        out_specs=[pl.BlockSpec((TM, PA_W), lambda i: (i, 0)),
                   pl.BlockSpec((TM, PN_W), lambda i: (i, 0))],
        out_shape=[jax.ShapeDtypeStruct((t, PA_W), F32),
                   jax.ShapeDtypeStruct((t, PN_W), BF16)],
        compiler_params=pltpu.CompilerParams(dimension_semantics=("arbitrary",),
                                             vmem_limit_bytes=VMEM_LIMIT),
        name="in_even",
    )(*parts, g, w)


def _hgrn_masks():
    t = np.arange(CHUNK)[:, None]
    s = np.arange(CHUNK)[None, :]
    out = np.zeros((2, len(LEVELS) - 1, CHUNK, CHUNK), np.float32)
    for d, rev in enumerate((False, True)):
        for li, L in enumerate(LEVELS[1:]):
            half = L // 2
            same = (t // L) == (s // L)
            if rev:
                m = same & ((t % L) < half) & ((s % L) >= half)
            else:
                m = same & ((t % L) >= half) & ((s % L) < half)
            out[d, li] = m
    return np.tile(out, (1, 1, 1, 2))


def _hgrn_signs():
    t = np.arange(CHUNK)
    out = np.zeros((2, len(LEVELS), CHUNK), np.float32)
    for d, rev in enumerate((False, True)):
        for li, L in enumerate(LEVELS):
            second = (t % L) >= L // 2
            out[d, li] = np.where(second != rev, 1.0, -1.0)
    return np.ascontiguousarray(np.broadcast_to(out[..., None], out.shape + (A_DK,)))


def _tri(rev):
    t = lax.broadcasted_iota(jnp.int32, (CHUNK, CHUNK), 0)
    s = lax.broadcasted_iota(jnp.int32, (CHUNK, CHUNK), 1)
    return jnp.where((s >= t) if rev else (s <= t), 1.0, 0.0).astype(BF16)


def _level_ref(b3, L, rev):
    g, sub, w = b3.shape
    half = L // 2
    if L >= 2 * SUBLANES:
        n = L // SUBLANES
        pieces = []
        for blk in range(CHUNK // L):
            r = blk * L + (half if rev else half - 1)
            pieces.append(jnp.broadcast_to(b3[r // SUBLANES:r // SUBLANES + 1, r % SUBLANES:r % SUBLANES + 1, :],
                                           (n, sub, w)))
        return pieces[0] if len(pieces) == 1 else jnp.concatenate(pieces, axis=0)
    srow = lax.broadcasted_iota(jnp.int32, b3.shape, 1)
    if L == SUBLANES:
        r = half if rev else half - 1
        return jnp.broadcast_to(b3[:, r:r + 1, :], b3.shape)
    if L == 4:
        r = half if rev else half - 1
        lo = jnp.broadcast_to(b3[:, r:r + 1, :], b3.shape)
        hi = jnp.broadcast_to(b3[:, r + 4:r + 5, :], b3.shape)
        return jnp.where(srow < 4, lo, hi)
    odd = (srow % 2) == 1
    if rev:
        return jnp.where(odd, b3, pltpu.roll(b3, SUBLANES - 1, 1))
    return jnp.where(odd, pltpu.roll(b3, 1, 1), b3)


def _pair_blockdiag(x):
    zero = jnp.zeros((CHUNK, A_DK), x.dtype)
    return jnp.concatenate([jnp.concatenate([x[:, :A_DK], zero], axis=1),
                            jnp.concatenate([zero, x[:, A_DK:]], axis=1)], axis=0)


def _hgrn_chunk(q, k, v, b, s_ref, d, rev, m_ref, sgn_ref):
    tot = b[0:1, :] if rev else b[CHUNK - 1:CHUNK, :]
    qb = q.astype(BF16)
    kb = k.astype(BF16)
    q0 = qb * jnp.exp2(b).astype(BF16)
    k2 = kb * jnp.exp2(tot - b).astype(BF16)
    vb = v.astype(BF16)
    dec = jnp.exp2(tot)
    inter = []
    for h in range(A_HEADS):
        sl = slice(h * A_DK, (h + 1) * A_DK)
        s_t = s_ref[d, h]
        inter.append(_dot_nt(q0[:, sl], s_t.astype(BF16)))
        ds = lax.dot_general(vb[:, sl], k2[:, sl], TN_DIMS, preferred_element_type=F32)
        s_ref[d, h] = dec[:, sl] * s_t + ds
    b3 = b.reshape(CHUNK // SUBLANES, SUBLANES, A_WIDTH)

    def level_decay(li):
        sgn = jnp.concatenate([sgn_ref[d, li]] * A_HEADS, axis=1).reshape(b3.shape)
        return jnp.exp2((b3 - _level_ref(b3, LEVELS[li], rev)) * sgn)

    shift = SUBLANES - 1 if rev else 1
    k_adj = pltpu.roll(k.reshape(b3.shape), shift, 1).reshape(CHUNK, A_WIDTH)
    v_adj = pltpu.roll(v.reshape(b3.shape), shift, 1).reshape(CHUNK, A_WIDTH)
    p_same = q * k
    p_adj = q * level_decay(0).reshape(CHUNK, A_WIDTH) * k_adj
    row = lax.broadcasted_iota(jnp.int32, (CHUNK, 1), 0)
    is_query = (row % 2 == 0) if rev else (row % 2 == 1)
    near = []
    for h in range(A_HEADS):
        sl = slice(h * A_DK, (h + 1) * A_DK)
        a_same = jnp.sum(p_same[:, sl], axis=-1, keepdims=True)
        a_adj = jnp.where(is_query, jnp.sum(p_adj[:, sl], axis=-1, keepdims=True), 0.0)
        near.append(a_same * v[:, sl] + a_adj * v_adj[:, sl])
    pairs = [slice(p * 2 * A_DK, (p + 1) * 2 * A_DK) for p in range(A_HEADS // 2)]
    acc = None
    for li in range(1, len(LEVELS)):
        e = level_decay(li).reshape(CHUNK, A_WIDTH).astype(BF16)
        qs = qb * e
        ks = kb * e
        sc = [_dot_nt(qs[:, p], _pair_blockdiag(ks[:, p])) * m_ref[d, li - 1] for p in pairs]
        acc = sc if acc is None else [a + s for a, s in zip(acc, sc)]
    intra = [_dot(a.astype(BF16), _pair_blockdiag(vb[:, p])) for a, p in zip(acc, pairs)]
    return jnp.concatenate(intra, axis=1) + jnp.concatenate(inter, axis=1) + jnp.concatenate(near, axis=1)


def _hgrn_body(layer, tbl_ref, qf_ref, zf_ref, vf_ref, qb_ref, zb_ref, vb_ref, lbl_ref, m_ref, sgn_ref,
               of_ref, ob_ref, s_ref):
    step = pl.program_id(0)

    @pl.when(tbl_ref[2, step] == 1)
    def _():
        s_ref[...] = jnp.zeros_like(s_ref)

    n_even = lbl_ref.shape[0]
    logits = [lbl_ref[i] for i in range(n_even)]
    mx = functools.reduce(jnp.maximum, logits)
    ex = [jnp.exp(l - mx) for l in logits]
    den = functools.reduce(lambda x, y: x + y, ex)
    ps = [e / den for e in ex]
    lbs = functools.reduce(lambda x, y: x + y, ps[:layer + 1]) - ps[0]

    nc = HGRN_BLOCK // CHUNK
    refs = ((qf_ref, zf_ref, vf_ref, of_ref), (qb_ref, zb_ref, vb_ref, ob_ref))
    tris = (_tri(False), _tri(True))

    def chunk(c, carry):
        prep = []
        for d, rev in enumerate((False, True)):
            q_ref, z_ref, v_ref, _ = refs[d]
            r0 = pl.multiple_of(((nc - 1 - c) if rev else c) * CHUNK, CHUNK)
            rows = pl.ds(r0, CHUNK)
            qa = q_ref[rows, :]
            z = z_ref[rows, :]
            lb = lbs[d:d + 1, :]
            q = qa * _sigmoid(qa)
            f = lb + (1.0 - lb) * _sigmoid(z)
            k = 1.0 - f
            lf = jnp.log(f)
            hi = lf.astype(BF16)
            r1 = lf - hi.astype(F32)
            mid = r1.astype(BF16)
            lo = (r1 - mid.astype(F32)).astype(BF16)
            cs = _dot(tris[d], jnp.concatenate([hi, mid, lo], axis=1))
            b = (cs[:, :A_WIDTH] + cs[:, A_WIDTH:2 * A_WIDTH] + cs[:, 2 * A_WIDTH:]) * LOG2E
            prep.append((rows, q, k, v_ref[rows, :], b))
        for d, rev in enumerate((False, True)):
            rows, q, k, v, b = prep[d]
            refs[d][3][rows, :] = _hgrn_chunk(q, k, v, b, s_ref, d, rev, m_ref, sgn_ref)
        return carry

    lax.fori_loop(0, nc, chunk, 0, unroll=True)


def _hgrn_table(segs):
    rows = []
    for off, batch, seq in segs:
        nblk = seq // HGRN_BLOCK
        for bi in range(batch):
            base = (off + bi * seq) // HGRN_BLOCK
            for i in range(nblk):
                rows.append((base + i, base + nblk - 1 - i, int(i == 0)))
    return np.asarray(rows, np.int32).T.copy()


def _hgrn(pa, lb_logits, layer, segs):
    t = pa.shape[0]
    tbl = _hgrn_table(segs)
    steps = tbl.shape[1]
    masks = _hgrn_masks()
    signs = _hgrn_signs()

    def spec(col, which):
        return pl.BlockSpec((HGRN_BLOCK, A_WIDTH), lambda i, tb: (tb[which, i], col))

    grid_spec = pltpu.PrefetchScalarGridSpec(
        num_scalar_prefetch=1,
        grid=(steps,),
        in_specs=[spec(0, 0), spec(1, 0), spec(3, 0), spec(0, 1), spec(2, 1), spec(3, 1),
                  pl.BlockSpec(lb_logits.shape, lambda i, tb: (0, 0, 0)),
                  pl.BlockSpec(masks.shape, lambda i, tb: (0, 0, 0, 0)),
                  pl.BlockSpec(signs.shape, lambda i, tb: (0, 0, 0, 0))],
        out_specs=[pl.BlockSpec((HGRN_BLOCK, A_WIDTH), lambda i, tb: (tb[0, i], 0)),
                   pl.BlockSpec((HGRN_BLOCK, A_WIDTH), lambda i, tb: (tb[1, i], 0))],
        scratch_shapes=[pltpu.VMEM((2, A_HEADS, A_DK, A_DK), F32)],
    )
    return pl.pallas_call(
        functools.partial(_hgrn_body, layer),
        grid_spec=grid_spec,
        out_shape=[jax.ShapeDtypeStruct((t, A_WIDTH), F32),
                   jax.ShapeDtypeStruct((t, A_WIDTH), F32)],
        compiler_params=pltpu.CompilerParams(dimension_semantics=("arbitrary",),
                                             vmem_limit_bytes=VMEM_LIMIT),
        name="hgrn",
    )(jnp.asarray(tbl), pa, pa, pa, pa, pa, pa, lb_logits, jnp.asarray(masks), jnp.asarray(signs))


def _na_build_bias(rpb_ref, bias_ref):
    n_ro = 2 * NA_KR - 1
    n_co = 2 * NA_KC - 1
    qc = lax.broadcasted_iota(jnp.int32, (GRID_W, GRID_W), 0)
    kc = lax.broadcasted_iota(jnp.int32, (GRID_W, GRID_W), 1)
    co = jnp.clip(kc - qc, -(NA_KC - 1), NA_KC - 1) + NA_KC - 1
    ws = jnp.clip(qc - NA_KC // 2, 0, GRID_W - NA_KC)
    valid = (kc >= ws) & (kc < ws + NA_KC)

    def build(hr, carry):
        h = hr // n_ro
        ro = hr % n_ro
        t = jnp.full((GRID_W, GRID_W), NEG, F32)
        for j in range(n_co):
            t = jnp.where(co == j, rpb_ref[hr * n_co + j], t)
        t = jnp.where(valid, t, NEG)
        for i in range(NA_KR):
            dd = ro - i

            @pl.when((dd >= 0) & (dd < NA_KR))
            def _():
                bias_ref[h, dd, :, i * GRID_W:(i + 1) * GRID_W] = t
        return carry

    lax.fori_loop(0, NA_HEADS * n_ro, build, 0)


def _na_body(tbl_ref, rpb_ref, q_ref, k_ref, v_ref, o_ref, bias_ref):
    step = pl.program_id(0)

    @pl.when(step == 0)
    def _():
        _na_build_bias(rpb_ref, bias_ref)

    r0 = tbl_ref[2, step]
    w0 = tbl_ref[3, step]
    n_rows = tbl_ref[4, step]
    head = lax.broadcasted_iota(jnp.int32, (GRID_W, NA_GROUP * NA_HD), 1) // NA_HD
    nkeys = NA_KR * GRID_W
    scale = jnp.asarray(NA_HD ** -0.5, BF16)

    groups = [slice(g * NA_GROUP * NA_HD, (g + 1) * NA_GROUP * NA_HD) for g in range(NA_HEADS // NA_GROUP)]

    def rows(it, carry):
        units = []
        for u in range(NA_UNROLL):
            rr = it * NA_UNROLL + u
            r = r0 + rr
            row_start = jnp.clip(r - NA_KR // 2, 0, n_rows - NA_KR)
            dd = row_start - r + (NA_KR - 1)
            keys = pl.ds(pl.multiple_of((row_start - w0) * GRID_W, GRID_W), nkeys)
            qrows = pl.ds(pl.multiple_of(rr * GRID_W, GRID_W), GRID_W)
            units += [(dd, keys, qrows, g, lanes) for g, lanes in enumerate(groups)]
        scores = []
        for dd, keys, qrows, g, lanes in units:
            qp = q_ref[qrows, lanes] * scale
            zero = jnp.zeros_like(qp)
            qs = jnp.concatenate([jnp.where(head == h, qp, zero) for h in range(NA_GROUP)], axis=0)
            scores.append(_dot_nt(qs, k_ref[keys, lanes]))
        probs = []
        for (dd, keys, qrows, g, lanes), s in zip(units, scores):
            s = s + jnp.concatenate([bias_ref[NA_GROUP * g + h, dd] for h in range(NA_GROUP)], axis=0)
            m = jnp.max(s, axis=-1, keepdims=True)
            p = jnp.exp(s - m)
            probs.append((p.astype(BF16), jnp.sum(p, axis=-1, keepdims=True)))
        for (dd, keys, qrows, g, lanes), (p, l) in zip(units, probs):
            pv = _dot(p, v_ref[keys, lanes]) / l
            o = pv[:GRID_W]
            for h in range(1, NA_GROUP):
                o = jnp.where(head == h, pv[h * GRID_W:(h + 1) * GRID_W], o)
            o_ref[qrows, lanes] = o.astype(o_ref.dtype)
        return carry

    lax.fori_loop(0, NA_ROWS // NA_UNROLL, rows, 0)


def _na_table(segs):
    blk = NA_ROWS * GRID_W
    rows = []
    for off, batch, seq in segs:
        n_rows = seq // GRID_W
        assert n_rows >= NA_WIN
        for bi in range(batch):
            base = off + bi * seq
            for i in range(seq // blk):
                r0 = i * NA_ROWS
                w0 = min(max(r0 - NA_ROWS, 0), n_rows - NA_WIN)
                rows.append((base // blk + i, base // GRID_W + w0, r0, w0, n_rows))
    return np.asarray(rows, np.int32).T.copy()


def _na(pn, rpb, segs):
    t = pn.shape[0]
    tbl = _na_table(segs)
    steps = tbl.shape[1]
    blk = NA_ROWS * GRID_W

    def window(col):
        return pl.BlockSpec((pl.Element(NA_WIN * GRID_W), pl.Element(NA_WIDTH)),
                            lambda i, tb: (tb[1, i] * GRID_W, col * NA_WIDTH))

    grid_spec = pltpu.PrefetchScalarGridSpec(
        num_scalar_prefetch=1,
        grid=(steps,),
        in_specs=[pl.BlockSpec(memory_space=pltpu.SMEM),
                  pl.BlockSpec((blk, NA_WIDTH), lambda i, tb: (tb[0, i], 0)),
                  window(1), window(2)],
        out_specs=pl.BlockSpec((blk, NA_WIDTH), lambda i, tb: (tb[0, i], 0)),
        scratch_shapes=[pltpu.VMEM((NA_HEADS, NA_KR, GRID_W, NA_KR * GRID_W), F32)],
    )
    return pl.pallas_call(
        _na_body,
        grid_spec=grid_spec,
        out_shape=jax.ShapeDtypeStruct((t, NA_WIDTH), BF16),
        compiler_params=pltpu.CompilerParams(dimension_semantics=("arbitrary",),
                                             vmem_limit_bytes=VMEM_LIMIT),
        name="natten",
    )(jnp.asarray(tbl), rpb.astype(F32).reshape(-1), pn, pn, pn)


def _out_specs_shapes(out_tokens):
    starts = [0]
    for t in out_tokens:
        starts.append(starts[-1] + t // TM)
    specs = []
    for k in range(len(out_tokens)):
        s, n = starts[k], starts[k + 1] - starts[k]
        specs.append(pl.BlockSpec((TM, D_MODEL), lambda i, *_, s=s, n=n: (jnp.clip(i - s, 0, n - 1), 0)))
    shapes = [jax.ShapeDtypeStruct((t, D_MODEL), F32) for t in out_tokens]
    return tuple(starts), specs, shapes


def _store_parts(starts, out_refs, accs):
    def store(o_ref):
        for sl, acc in zip(SLABS, accs):
            o_ref[sl, :] = acc

    if len(out_refs) == 1:
        store(out_refs[0])
        return
    step = pl.program_id(0)
    for k, o_ref in enumerate(out_refs):
        @pl.when((step >= starts[k]) & (step < starts[k + 1]))
        def _(o_ref=o_ref):
            store(o_ref)


def _ffn_tail(x1s, gf_ref, wi_ref, wo_ref, gfin_ref, final):
    hs = [_rms(x1, gf_ref[...]).astype(BF16) for x1 in x1s]
    accs = list(x1s)
    for c in range(0, D_FF, FF_CHUNK):
        gus = [(_dot(h, wi_ref[:, c:c + FF_CHUNK]), _dot(h, wi_ref[:, D_FF + c:D_FF + c + FF_CHUNK])) for h in hs]
        acts = [(g * _sigmoid(g) * u).astype(BF16) for g, u in gus]
        accs = [acc + _dot(a, wo_ref[c:c + FF_CHUNK, :]) for acc, a in zip(accs, acts)]
    if final:
        accs = [_rms(acc, gfin_ref[...]) for acc in accs]
    return accs


def _post_even_body(final, starts, out_starts, *refs):
    n = len(starts) - 1
    of_ref, ob_ref, ga_ref, n_ref, gain_ref, wm_ref, gf_ref, wi_ref, wo_ref, gfin_ref = refs[n:n + 10]
    x = _select_part(starts, refs[:n])
    x1s = []
    for sl in SLABS:
        o = of_ref[sl, :] + ob_ref[sl, :]
        parts = []
        for h in range(A_HEADS):
            oh = o[:, h * A_DK:(h + 1) * A_DK]
            parts.append(oh * lax.rsqrt(jnp.mean(oh * oh, axis=-1, keepdims=True) + EPS))
        on = jnp.concatenate(parts, axis=1) * gain_ref[...]
        g = ga_ref[sl, :]
        oa = (on * (g * _sigmoid(g))).astype(BF16)
        mix = _dot(oa, wm_ref[0:A_WIDTH, :]) + _dot(n_ref[sl, :], wm_ref[A_WIDTH:A_WIDTH + NA_WIDTH, :])
        x1s.append(x[sl, :] + mix)
    accs = _ffn_tail(x1s, gf_ref, wi_ref, wo_ref, gfin_ref, final)
    _store_parts(out_starts, refs[n + 10:], accs)


def _post_even(parts, o_f, o_b, pa, o_n, gain, w_mix, g_ffn, w_ffn_in, w_ffn_out, g_final, final, out_tokens):
    starts = _tile_starts(parts)
    out_starts, out_specs, out_shapes = _out_specs_shapes(out_tokens)
    tok = lambda w, col=0: pl.BlockSpec((TM, w), lambda i: (i, col))
    return pl.pallas_call(
        functools.partial(_post_even_body, final, starts, out_starts),
        grid=(starts[-1],),
        in_specs=_part_specs(parts) + [tok(A_WIDTH), tok(A_WIDTH), tok(A_WIDTH, 4), tok(NA_WIDTH),
                                       _resident((1, A_WIDTH)), _resident((A_WIDTH + NA_WIDTH, D_MODEL)),
                                       _resident((1, D_MODEL)), _resident((D_MODEL, 2 * D_FF)),
                                       _resident((D_FF, D_MODEL)), _resident((1, D_MODEL))],
        out_specs=out_specs,
        out_shape=out_shapes,
        compiler_params=pltpu.CompilerParams(dimension_semantics=("arbitrary",),
                                             vmem_limit_bytes=VMEM_LIMIT),
        name="post_even",
    )(*parts, o_f, o_b, pa, o_n, gain, w_mix, g_ffn, w_ffn_in, w_ffn_out, g_final)


def _odd_body(final, out_starts, tbl_ref, x_ref, xp_ref, xn_ref, gm_ref, wc_ref, cw_ref, wco_ref, gf_ref,
              wi_ref, wo_ref, gfin_ref, *out_refs):
    step = pl.program_id(0)
    x = x_ref[...]
    gm = gm_ref[...]
    hs = [_rms(x[sl, :], gm).astype(BF16) for sl in SLABS]
    hh = _rms(jnp.concatenate([xp_ref[...], xn_ref[...]], axis=0), gm).astype(BF16)
    zin = [jnp.concatenate([hs[0], hh], axis=0)] + hs[1:]
    zs = [_dot(h, wc_ref[:, D_MODEL:2 * D_MODEL]) * _dot(h, wc_ref[:, 2 * D_MODEL:3 * D_MODEL]) for h in zin]
    zh = zs[0][TM // len(SLABS):, :]
    zs[0] = zs[0][:TM // len(SLABS), :]
    bgs = [_dot(h, wc_ref[:, 0:D_MODEL]) for h in hs]
    z_prev = jnp.where(tbl_ref[2, step] == 1, 0.0, zh[HALO - 1:HALO, :])
    z_next = jnp.where(tbl_ref[3, step] == 1, 0.0, zh[HALO:HALO + 1, :])
    rows = TM // len(SLABS)
    row = lax.broadcasted_iota(jnp.int32, (rows, 1), 0)
    x1s = []
    for k, (sl, z, bg) in enumerate(zip(SLABS, zs, bgs)):
        above = z_prev if k == 0 else zs[k - 1][rows - 1:rows, :]
        below = z_next if k == len(SLABS) - 1 else zs[k + 1][0:1, :]
        z_dn = jnp.where(row == 0, above, pltpu.roll(z, 1, 0))
        z_up = jnp.where(row == rows - 1, below, pltpu.roll(z, rows - 1, 0))
        conv = z_dn * cw_ref[0:1, :] + z * cw_ref[1:2, :] + z_up * cw_ref[2:3, :]
        x1s.append(x[sl, :] + _dot((bg * conv).astype(BF16), wco_ref[...]))
    accs = _ffn_tail(x1s, gf_ref, wi_ref, wo_ref, gfin_ref, final)
    _store_parts(out_starts, out_refs, accs)


def _odd_table(segs, t):
    rows = []
    starts = set()
    ends = set()
    for off, batch, seq in segs:
        for bi in range(batch):
            starts.add(off + bi * seq)
            ends.add(off + (bi + 1) * seq)
    for i in range(t // TM):
        t0 = i * TM
        first = int(t0 in starts)
        last = int(t0 + TM in ends)
        rows.append((max(t0 // HALO - 1, 0), min((t0 + TM) // HALO, t // HALO - 1), first, last))
    return np.asarray(rows, np.int32).T.copy()


def _odd(x, g_mix, w_conv_in, conv_w, w_conv_out, g_ffn, w_ffn_in, w_ffn_out, g_final, final, segs, out_tokens):
    t = x.shape[0]
    tbl = _odd_table(segs, t)
    out_starts, out_specs, out_shapes = _out_specs_shapes(out_tokens)
    res = lambda shape: pl.BlockSpec(shape, lambda i, tb: (0,) * len(shape), pipeline_mode=pl.Buffered(1))
    grid_spec = pltpu.PrefetchScalarGridSpec(
        num_scalar_prefetch=1,
        grid=(t // TM,),
        in_specs=[pl.BlockSpec((TM, D_MODEL), lambda i, tb: (i, 0)),
                  pl.BlockSpec((HALO, D_MODEL), lambda i, tb: (tb[0, i], 0)),
                  pl.BlockSpec((HALO, D_MODEL), lambda i, tb: (tb[1, i], 0)),
                  res((1, D_MODEL)), res((D_MODEL, 3 * D_MODEL)), res((CONV_W, D_MODEL)),
                  res((D_MODEL, D_MODEL)), res((1, D_MODEL)), res((D_MODEL, 2 * D_FF)),
                  res((D_FF, D_MODEL)), res((1, D_MODEL))],
        out_specs=out_specs,
    )
    return pl.pallas_call(
        functools.partial(_odd_body, final, out_starts),
        grid_spec=grid_spec,
        out_shape=out_shapes,
        compiler_params=pltpu.CompilerParams(dimension_semantics=("arbitrary",),
                                             vmem_limit_bytes=VMEM_LIMIT),
        name="odd_layer",
    )(jnp.asarray(tbl), x, x, x, g_mix, w_conv_in, conv_w, w_conv_out, g_ffn, w_ffn_in, w_ffn_out, g_final)


def _trunk(xs, norm_mix, norm_ffn, norm_final, even_w_in, even_w_out, hgrn_lb_logits, hgrn_norm,
           na_rpb, conv_w_in, conv_w, conv_w_out, ffn_w_in, ffn_w_out):
    depth = norm_mix.shape[0]
    segs = []
    off = 0
    for a in xs:
        segs.append((off, a.shape[0], a.shape[1]))
        off += a.shape[0] * a.shape[1]
    parts = [a.reshape(-1, D_MODEL) for a in xs]
    part_tokens = [p.shape[0] for p in parts]
    total = off
    row = lambda v: v.reshape(1, -1).astype(F32)
    g_final = row(norm_final)
    for l in range(depth):
        final = l == depth - 1
        out_tokens = part_tokens if final else [total]
        w_fi = ffn_w_in[l].astype(BF16)
        w_fo = ffn_w_out[l].astype(BF16)
        if l % 2 == 0:
            e = l // 2
            pa, pn = _in_even(parts, row(norm_mix[l]), even_w_in[e].astype(BF16))
            o_f, o_b = _hgrn(pa, hgrn_lb_logits.astype(F32), e, segs)
            o_n = _na(pn, na_rpb[e], segs)
            outs = _post_even(parts, o_f, o_b, pa, o_n, row(hgrn_norm[e]), even_w_out[e].astype(BF16),
                              row(norm_ffn[l]), w_fi, w_fo, g_final, final, out_tokens)
        else:
            o = l // 2
            x = parts[0] if len(parts) == 1 else jnp.concatenate(parts, axis=0)
            outs = _odd(x, row(norm_mix[l]), conv_w_in[o].astype(BF16), conv_w[o].astype(F32),
                        conv_w_out[o].astype(BF16), row(norm_ffn[l]), w_fi, w_fo, g_final, final, segs,
                        out_tokens)
        parts = list(outs)
    return tuple(p.reshape(a.shape) for p, a in zip(parts, xs))


def kernel(x_prompt, x_sample, norm_mix, norm_ffn, norm_final, even_w_in, even_w_out, hgrn_lb_logits,
           hgrn_norm, na_rpb, conv_w_in, conv_w, conv_w_out, ffn_w_in, ffn_w_out):
    return _trunk([x_prompt, x_sample], norm_mix, norm_ffn, norm_final, even_w_in, even_w_out,
                  hgrn_lb_logits, hgrn_norm, na_rpb, conv_w_in, conv_w, conv_w_out, ffn_w_in, ffn_w_out)
```

```python
import functools

import numpy as np
import jax
import jax.numpy as jnp
from jax import lax
from jax.experimental import pallas as pl
from jax.experimental.pallas import tpu as pltpu

F32 = jnp.float32
BF16 = jnp.bfloat16

SUBLANES = 8
D_MODEL = 1024
EPS = 1e-6
NEG = -1e30
LOG2E = 1.4426950408889634
GRID_W = 64
A_WIDTH = 512
A_HEADS = 4
A_DK = 128
CHUNK = 64
LEVELS = (2, 4, 8, 16, 32, 64)
NA_WIDTH = 512
NA_HEADS = 8
NA_HD = 64
NA_KR = 8
NA_KC = 16
NA_GROUP = 2
NA_UNROLL = 8
CONV_W = 3
D_FF = 2816
FF_CHUNK = 256
PA_W = 5 * A_WIDTH
PN_W = 3 * NA_WIDTH
EVEN_IN = PA_W + PN_W

TM = 512
SLABS = (slice(0, TM // 2), slice(TM // 2, TM))
HGRN_BLOCK = 1024
NA_ROWS = 8
NA_WIN = 3 * NA_ROWS
HALO = 8
VMEM_LIMIT = 56 * 1024 * 1024

NT_DIMS = (((1,), (1,)), ((), ()))
TN_DIMS = (((0,), (0,)), ((), ()))


def _resident(shape):
    nd = len(shape)
    return pl.BlockSpec(shape, lambda *_: (0,) * nd, pipeline_mode=pl.Buffered(1))


def _rms(x, g):
    return x * lax.rsqrt(jnp.mean(x * x, axis=-1, keepdims=True) + EPS) * g


def _sigmoid(x):
    return jax.nn.sigmoid(x)


def _dot(a, b):
    return jnp.dot(a, b, preferred_element_type=F32)


def _dot_nt(a, b):
    return lax.dot_general(a, b, NT_DIMS, preferred_element_type=F32)


def _tile_starts(parts):
    starts = [0]
    for a in parts:
        starts.append(starts[-1] + a.shape[0] // TM)
    return tuple(starts)


def _part_specs(parts):
    starts = _tile_starts(parts)
    return [pl.BlockSpec((TM, D_MODEL),
                         lambda i, *_, s=starts[k], n=starts[k + 1] - starts[k]: (jnp.clip(i - s, 0, n - 1), 0))
            for k in range(len(parts))]


def _select_part(starts, refs):
    step = pl.program_id(0)
    x = refs[-1][...]
    for k in range(len(refs) - 2, -1, -1):
        x = jnp.where(step < starts[k + 1], refs[k][...], x)
    return x


def _in_even_body(starts, *refs):
    n = len(starts) - 1
    g_ref, w_ref, pa_ref, pn_ref = refs[n:]
    h = _rms(_select_part(starts, refs[:n]), g_ref[...]).astype(BF16)
    for c in range(0, PA_W, 512):
        pa_ref[:, c:c + 512] = _dot(h, w_ref[:, c:c + 512])
    for c in range(0, PN_W, 512):
        pn_ref[:, c:c + 512] = _dot(h, w_ref[:, PA_W + c:PA_W + c + 512]).astype(BF16)


def _in_even(parts, g, w):
    starts = _tile_starts(parts)
    t = starts[-1] * TM
    return pl.pallas_call(
        functools.partial(_in_even_body, starts),
        grid=(starts[-1],),
        in_specs=_part_specs(parts) + [_resident((1, D_MODEL)), _resident((D_MODEL, EVEN_IN))],
        out_specs=[pl.BlockSpec((TM, PA_W), lambda i: (i, 0)),
                   pl.BlockSpec((TM, PN_W), lambda i: (i, 0))],
        out_shape=[jax.ShapeDtypeStruct((t, PA_W), F32),
                   jax.ShapeDtypeStruct((t, PN_W), BF16)],
        compiler_params=pltpu.CompilerParams(dimension_semantics=("arbitrary",),
                                             vmem_limit_bytes=VMEM_LIMIT),
        name="in_even",
    )(*parts, g, w)


def _hgrn_masks():
    t = np.arange(CHUNK)[:, None]
    s = np.arange(CHUNK)[None, :]
    out = np.zeros((2, len(LEVELS) - 1, CHUNK, CHUNK), np.float32)
    for d, rev in enumerate((False, True)):
        for li, L in enumerate(LEVELS[1:]):
            half = L // 2
            same = (t // L) == (s // L)
            if rev:
                m = same & ((t % L) < half) & ((s % L) >= half)
            else:
                m = same & ((t % L) >= half) & ((s % L) < half)
            out[d, li] = m
    return np.tile(out, (1, 1, 1, 2))


def _hgrn_signs():
    t = np.arange(CHUNK)
    out = np.zeros((2, len(LEVELS), CHUNK), np.float32)
    for d, rev in enumerate((False, True)):
        for li, L in enumerate(LEVELS):
            second = (t % L) >= L // 2
            out[d, li] = np.where(second != rev, 1.0, -1.0)
    return np.ascontiguousarray(np.broadcast_to(out[..., None], out.shape + (A_DK,)))


def _tri(rev):
    t = lax.broadcasted_iota(jnp.int32, (CHUNK, CHUNK), 0)
    s = lax.broadcasted_iota(jnp.int32, (CHUNK, CHUNK), 1)
    return jnp.where((s >= t) if rev else (s <= t), 1.0, 0.0).astype(BF16)


def _level_ref(b3, L, rev):
    g, sub, w = b3.shape
    half = L // 2
    if L >= 2 * SUBLANES:
        n = L // SUBLANES
        pieces = []
        for blk in range(CHUNK // L):
            r = blk * L + (half if rev else half - 1)
            pieces.append(jnp.broadcast_to(b3[r // SUBLANES:r // SUBLANES + 1, r % SUBLANES:r % SUBLANES + 1, :],
                                           (n, sub, w)))
        return pieces[0] if len(pieces) == 1 else jnp.concatenate(pieces, axis=0)
    srow = lax.broadcasted_iota(jnp.int32, b3.shape, 1)
    if L == SUBLANES:
        r = half if rev else half - 1
        return jnp.broadcast_to(b3[:, r:r + 1, :], b3.shape)
    if L == 4:
        r = half if rev else half - 1
        lo = jnp.broadcast_to(b3[:, r:r + 1, :], b3.shape)
        hi = jnp.broadcast_to(b3[:, r + 4:r + 5, :], b3.shape)
        return jnp.where(srow < 4, lo, hi)
    odd = (srow % 2) == 1
    if rev:
        return jnp.where(odd, b3, pltpu.roll(b3, SUBLANES - 1, 1))
    return jnp.where(odd, pltpu.roll(b3, 1, 1), b3)


def _pair_blockdiag(x):
    zero = jnp.zeros((CHUNK, A_DK), x.dtype)
    return jnp.concatenate([jnp.concatenate([x[:, :A_DK], zero], axis=1),
                            jnp.concatenate([zero, x[:, A_DK:]], axis=1)], axis=0)


def _hgrn_chunk(q, k, v, b, s_ref, d, rev, m_ref, sgn_ref):
    tot = b[0:1, :] if rev else b[CHUNK - 1:CHUNK, :]
    qb = q.astype(BF16)
    kb = k.astype(BF16)
    q0 = qb * jnp.exp2(b).astype(BF16)
    k2 = kb * jnp.exp2(tot - b).astype(BF16)
    vb = v.astype(BF16)
    dec = jnp.exp2(tot)
    inter = []
    for h in range(A_HEADS):
        sl = slice(h * A_DK, (h + 1) * A_DK)
        s_t = s_ref[d, h]
        inter.append(_dot_nt(q0[:, sl], s_t.astype(BF16)))
        ds = lax.dot_general(vb[:, sl], k2[:, sl], TN_DIMS, preferred_element_type=F32)
        s_ref[d, h] = dec[:, sl] * s_t + ds
    b3 = b.reshape(CHUNK // SUBLANES, SUBLANES, A_WIDTH)

    def level_decay(li):
        sgn = jnp.concatenate([sgn_ref[d, li]] * A_HEADS, axis=1).reshape(b3.shape)
        return jnp.exp2((b3 - _level_ref(b3, LEVELS[li], rev)) * sgn)

    shift = SUBLANES - 1 if rev else 1
    k_adj = pltpu.roll(k.reshape(b3.shape), shift, 1).reshape(CHUNK, A_WIDTH)
    v_adj = pltpu.roll(v.reshape(b3.shape), shift, 1).reshape(CHUNK, A_WIDTH)
    p_same = q * k
    p_adj = q * level_decay(0).reshape(CHUNK, A_WIDTH) * k_adj
    row = lax.broadcasted_iota(jnp.int32, (CHUNK, 1), 0)
    is_query = (row % 2 == 0) if rev else (row % 2 == 1)
    near = []
    for h in range(A_HEADS):
        sl = slice(h * A_DK, (h + 1) * A_DK)
        a_same = jnp.sum(p_same[:, sl], axis=-1, keepdims=True)
        a_adj = jnp.where(is_query, jnp.sum(p_adj[:, sl], axis=-1, keepdims=True), 0.0)
        near.append(a_same * v[:, sl] + a_adj * v_adj[:, sl])
    pairs = [slice(p * 2 * A_DK, (p + 1) * 2 * A_DK) for p in range(A_HEADS // 2)]
    acc = None
    for li in range(1, len(LEVELS)):
        e = level_decay(li).reshape(CHUNK, A_WIDTH).astype(BF16)
        qs = qb * e
        ks = kb * e
        sc = [_dot_nt(qs[:, p], _pair_blockdiag(ks[:, p])) * m_ref[d, li - 1] for p in pairs]
        acc = sc if acc is None else [a + s for a, s in zip(acc, sc)]
    intra = [_dot(a.astype(BF16), _pair_blockdiag(vb[:, p])) for a, p in zip(acc, pairs)]
    return jnp.concatenate(intra, axis=1) + jnp.concatenate(inter, axis=1) + jnp.concatenate(near, axis=1)


def _hgrn_body(layer, tbl_ref, qf_ref, zf_ref, vf_ref, qb_ref, zb_ref, vb_ref, lbl_ref, m_ref, sgn_ref,
               of_ref, ob_ref, s_ref):
    step = pl.program_id(0)

    @pl.when(tbl_ref[2, step] == 1)
    def _():
        s_ref[...] = jnp.zeros_like(s_ref)

    n_even = lbl_ref.shape[0]
    logits = [lbl_ref[i] for i in range(n_even)]
    mx = functools.reduce(jnp.maximum, logits)
    ex = [jnp.exp(l - mx) for l in logits]
    den = functools.reduce(lambda x, y: x + y, ex)
    ps = [e / den for e in ex]
    lbs = functools.reduce(lambda x, y: x + y, ps[:layer + 1]) - ps[0]

    nc = HGRN_BLOCK // CHUNK
    refs = ((qf_ref, zf_ref, vf_ref, of_ref), (qb_ref, zb_ref, vb_ref, ob_ref))
    tris = (_tri(False), _tri(True))

    def chunk(c, carry):
        prep = []
        for d, rev in enumerate((False, True)):
            q_ref, z_ref, v_ref, _ = refs[d]
            r0 = pl.multiple_of(((nc - 1 - c) if rev else c) * CHUNK, CHUNK)
            rows = pl.ds(r0, CHUNK)
            qa = q_ref[rows, :]
            z = z_ref[rows, :]
            lb = lbs[d:d + 1, :]
            q = qa * _sigmoid(qa)
            f = lb + (1.0 - lb) * _sigmoid(z)
            k = 1.0 - f
            lf = jnp.log(f)
            hi = lf.astype(BF16)
            r1 = lf - hi.astype(F32)
            mid = r1.astype(BF16)
            lo = (r1 - mid.astype(F32)).astype(BF16)
            cs = _dot(tris[d], jnp.concatenate([hi, mid, lo], axis=1))
            b = (cs[:, :A_WIDTH] + cs[:, A_WIDTH:2 * A_WIDTH] + cs[:, 2 * A_WIDTH:]) * LOG2E
            prep.append((rows, q, k, v_ref[rows, :], b))
        for d, rev in enumerate((False, True)):
            rows, q, k, v, b = prep[d]
            refs[d][3][rows, :] = _hgrn_chunk(q, k, v, b, s_ref, d, rev, m_ref, sgn_ref)
        return carry

    lax.fori_loop(0, nc, chunk, 0, unroll=True)


def _hgrn_table(segs):
    rows = []
    for off, batch, seq in segs:
        nblk = seq // HGRN_BLOCK
        for bi in range(batch):
            base = (off + bi * seq) // HGRN_BLOCK
            for i in range(nblk):
                rows.append((base + i, base + nblk - 1 - i, int(i == 0)))
    return np.asarray(rows, np.int32).T.copy()


def _hgrn(pa, lb_logits, layer, segs):
    t = pa.shape[0]
    tbl = _hgrn_table(segs)
    steps = tbl.shape[1]
    masks = _hgrn_masks()
    signs = _hgrn_signs()

    def spec(col, which):
        return pl.BlockSpec((HGRN_BLOCK, A_WIDTH), lambda i, tb: (tb[which, i], col))

    grid_spec = pltpu.PrefetchScalarGridSpec(
        num_scalar_prefetch=1,
        grid=(steps,),
        in_specs=[spec(0, 0), spec(1, 0), spec(3, 0), spec(0, 1), spec(2, 1), spec(3, 1),
                  pl.BlockSpec(lb_logits.shape, lambda i, tb: (0, 0, 0)),
                  pl.BlockSpec(masks.shape, lambda i, tb: (0, 0, 0, 0)),
                  pl.BlockSpec(signs.shape, lambda i, tb: (0, 0, 0, 0))],
        out_specs=[pl.BlockSpec((HGRN_BLOCK, A_WIDTH), lambda i, tb: (tb[0, i], 0)),
                   pl.BlockSpec((HGRN_BLOCK, A_WIDTH), lambda i, tb: (tb[1, i], 0))],
        scratch_shapes=[pltpu.VMEM((2, A_HEADS, A_DK, A_DK), F32)],
    )
    return pl.pallas_call(
        functools.partial(_hgrn_body, layer),
        grid_spec=grid_spec,
        out_shape=[jax.ShapeDtypeStruct((t, A_WIDTH), F32),
                   jax.ShapeDtypeStruct((t, A_WIDTH), F32)],
        compiler_params=pltpu.CompilerParams(dimension_semantics=("arbitrary",),
                                             vmem_limit_bytes=VMEM_LIMIT),
        name="hgrn",
    )(jnp.asarray(tbl), pa, pa, pa, pa, pa, pa, lb_logits, jnp.asarray(masks), jnp.asarray(signs))


def _na_build_bias(rpb_ref, bias_ref):
    n_ro = 2 * NA_KR - 1
    n_co = 2 * NA_KC - 1
    qc = lax.broadcasted_iota(jnp.int32, (GRID_W, GRID_W), 0)
    kc = lax.broadcasted_iota(jnp.int32, (GRID_W, GRID_W), 1)
    co = jnp.clip(kc - qc, -(NA_KC - 1), NA_KC - 1) + NA_KC - 1
    ws = jnp.clip(qc - NA_KC // 2, 0, GRID_W - NA_KC)
    valid = (kc >= ws) & (kc < ws + NA_KC)

    def build(hr, carry):
        h = hr // n_ro
        ro = hr % n_ro
        t = jnp.full((GRID_W, GRID_W), NEG, F32)
        for j in range(n_co):
            t = jnp.where(co == j, rpb_ref[hr * n_co + j], t)
        t = jnp.where(valid, t, NEG)
        for i in range(NA_KR):
            dd = ro - i

            @pl.when((dd >= 0) & (dd < NA_KR))
            def _():
                bias_ref[h, dd, :, i * GRID_W:(i + 1) * GRID_W] = t
        return carry

    lax.fori_loop(0, NA_HEADS * n_ro, build, 0)


def _na_body(tbl_ref, rpb_ref, q_ref, k_ref, v_ref, o_ref, bias_ref):
    step = pl.program_id(0)

    @pl.when(step == 0)
    def _():
        _na_build_bias(rpb_ref, bias_ref)

    r0 = tbl_ref[2, step]
    w0 = tbl_ref[3, step]
    n_rows = tbl_ref[4, step]
    head = lax.broadcasted_iota(jnp.int32, (GRID_W, NA_GROUP * NA_HD), 1) // NA_HD
    nkeys = NA_KR * GRID_W
    scale = jnp.asarray(NA_HD ** -0.5, BF16)

    groups = [slice(g * NA_GROUP * NA_HD, (g + 1) * NA_GROUP * NA_HD) for g in range(NA_HEADS // NA_GROUP)]

    def rows(it, carry):
        units = []
        for u in range(NA_UNROLL):
            rr = it * NA_UNROLL + u
            r = r0 + rr
            row_start = jnp.clip(r - NA_KR // 2, 0, n_rows - NA_KR)
            dd = row_start - r + (NA_KR - 1)
            keys = pl.ds(pl.multiple_of((row_start - w0) * GRID_W, GRID_W), nkeys)
            qrows = pl.ds(pl.multiple_of(rr * GRID_W, GRID_W), GRID_W)
            units += [(dd, keys, qrows, g, lanes) for g, lanes in enumerate(groups)]
        scores = []
        for dd, keys, qrows, g, lanes in units:
            qp = q_ref[qrows, lanes] * scale
            zero = jnp.zeros_like(qp)
            qs = jnp.concatenate([jnp.where(head == h, qp, zero) for h in range(NA_GROUP)], axis=0)
            scores.append(_dot_nt(qs, k_ref[keys, lanes]))
        probs = []
        for (dd, keys, qrows, g, lanes), s in zip(units, scores):
            s = s + jnp.concatenate([bias_ref[NA_GROUP * g + h, dd] for h in range(NA_GROUP)], axis=0)
            m = jnp.max(s, axis=-1, keepdims=True)
            p = jnp.exp(s - m)
            probs.append((p.astype(BF16), jnp.sum(p, axis=-1, keepdims=True)))
        for (dd, keys, qrows, g, lanes), (p, l) in zip(units, probs):
            pv = _dot(p, v_ref[keys, lanes]) / l
            o = pv[:GRID_W]
            for h in range(1, NA_GROUP):
                o = jnp.where(head == h, pv[h * GRID_W:(h + 1) * GRID_W], o)
            o_ref[qrows, lanes] = o.astype(o_ref.dtype)
        return carry

    lax.fori_loop(0, NA_ROWS // NA_UNROLL, rows, 0)


def _na_table(segs):
    blk = NA_ROWS * GRID_W
    rows = []
    for off, batch, seq in segs:
        n_rows = seq // GRID_W
        assert n_rows >= NA_WIN
        for bi in range(batch):
            base = off + bi * seq
            for i in range(seq // blk):
                r0 = i * NA_ROWS
                w0 = min(max(r0 - NA_ROWS, 0), n_rows - NA_WIN)
                rows.append((base // blk + i, base // GRID_W + w0, r0, w0, n_rows))
    return np.asarray(rows, np.int32).T.copy()


def _na(pn, rpb, segs):
    t = pn.shape[0]
    tbl = _na_table(segs)
    steps = tbl.shape[1]
    blk = NA_ROWS * GRID_W

    def window(col):
        return pl.BlockSpec((pl.Element(NA_WIN * GRID_W), pl.Element(NA_WIDTH)),
                            lambda i, tb: (tb[1, i] * GRID_W, col * NA_WIDTH))

    grid_spec = pltpu.PrefetchScalarGridSpec(
        num_scalar_prefetch=1,
        grid=(steps,),
        in_specs=[pl.BlockSpec(memory_space=pltpu.SMEM),
                  pl.BlockSpec((blk, NA_WIDTH), lambda i, tb: (tb[0, i], 0)),
                  window(1), window(2)],
        out_specs=pl.BlockSpec((blk, NA_WIDTH), lambda i, tb: (tb[0, i], 0)),
        scratch_shapes=[pltpu.VMEM((NA_HEADS, NA_KR, GRID_W, NA_KR * GRID_W), F32)],
    )
    return pl.pallas_call(
        _na_body,
        grid_spec=grid_spec,
        out_shape=jax.ShapeDtypeStruct((t, NA_WIDTH), BF16),
        compiler_params=pltpu.CompilerParams(dimension_semantics=("arbitrary",),
                                             vmem_limit_bytes=VMEM_LIMIT),
        name="natten",
    )(jnp.asarray(tbl), rpb.astype(F32).reshape(-1), pn, pn, pn)


def _out_specs_shapes(out_tokens):
    starts = [0]
    for t in out_tokens:
        starts.append(starts[-1] + t // TM)
    specs = []
    for k in range(len(out_tokens)):
        s, n = starts[k], starts[k + 1] - starts[k]
        specs.append(pl.BlockSpec((TM, D_MODEL), lambda i, *_, s=s, n=n: (jnp.clip(i - s, 0, n - 1), 0)))
    shapes = [jax.ShapeDtypeStruct((t, D_MODEL), F32) for t in out_tokens]
    return tuple(starts), specs, shapes


def _store_parts(starts, out_refs, accs):
    def store(o_ref):
        for sl, acc in zip(SLABS, accs):
            o_ref[sl, :] = acc

    if len(out_refs) == 1:
        store(out_refs[0])
        return
    step = pl.program_id(0)
    for k, o_ref in enumerate(out_refs):
        @pl.when((step >= starts[k]) & (step < starts[k + 1]))
        def _(o_ref=o_ref):
            store(o_ref)


def _ffn_tail(x1s, gf_ref, wi_ref, wo_ref, gfin_ref, final):
    hs = [_rms(x1, gf_ref[...]).astype(BF16) for x1 in x1s]
    accs = list(x1s)
    for c in range(0, D_FF, FF_CHUNK):
        gus = [(_dot(h, wi_ref[:, c:c + FF_CHUNK]), _dot(h, wi_ref[:, D_FF + c:D_FF + c + FF_CHUNK])) for h in hs]
        acts = [(g * _sigmoid(g) * u).astype(BF16) for g, u in gus]
        accs = [acc + _dot(a, wo_ref[c:c + FF_CHUNK, :]) for acc, a in zip(accs, acts)]
    if final:
        accs = [_rms(acc, gfin_ref[...]) for acc in accs]
    return accs


def _post_even_body(final, starts, out_starts, *refs):
    n = len(starts) - 1
    of_ref, ob_ref, ga_ref, n_ref, gain_ref, wm_ref, gf_ref, wi_ref, wo_ref, gfin_ref = refs[n:n + 10]
    x = _select_part(starts, refs[:n])
    x1s = []
    for sl in SLABS:
        o = of_ref[sl, :] + ob_ref[sl, :]
        parts = []
        for h in range(A_HEADS):
            oh = o[:, h * A_DK:(h + 1) * A_DK]
            parts.append(oh * lax.rsqrt(jnp.mean(oh * oh, axis=-1, keepdims=True) + EPS))
        on = jnp.concatenate(parts, axis=1) * gain_ref[...]
        g = ga_ref[sl, :]
        oa = (on * (g * _sigmoid(g))).astype(BF16)
        mix = _dot(oa, wm_ref[0:A_WIDTH, :]) + _dot(n_ref[sl, :], wm_ref[A_WIDTH:A_WIDTH + NA_WIDTH, :])
        x1s.append(x[sl, :] + mix)
    accs = _ffn_tail(x1s, gf_ref, wi_ref, wo_ref, gfin_ref, final)
    _store_parts(out_starts, refs[n + 10:], accs)


def _post_even(parts, o_f, o_b, pa, o_n, gain, w_mix, g_ffn, w_ffn_in, w_ffn_out, g_final, final, out_tokens):
    starts = _tile_starts(parts)
    out_starts, out_specs, out_shapes = _out_specs_shapes(out_tokens)
    tok = lambda w, col=0: pl.BlockSpec((TM, w), lambda i: (i, col))
    return pl.pallas_call(
        functools.partial(_post_even_body, final, starts, out_starts),
        grid=(starts[-1],),
        in_specs=_part_specs(parts) + [tok(A_WIDTH), tok(A_WIDTH), tok(A_WIDTH, 4), tok(NA_WIDTH),
                                       _resident((1, A_WIDTH)), _resident((A_WIDTH + NA_WIDTH, D_MODEL)),
                                       _resident((1, D_MODEL)), _resident((D_MODEL, 2 * D_FF)),
                                       _resident((D_FF, D_MODEL)), _resident((1, D_MODEL))],
        out_specs=out_specs,
        out_shape=out_shapes,
        compiler_params=pltpu.CompilerParams(dimension_semantics=("arbitrary",),
                                             vmem_limit_bytes=VMEM_LIMIT),
        name="post_even",
    )(*parts, o_f, o_b, pa, o_n, gain, w_mix, g_ffn, w_ffn_in, w_ffn_out, g_final)


def _odd_body(final, out_starts, tbl_ref, x_ref, xp_ref, xn_ref, gm_ref, wc_ref, cw_ref, wco_ref, gf_ref,
              wi_ref, wo_ref, gfin_ref, *out_refs):
    step = pl.program_id(0)
    x = x_ref[...]
    gm = gm_ref[...]
    hs = [_rms(x[sl, :], gm).astype(BF16) for sl in SLABS]
    hh = _rms(jnp.concatenate([xp_ref[...], xn_ref[...]], axis=0), gm).astype(BF16)
    zin = [jnp.concatenate([hs[0], hh], axis=0)] + hs[1:]
    zs = [_dot(h, wc_ref[:, D_MODEL:2 * D_MODEL]) * _dot(h, wc_ref[:, 2 * D_MODEL:3 * D_MODEL]) for h in zin]
    zh = zs[0][TM // len(SLABS):, :]
    zs[0] = zs[0][:TM // len(SLABS), :]
    bgs = [_dot(h, wc_ref[:, 0:D_MODEL]) for h in hs]
    z_prev = jnp.where(tbl_ref[2, step] == 1, 0.0, zh[HALO - 1:HALO, :])
    z_next = jnp.where(tbl_ref[3, step] == 1, 0.0, zh[HALO:HALO + 1, :])
    rows = TM // len(SLABS)
    row = lax.broadcasted_iota(jnp.int32, (rows, 1), 0)
    x1s = []
    for k, (sl, z, bg) in enumerate(zip(SLABS, zs, bgs)):
        above = z_prev if k == 0 else zs[k - 1][rows - 1:rows, :]
        below = z_next if k == len(SLABS) - 1 else zs[k + 1][0:1, :]
        z_dn = jnp.where(row == 0, above, pltpu.roll(z, 1, 0))
        z_up = jnp.where(row == rows - 1, below, pltpu.roll(z, rows - 1, 0))
        conv = z_dn * cw_ref[0:1, :] + z * cw_ref[1:2, :] + z_up * cw_ref[2:3, :]
        x1s.append(x[sl, :] + _dot((bg * conv).astype(BF16), wco_ref[...]))
    accs = _ffn_tail(x1s, gf_ref, wi_ref, wo_ref, gfin_ref, final)
    _store_parts(out_starts, out_refs, accs)


def _odd_table(segs, t):
    rows = []
    starts = set()
    ends = set()
    for off, batch, seq in segs:
        for bi in range(batch):
            starts.add(off + bi * seq)
            ends.add(off + (bi + 1) * seq)
    for i in range(t // TM):
        t0 = i * TM
        first = int(t0 in starts)
        last = int(t0 + TM in ends)
        rows.append((max(t0 // HALO - 1, 0), min((t0 + TM) // HALO, t // HALO - 1), first, last))
    return np.asarray(rows, np.int32).T.copy()


def _odd(x, g_mix, w_conv_in, conv_w, w_conv_out, g_ffn, w_ffn_in, w_ffn_out, g_final, final, segs, out_tokens):
    t = x.shape[0]
    tbl = _odd_table(segs, t)
    out_starts, out_specs, out_shapes = _out_specs_shapes(out_tokens)
    res = lambda shape: pl.BlockSpec(shape, lambda i, tb: (0,) * len(shape), pipeline_mode=pl.Buffered(1))
    grid_spec = pltpu.PrefetchScalarGridSpec(
        num_scalar_prefetch=1,
        grid=(t // TM,),
        in_specs=[pl.BlockSpec((TM, D_MODEL), lambda i, tb: (i, 0)),
                  pl.BlockSpec((HALO, D_MODEL), lambda i, tb: (tb[0, i], 0)),
                  pl.BlockSpec((HALO, D_MODEL), lambda i, tb: (tb[1, i], 0)),
                  res((1, D_MODEL)), res((D_MODEL, 3 * D_MODEL)), res((CONV_W, D_MODEL)),
                  res((D_MODEL, D_MODEL)), res((1, D_MODEL)), res((D_MODEL, 2 * D_FF)),
                  res((D_FF, D_MODEL)), res((1, D_MODEL))],
        out_specs=out_specs,
    )
    return pl.pallas_call(
        functools.partial(_odd_body, final, out_starts),
        grid_spec=grid_spec,
        out_shape=out_shapes,
        compiler_params=pltpu.CompilerParams(dimension_semantics=("arbitrary",),
                                             vmem_limit_bytes=VMEM_LIMIT),
        name="odd_layer",
    )(jnp.asarray(tbl), x, x, x, g_mix, w_conv_in, conv_w, w_conv_out, g_ffn, w_ffn_in, w_ffn_out, g_final)


def _trunk(xs, norm_mix, norm_ffn, norm_final, even_w_in, even_w_out, hgrn_lb_logits, hgrn_norm,
           na_rpb, conv_w_in, conv_w, conv_w_out, ffn_w_in, ffn_w_out):
    depth = norm_mix.shape[0]
    segs = []
    off = 0
    for a in xs:
        segs.append((off, a.shape[0], a.shape[1]))
        off += a.shape[0] * a.shape[1]
    parts = [a.reshape(-1, D_MODEL) for a in xs]
    part_tokens = [p.shape[0] for p in parts]
    total = off
    row = lambda v: v.reshape(1, -1).astype(F32)
    g_final = row(norm_final)
    for l in range(depth):
        final = l == depth - 1
        out_tokens = part_tokens if final else [total]
        w_fi = ffn_w_in[l].astype(BF16)
        w_fo = ffn_w_out[l].astype(BF16)
        if l % 2 == 0:
            e = l // 2
            pa, pn = _in_even(parts, row(norm_mix[l]), even_w_in[e].astype(BF16))
            o_f, o_b = _hgrn(pa, hgrn_lb_logits.astype(F32), e, segs)
            o_n = _na(pn, na_rpb[e], segs)
            outs = _post_even(parts, o_f, o_b, pa, o_n, row(hgrn_norm[e]), even_w_out[e].astype(BF16),
                              row(norm_ffn[l]), w_fi, w_fo, g_final, final, out_tokens)
        else:
            o = l // 2
            x = parts[0] if len(parts) == 1 else jnp.concatenate(parts, axis=0)
            outs = _odd(x, row(norm_mix[l]), conv_w_in[o].astype(BF16), conv_w[o].astype(F32),
                        conv_w_out[o].astype(BF16), row(norm_ffn[l]), w_fi, w_fo, g_final, final, segs,
                        out_tokens)
        parts = list(outs)
    return tuple(p.reshape(a.shape) for p, a in zip(parts, xs))


def kernel(x_prompt, x_sample, norm_mix, norm_ffn, norm_final, even_w_in, even_w_out, hgrn_lb_logits,
           hgrn_norm, na_rpb, conv_w_in, conv_w, conv_w_out, ffn_w_in, ffn_w_out):
    return _trunk([x_prompt, x_sample], norm_mix, norm_ffn, norm_final, even_w_in, even_w_out,
                  hgrn_lb_logits, hgrn_norm, na_rpb, conv_w_in, conv_w, conv_w_out, ffn_w_in, ffn_w_out)
```

```python
import functools

import numpy as np
import jax
import jax.numpy as jnp
from jax import lax
from jax.experimental import pallas as pl
from jax.experimental.pallas import tpu as pltpu

F32 = jnp.float32
BF16 = jnp.bfloat16

SUBLANES = 8
D_MODEL = 1024
EPS = 1e-6
NEG = -1e30
LOG2E = 1.4426950408889634
GRID_W = 64
A_WIDTH = 512
A_HEADS = 4
A_DK = 128
CHUNK = 64
LEVELS = (2, 4, 8, 16, 32, 64)
NA_WIDTH = 512
NA_HEADS = 8
NA_HD = 64
NA_KR = 8
NA_KC = 16
NA_GROUP = 2
CONV_W = 3
D_FF = 2816
FF_CHUNK = 256
PA_W = 5 * A_WIDTH
PA_Q, PA_ZF, PA_ZB, PA_I, PA_G = range(5)
PN_W = 3 * NA_WIDTH
EVEN_IN = PA_W + PN_W

TM = 512
SLABS = (slice(0, TM // 2), slice(TM // 2, TM))
HGRN_BLOCK = 1024
NA_ROWS = 8
NA_WIN = 3 * NA_ROWS
HALO = 8
V7X_VMEM_BYTES = 64 * 1024 * 1024
VMEM_LIMIT = V7X_VMEM_BYTES * 7 // 8

NT_DIMS = (((1,), (1,)), ((), ()))
TN_DIMS = (((0,), (0,)), ((), ()))


def _resident(shape):
    nd = len(shape)
    return pl.BlockSpec(shape, lambda *_: (0,) * nd, pipeline_mode=pl.Buffered(1))


def _rms(x, g):
    return x * lax.rsqrt(jnp.mean(x * x, axis=-1, keepdims=True) + EPS) * g


def _sigmoid(x):
    return jax.nn.sigmoid(x)


def _dot(a, b):
    return jnp.dot(a, b, preferred_element_type=F32)


def _dot_nt(a, b):
    return lax.dot_general(a, b, NT_DIMS, preferred_element_type=F32)


def _tile_starts(parts):
    starts = [0]
    for a in parts:
        starts.append(starts[-1] + a.shape[0] // TM)
    return tuple(starts)


def _part_specs(parts):
    starts = _tile_starts(parts)
    return [pl.BlockSpec((TM, D_MODEL),
                         lambda i, *_, s=starts[k], n=starts[k + 1] - starts[k]: (jnp.clip(i - s, 0, n - 1), 0))
            for k in range(len(parts))]


def _select_part(starts, refs):
    step = pl.program_id(0)
    x = refs[-1][...]
    for k in range(len(refs) - 2, -1, -1):
        x = jnp.where(step < starts[k + 1], refs[k][...], x)
    return x


def _in_even_body(starts, *refs):
    n = len(starts) - 1
    g_ref, w_ref, pa_ref, pn_ref = refs[n:]
    h = _rms(_select_part(starts, refs[:n]), g_ref[...]).astype(BF16)
    for c in range(0, PA_W, A_WIDTH):
        pa_ref[:, c:c + A_WIDTH] = _dot(h, w_ref[:, c:c + A_WIDTH])
    for c in range(0, PN_W, NA_WIDTH):
        pn_ref[:, c:c + NA_WIDTH] = _dot(h, w_ref[:, PA_W + c:PA_W + c + NA_WIDTH]).astype(BF16)


def _in_even(parts, g, w):
    starts = _tile_starts(parts)
    t = starts[-1] * TM
    return pl.pallas_call(
        functools.partial(_in_even_body, starts),
        grid=(starts[-1],),
        in_specs=_part_specs(parts) + [_resident((1, D_MODEL)), _resident((D_MODEL, EVEN_IN))],
        out_specs=[pl.BlockSpec((TM, PA_W), lambda i: (i, 0)),
                   pl.BlockSpec((TM, PN_W), lambda i: (i, 0))],
        out_shape=[jax.ShapeDtypeStruct((t, PA_W), F32),
                   jax.ShapeDtypeStruct((t, PN_W), BF16)],
        compiler_params=pltpu.CompilerParams(dimension_semantics=("arbitrary",),
                                             vmem_limit_bytes=VMEM_LIMIT),
        name="in_even",
    )(*parts, g, w)


def _hgrn_masks():
    t = np.arange(CHUNK)[:, None]
    s = np.arange(CHUNK)[None, :]
    out = np.zeros((2, len(LEVELS) - 1, CHUNK, CHUNK), np.float32)
    for d, rev in enumerate((False, True)):
        for li, L in enumerate(LEVELS[1:]):
            half = L // 2
            same = (t // L) == (s // L)
            if rev:
                m = same & ((t % L) < half) & ((s % L) >= half)
            else:
                m = same & ((t % L) >= half) & ((s % L) < half)
            out[d, li] = m
    return np.tile(out, (1, 1, 1, 2))


def _hgrn_signs():
    t = np.arange(CHUNK)
    out = np.zeros((2, len(LEVELS), CHUNK), np.float32)
    for d, rev in enumerate((False, True)):
        for li, L in enumerate(LEVELS):
            second = (t % L) >= L // 2
            out[d, li] = np.where(second != rev, 1.0, -1.0)
    return np.ascontiguousarray(np.broadcast_to(out[..., None], out.shape + (A_DK,)))


def _tri(rev):
    t = lax.broadcasted_iota(jnp.int32, (CHUNK, CHUNK), 0)
    s = lax.broadcasted_iota(jnp.int32, (CHUNK, CHUNK), 1)
    return jnp.where((s >= t) if rev else (s <= t), 1.0, 0.0).astype(BF16)


def _level_ref(b3, L, rev):
    g, sub, w = b3.shape
    half = L // 2
    if L >= 2 * SUBLANES:
        n = L // SUBLANES
        pieces = []
        for blk in range(CHUNK // L):
            r = blk * L + (half if rev else half - 1)
            pieces.append(jnp.broadcast_to(b3[r // SUBLANES:r // SUBLANES + 1, r % SUBLANES:r % SUBLANES + 1, :],
                                           (n, sub, w)))
        return pieces[0] if len(pieces) == 1 else jnp.concatenate(pieces, axis=0)
    srow = lax.broadcasted_iota(jnp.int32, b3.shape, 1)
    if L == SUBLANES:
        r = half if rev else half - 1
        return jnp.broadcast_to(b3[:, r:r + 1, :], b3.shape)
    if L == 4:
        r = half if rev else half - 1
        lo = jnp.broadcast_to(b3[:, r:r + 1, :], b3.shape)
        hi = jnp.broadcast_to(b3[:, r + 4:r + 5, :], b3.shape)
        return jnp.where(srow < 4, lo, hi)
    odd = (srow % 2) == 1
    if rev:
        return jnp.where(odd, b3, pltpu.roll(b3, SUBLANES - 1, 1))
    return jnp.where(odd, pltpu.roll(b3, 1, 1), b3)


def _pair_blockdiag(x):
    zero = jnp.zeros((CHUNK, A_DK), x.dtype)
    return jnp.concatenate([jnp.concatenate([x[:, :A_DK], zero], axis=1),
                            jnp.concatenate([zero, x[:, A_DK:]], axis=1)], axis=0)


def _hgrn_chunk(q, k, v, b, s_ref, d, rev, m_ref, sgn_ref):
    tot = b[0:1, :] if rev else b[CHUNK - 1:CHUNK, :]
    qb = q.astype(BF16)
    kb = k.astype(BF16)
    q0 = qb * jnp.exp2(b).astype(BF16)
    k2 = kb * jnp.exp2(tot - b).astype(BF16)
    vb = v.astype(BF16)
    dec = jnp.exp2(tot)
    inter = []
    for h in range(A_HEADS):
        sl = slice(h * A_DK, (h + 1) * A_DK)
        s_t = s_ref[d, h]
        inter.append(_dot_nt(q0[:, sl], s_t.astype(BF16)))
        ds = lax.dot_general(vb[:, sl], k2[:, sl], TN_DIMS, preferred_element_type=F32)
        s_ref[d, h] = dec[:, sl] * s_t + ds
    b3 = b.reshape(CHUNK // SUBLANES, SUBLANES, A_WIDTH)

    def level_decay(li):
        sgn = jnp.concatenate([sgn_ref[d, li]] * A_HEADS, axis=1).reshape(b3.shape)
        return jnp.exp2((b3 - _level_ref(b3, LEVELS[li], rev)) * sgn)

    shift = SUBLANES - 1 if rev else 1
    k_adj = pltpu.roll(k.reshape(b3.shape), shift, 1).reshape(CHUNK, A_WIDTH)
    v_adj = pltpu.roll(v.reshape(b3.shape), shift, 1).reshape(CHUNK, A_WIDTH)
    p_same = q * k
    p_adj = q * level_decay(0).reshape(CHUNK, A_WIDTH) * k_adj
    row = lax.broadcasted_iota(jnp.int32, (CHUNK, 1), 0)
    is_query = (row % 2 == 0) if rev else (row % 2 == 1)
    near = []
    for h in range(A_HEADS):
        sl = slice(h * A_DK, (h + 1) * A_DK)
        a_same = jnp.sum(p_same[:, sl], axis=-1, keepdims=True)
        a_adj = jnp.where(is_query, jnp.sum(p_adj[:, sl], axis=-1, keepdims=True), 0.0)
        near.append(a_same * v[:, sl] + a_adj * v_adj[:, sl])
    pairs = [slice(p * 2 * A_DK, (p + 1) * 2 * A_DK) for p in range(A_HEADS // 2)]
    acc = None
    for li in range(1, len(LEVELS)):
        e = level_decay(li).reshape(CHUNK, A_WIDTH).astype(BF16)
        qs = qb * e
        ks = kb * e
        sc = [_dot_nt(qs[:, p], _pair_blockdiag(ks[:, p])) * m_ref[d, li - 1] for p in pairs]
        acc = sc if acc is None else [a + s for a, s in zip(acc, sc)]
    intra = [_dot(a.astype(BF16), _pair_blockdiag(vb[:, p])) for a, p in zip(acc, pairs)]
    return jnp.concatenate(intra, axis=1) + jnp.concatenate(inter, axis=1) + jnp.concatenate(near, axis=1)


def _hgrn_body(layer, tbl_ref, qf_ref, zf_ref, vf_ref, qb_ref, zb_ref, vb_ref, lbl_ref, m_ref, sgn_ref,
               of_ref, ob_ref, s_ref):
    step = pl.program_id(0)

    @pl.when(tbl_ref[2, step] == 1)
    def _():
        s_ref[...] = jnp.zeros_like(s_ref)

    n_even = lbl_ref.shape[0]
    logits = [lbl_ref[i] for i in range(n_even)]
    mx = functools.reduce(jnp.maximum, logits)
    ex = [jnp.exp(l - mx) for l in logits]
    den = functools.reduce(lambda x, y: x + y, ex)
    ps = [e / den for e in ex]
    lbs = functools.reduce(lambda x, y: x + y, ps[:layer + 1]) - ps[0]

    nc = HGRN_BLOCK // CHUNK
    refs = ((qf_ref, zf_ref, vf_ref, of_ref), (qb_ref, zb_ref, vb_ref, ob_ref))
    tris = (_tri(False), _tri(True))

    for c in range(nc):
        prep = []
        for d, rev in enumerate((False, True)):
            q_ref, z_ref, v_ref, _ = refs[d]
            r0 = ((nc - 1 - c) if rev else c) * CHUNK
            rows = slice(r0, r0 + CHUNK)
            qa = q_ref[rows, :]
            z = z_ref[rows, :]
            lb = lbs[d:d + 1, :]
            q = qa * _sigmoid(qa)
            f = lb + (1.0 - lb) * _sigmoid(z)
            k = 1.0 - f
            lf = jnp.log(f)
            hi = lf.astype(BF16)
            r1 = lf - hi.astype(F32)
            mid = r1.astype(BF16)
            lo = (r1 - mid.astype(F32)).astype(BF16)
            cs = _dot(tris[d], jnp.concatenate([hi, mid, lo], axis=1))
            b = (cs[:, :A_WIDTH] + cs[:, A_WIDTH:2 * A_WIDTH] + cs[:, 2 * A_WIDTH:]) * LOG2E
            prep.append((rows, q, k, v_ref[rows, :], b))
        for d, rev in enumerate((False, True)):
            rows, q, k, v, b = prep[d]
            refs[d][3][rows, :] = _hgrn_chunk(q, k, v, b, s_ref, d, rev, m_ref, sgn_ref)


def _hgrn_table(segs):
    rows = []
    for off, batch, seq in segs:
        assert seq % HGRN_BLOCK == 0 and off % HGRN_BLOCK == 0
        nblk = seq // HGRN_BLOCK
        for bi in range(batch):
            base = (off + bi * seq) // HGRN_BLOCK
            for i in range(nblk):
                rows.append((base + i, base + nblk - 1 - i, int(i == 0)))
    return np.asarray(rows, np.int32).T.copy()


def _hgrn(pa, lb_logits, layer, segs):
    t = pa.shape[0]
    tbl = _hgrn_table(segs)
    steps = tbl.shape[1]
    masks = _hgrn_masks()
    signs = _hgrn_signs()

    def spec(col, which):
        return pl.BlockSpec((HGRN_BLOCK, A_WIDTH), lambda i, tb: (tb[which, i], col))

    grid_spec = pltpu.PrefetchScalarGridSpec(
        num_scalar_prefetch=1,
        grid=(steps,),
        in_specs=[spec(PA_Q, 0), spec(PA_ZF, 0), spec(PA_I, 0), spec(PA_Q, 1), spec(PA_ZB, 1), spec(PA_I, 1),
                  pl.BlockSpec(lb_logits.shape, lambda i, tb: (0, 0, 0)),
                  pl.BlockSpec(masks.shape, lambda i, tb: (0, 0, 0, 0)),
                  pl.BlockSpec(signs.shape, lambda i, tb: (0, 0, 0, 0))],
        out_specs=[pl.BlockSpec((HGRN_BLOCK, A_WIDTH), lambda i, tb: (tb[0, i], 0)),
                   pl.BlockSpec((HGRN_BLOCK, A_WIDTH), lambda i, tb: (tb[1, i], 0))],
        scratch_shapes=[pltpu.VMEM((2, A_HEADS, A_DK, A_DK), F32)],
    )
    return pl.pallas_call(
        functools.partial(_hgrn_body, layer),
        grid_spec=grid_spec,
        out_shape=[jax.ShapeDtypeStruct((t, A_WIDTH), F32),
                   jax.ShapeDtypeStruct((t, A_WIDTH), F32)],
        compiler_params=pltpu.CompilerParams(dimension_semantics=("arbitrary",),
                                             vmem_limit_bytes=VMEM_LIMIT),
        name="hgrn",
    )(jnp.asarray(tbl), pa, pa, pa, pa, pa, pa, lb_logits, jnp.asarray(masks), jnp.asarray(signs))


def _na_build_bias(rpb_ref, bias_ref):
    n_ro = 2 * NA_KR - 1
    n_co = 2 * NA_KC - 1
    qc = lax.broadcasted_iota(jnp.int32, (GRID_W, GRID_W), 0)
    kc = lax.broadcasted_iota(jnp.int32, (GRID_W, GRID_W), 1)
    co = jnp.clip(kc - qc, -(NA_KC - 1), NA_KC - 1) + NA_KC - 1
    ws = jnp.clip(qc - NA_KC // 2, 0, GRID_W - NA_KC)
    valid = (kc >= ws) & (kc < ws + NA_KC)

    def build(hr, carry):
        h = hr // n_ro
        ro = hr % n_ro
        t = jnp.full((GRID_W, GRID_W), NEG, F32)
        for j in range(n_co):
            t = jnp.where(co == j, rpb_ref[hr * n_co + j], t)
        t = jnp.where(valid, t, NEG)
        for i in range(NA_KR):
            dd = ro - i

            @pl.when((dd >= 0) & (dd < NA_KR))
            def _():
                bias_ref[h, dd, :, i * GRID_W:(i + 1) * GRID_W] = t
        return carry

    lax.fori_loop(0, NA_HEADS * n_ro, build, 0)


def _na_body(tbl_ref, rpb_ref, q_ref, k_ref, v_ref, o_ref, bias_ref):
    step = pl.program_id(0)

    @pl.when(step == 0)
    def _():
        _na_build_bias(rpb_ref, bias_ref)

    r0 = tbl_ref[2, step]
    w0 = tbl_ref[3, step]
    n_rows = tbl_ref[4, step]
    head = lax.broadcasted_iota(jnp.int32, (GRID_W, NA_GROUP * NA_HD), 1) // NA_HD
    nkeys = NA_KR * GRID_W
    scale = jnp.asarray(NA_HD ** -0.5, BF16)

    groups = [slice(g * NA_GROUP * NA_HD, (g + 1) * NA_GROUP * NA_HD) for g in range(NA_HEADS // NA_GROUP)]

    units = []
    for rr in range(NA_ROWS):
        r = r0 + rr
        row_start = jnp.clip(r - NA_KR // 2, 0, n_rows - NA_KR)
        dd = row_start - r + (NA_KR - 1)
        keys = pl.ds(pl.multiple_of((row_start - w0) * GRID_W, GRID_W), nkeys)
        qrows = slice(rr * GRID_W, (rr + 1) * GRID_W)
        units += [(dd, keys, qrows, g, lanes) for g, lanes in enumerate(groups)]
    scores = []
    for dd, keys, qrows, g, lanes in units:
        qp = q_ref[qrows, lanes] * scale
        zero = jnp.zeros_like(qp)
        qs = jnp.concatenate([jnp.where(head == h, qp, zero) for h in range(NA_GROUP)], axis=0)
        scores.append(_dot_nt(qs, k_ref[keys, lanes]))
    probs = []
    for (dd, keys, qrows, g, lanes), s in zip(units, scores):
        s = s + jnp.concatenate([bias_ref[NA_GROUP * g + h, dd] for h in range(NA_GROUP)], axis=0)
        m = jnp.max(s, axis=-1, keepdims=True)
        p = jnp.exp(s - m)
        probs.append((p.astype(BF16), jnp.sum(p, axis=-1, keepdims=True)))
    for (dd, keys, qrows, g, lanes), (p, l) in zip(units, probs):
        pv = _dot(p, v_ref[keys, lanes]) / l
        o = pv[:GRID_W]
        for h in range(1, NA_GROUP):
            o = jnp.where(head == h, pv[h * GRID_W:(h + 1) * GRID_W], o)
        o_ref[qrows, lanes] = o.astype(o_ref.dtype)


def _na_table(segs):
    blk = NA_ROWS * GRID_W
    rows = []
    for off, batch, seq in segs:
        n_rows = seq // GRID_W
        assert n_rows >= NA_WIN and seq % blk == 0 and off % blk == 0
        for bi in range(batch):
            base = off + bi * seq
            for i in range(seq // blk):
                r0 = i * NA_ROWS
                w0 = min(max(r0 - NA_ROWS, 0), n_rows - NA_WIN)
                rows.append((base // blk + i, base // GRID_W + w0, r0, w0, n_rows))
    return np.asarray(rows, np.int32).T.copy()


def _na(pn, rpb, segs):
    t = pn.shape[0]
    tbl = _na_table(segs)
    steps = tbl.shape[1]
    blk = NA_ROWS * GRID_W

    def window(col):
        return pl.BlockSpec((pl.Element(NA_WIN * GRID_W), pl.Element(NA_WIDTH)),
                            lambda i, tb: (tb[1, i] * GRID_W, col * NA_WIDTH))

    grid_spec = pltpu.PrefetchScalarGridSpec(
        num_scalar_prefetch=1,
        grid=(steps,),
        in_specs=[pl.BlockSpec(memory_space=pltpu.SMEM),
                  pl.BlockSpec((blk, NA_WIDTH), lambda i, tb: (tb[0, i], 0)),
                  window(1), window(2)],
        out_specs=pl.BlockSpec((blk, NA_WIDTH), lambda i, tb: (tb[0, i], 0)),
        scratch_shapes=[pltpu.VMEM((NA_HEADS, NA_KR, GRID_W, NA_KR * GRID_W), F32)],
    )
    return pl.pallas_call(
        _na_body,
        grid_spec=grid_spec,
        out_shape=jax.ShapeDtypeStruct((t, NA_WIDTH), BF16),
        compiler_params=pltpu.CompilerParams(dimension_semantics=("arbitrary",),
                                             vmem_limit_bytes=VMEM_LIMIT),
        name="natten",
    )(jnp.asarray(tbl), rpb.astype(F32).reshape(-1), pn, pn, pn)


def _out_specs_shapes(out_tokens):
    starts = [0]
    for t in out_tokens:
        starts.append(starts[-1] + t // TM)
    specs = []
    for k in range(len(out_tokens)):
        s, n = starts[k], starts[k + 1] - starts[k]
        specs.append(pl.BlockSpec((TM, D_MODEL), lambda i, *_, s=s, n=n: (jnp.clip(i - s, 0, n - 1), 0)))
    shapes = [jax.ShapeDtypeStruct((t, D_MODEL), F32) for t in out_tokens]
    return tuple(starts), specs, shapes


def _store_parts(starts, out_refs, accs):
    def store(o_ref):
        for sl, acc in zip(SLABS, accs):
            o_ref[sl, :] = acc

    if len(out_refs) == 1:
        store(out_refs[0])
        return
    step = pl.program_id(0)
    for k, o_ref in enumerate(out_refs):
        @pl.when((step >= starts[k]) & (step < starts[k + 1]))
        def _(o_ref=o_ref):
            store(o_ref)


def _ffn_tail(x1s, gf_ref, wi_ref, wo_ref, gfin_ref, final):
    hs = [_rms(x1, gf_ref[...]).astype(BF16) for x1 in x1s]
    accs = list(x1s)
    for c in range(0, D_FF, FF_CHUNK):
        gus = [(_dot(h, wi_ref[:, c:c + FF_CHUNK]), _dot(h, wi_ref[:, D_FF + c:D_FF + c + FF_CHUNK])) for h in hs]
        acts = [(g * _sigmoid(g) * u).astype(BF16) for g, u in gus]
        accs = [acc + _dot(a, wo_ref[c:c + FF_CHUNK, :]) for acc, a in zip(accs, acts)]
    if final:
        accs = [_rms(acc, gfin_ref[...]) for acc in accs]
    return accs


def _post_even_body(final, starts, out_starts, *refs):
    n = len(starts) - 1
    of_ref, ob_ref, ga_ref, n_ref, gain_ref, wm_ref, gf_ref, wi_ref, wo_ref, gfin_ref = refs[n:n + 10]
    x = _select_part(starts, refs[:n])
    x1s = []
    for sl in SLABS:
        o = of_ref[sl, :] + ob_ref[sl, :]
        parts = []
        for h in range(A_HEADS):
            oh = o[:, h * A_DK:(h + 1) * A_DK]
            parts.append(oh * lax.rsqrt(jnp.mean(oh * oh, axis=-1, keepdims=True) + EPS))
        on = jnp.concatenate(parts, axis=1) * gain_ref[...]
        g = ga_ref[sl, :]
        oa = (on * (g * _sigmoid(g))).astype(BF16)
        mix = _dot(oa, wm_ref[0:A_WIDTH, :]) + _dot(n_ref[sl, :], wm_ref[A_WIDTH:A_WIDTH + NA_WIDTH, :])
        x1s.append(x[sl, :] + mix)
    accs = _ffn_tail(x1s, gf_ref, wi_ref, wo_ref, gfin_ref, final)
    _store_parts(out_starts, refs[n + 10:], accs)


def _post_even(parts, o_f, o_b, pa, o_n, gain, w_mix, g_ffn, w_ffn_in, w_ffn_out, g_final, final, out_tokens):
    starts = _tile_starts(parts)
    out_starts, out_specs, out_shapes = _out_specs_shapes(out_tokens)
    tok = lambda w, col=0: pl.BlockSpec((TM, w), lambda i: (i, col))
    return pl.pallas_call(
        functools.partial(_post_even_body, final, starts, out_starts),
        grid=(starts[-1],),
        in_specs=_part_specs(parts) + [tok(A_WIDTH), tok(A_WIDTH), tok(A_WIDTH, PA_G), tok(NA_WIDTH),
                                       _resident((1, A_WIDTH)), _resident((A_WIDTH + NA_WIDTH, D_MODEL)),
                                       _resident((1, D_MODEL)), _resident((D_MODEL, 2 * D_FF)),
                                       _resident((D_FF, D_MODEL)), _resident((1, D_MODEL))],
        out_specs=out_specs,
        out_shape=out_shapes,
        compiler_params=pltpu.CompilerParams(dimension_semantics=("arbitrary",),
                                             vmem_limit_bytes=VMEM_LIMIT),
        name="post_even",
    )(*parts, o_f, o_b, pa, o_n, gain, w_mix, g_ffn, w_ffn_in, w_ffn_out, g_final)


def _odd_body(final, out_starts, tbl_ref, x_ref, xp_ref, xn_ref, gm_ref, wc_ref, cw_ref, wco_ref, gf_ref,
              wi_ref, wo_ref, gfin_ref, *out_refs):
    step = pl.program_id(0)
    x = x_ref[...]
    gm = gm_ref[...]
    hs = [_rms(x[sl, :], gm).astype(BF16) for sl in SLABS]
    hh = _rms(jnp.concatenate([xp_ref[...], xn_ref[...]], axis=0), gm).astype(BF16)
    zin = [jnp.concatenate([hs[0], hh], axis=0)] + hs[1:]
    zs = [_dot(h, wc_ref[:, D_MODEL:2 * D_MODEL]) * _dot(h, wc_ref[:, 2 * D_MODEL:3 * D_MODEL]) for h in zin]
    zh = zs[0][TM // len(SLABS):, :]
    zs[0] = zs[0][:TM // len(SLABS), :]
    bgs = [_dot(h, wc_ref[:, 0:D_MODEL]) for h in hs]
    z_prev = jnp.where(tbl_ref[2, step] == 1, 0.0, zh[HALO - 1:HALO, :])
    z_next = jnp.where(tbl_ref[3, step] == 1, 0.0, zh[HALO:HALO + 1, :])
    rows = TM // len(SLABS)
    row = lax.broadcasted_iota(jnp.int32, (rows, 1), 0)
    x1s = []
    for k, (sl, z, bg) in enumerate(zip(SLABS, zs, bgs)):
        above = z_prev if k == 0 else zs[k - 1][rows - 1:rows, :]
        below = z_next if k == len(SLABS) - 1 else zs[k + 1][0:1, :]
        z_dn = jnp.where(row == 0, above, pltpu.roll(z, 1, 0))
        z_up = jnp.where(row == rows - 1, below, pltpu.roll(z, rows - 1, 0))
        conv = z_dn * cw_ref[0:1, :] + z * cw_ref[1:2, :] + z_up * cw_ref[2:3, :]
        x1s.append(x[sl, :] + _dot((bg * conv).astype(BF16), wco_ref[...]))
    accs = _ffn_tail(x1s, gf_ref, wi_ref, wo_ref, gfin_ref, final)
    _store_parts(out_starts, out_refs, accs)


def _odd_table(segs, t):
    rows = []
    starts = set()
    ends = set()
    for off, batch, seq in segs:
        for bi in range(batch):
            starts.add(off + bi * seq)
            ends.add(off + (bi + 1) * seq)
    for i in range(t // TM):
        t0 = i * TM
        first = int(t0 in starts)
        last = int(t0 + TM in ends)
        rows.append((max(t0 // HALO - 1, 0), min((t0 + TM) // HALO, t // HALO - 1), first, last))
    return np.asarray(rows, np.int32).T.copy()


def _odd(x, g_mix, w_conv_in, conv_w, w_conv_out, g_ffn, w_ffn_in, w_ffn_out, g_final, final, segs, out_tokens):
    t = x.shape[0]
    tbl = _odd_table(segs, t)
    out_starts, out_specs, out_shapes = _out_specs_shapes(out_tokens)
    res = lambda shape: pl.BlockSpec(shape, lambda i, tb: (0,) * len(shape), pipeline_mode=pl.Buffered(1))
    grid_spec = pltpu.PrefetchScalarGridSpec(
        num_scalar_prefetch=1,
        grid=(t // TM,),
        in_specs=[pl.BlockSpec((TM, D_MODEL), lambda i, tb: (i, 0)),
                  pl.BlockSpec((HALO, D_MODEL), lambda i, tb: (tb[0, i], 0)),
                  pl.BlockSpec((HALO, D_MODEL), lambda i, tb: (tb[1, i], 0)),
                  res((1, D_MODEL)), res((D_MODEL, 3 * D_MODEL)), res((CONV_W, D_MODEL)),
                  res((D_MODEL, D_MODEL)), res((1, D_MODEL)), res((D_MODEL, 2 * D_FF)),
                  res((D_FF, D_MODEL)), res((1, D_MODEL))],
        out_specs=out_specs,
    )
    return pl.pallas_call(
        functools.partial(_odd_body, final, out_starts),
        grid_spec=grid_spec,
        out_shape=out_shapes,
        compiler_params=pltpu.CompilerParams(dimension_semantics=("arbitrary",),
                                             vmem_limit_bytes=VMEM_LIMIT),
        name="odd_layer",
    )(jnp.asarray(tbl), x, x, x, g_mix, w_conv_in, conv_w, w_conv_out, g_ffn, w_ffn_in, w_ffn_out, g_final)


def _trunk(xs, norm_mix, norm_ffn, norm_final, even_w_in, even_w_out, hgrn_lb_logits, hgrn_norm,
           na_rpb, conv_w_in, conv_w, conv_w_out, ffn_w_in, ffn_w_out):
    depth = norm_mix.shape[0]
    segs = []
    off = 0
    for a in xs:
        segs.append((off, a.shape[0], a.shape[1]))
        off += a.shape[0] * a.shape[1]
    parts = [a.reshape(-1, D_MODEL) for a in xs]
    part_tokens = [p.shape[0] for p in parts]
    total = off
    row = lambda v: v.reshape(1, -1).astype(F32)
    g_final = row(norm_final)
    for l in range(depth):
        final = l == depth - 1
        out_tokens = part_tokens if final else [total]
        w_fi = ffn_w_in[l].astype(BF16)
        w_fo = ffn_w_out[l].astype(BF16)
        if l % 2 == 0:
            e = l // 2
            pa, pn = _in_even(parts, row(norm_mix[l]), even_w_in[e].astype(BF16))
            o_f, o_b = _hgrn(pa, hgrn_lb_logits.astype(F32), e, segs)
            o_n = _na(pn, na_rpb[e], segs)
            outs = _post_even(parts, o_f, o_b, pa, o_n, row(hgrn_norm[e]), even_w_out[e].astype(BF16),
                              row(norm_ffn[l]), w_fi, w_fo, g_final, final, out_tokens)
        else:
            o = l // 2
            x = parts[0] if len(parts) == 1 else jnp.concatenate(parts, axis=0)
            outs = _odd(x, row(norm_mix[l]), conv_w_in[o].astype(BF16), conv_w[o].astype(F32),
                        conv_w_out[o].astype(BF16), row(norm_ffn[l]), w_fi, w_fo, g_final, final, segs,
                        out_tokens)
        parts = list(outs)
    return tuple(p.reshape(a.shape) for p, a in zip(parts, xs))


def kernel(x_prompt, x_sample, norm_mix, norm_ffn, norm_final, even_w_in, even_w_out, hgrn_lb_logits,
           hgrn_norm, na_rpb, conv_w_in, conv_w, conv_w_out, ffn_w_in, ffn_w_out):
    return _trunk([x_prompt, x_sample], norm_mix, norm_ffn, norm_final, even_w_in, even_w_out,
                  hgrn_lb_logits, hgrn_norm, na_rpb, conv_w_in, conv_w, conv_w_out, ffn_w_in, ffn_w_out)
```

```python
import functools

import numpy as np
import jax
import jax.numpy as jnp
from jax import lax
from jax.experimental import pallas as pl
from jax.experimental.pallas import tpu as pltpu

F32 = jnp.float32
BF16 = jnp.bfloat16

SUBLANES = 8
D_MODEL = 1024
EPS = 1e-6
NEG = -1e30
LOG2E = 1.4426950408889634
GRID_W = 64
A_WIDTH = 512
A_HEADS = 4
A_DK = 128
CHUNK = 64
LEVELS = (2, 4, 8, 16, 32, 64)
NA_WIDTH = 512
NA_HEADS = 8
NA_HD = 64
NA_KR = 8
NA_KC = 16
NA_GROUP = 2
CONV_W = 3
D_FF = 2816
FF_CHUNK = 256
PA_W = 5 * A_WIDTH
PA_Q, PA_ZF, PA_ZB, PA_I, PA_G = range(5)
PN_W = 3 * NA_WIDTH
EVEN_IN = PA_W + PN_W

TM = 512
SLABS = (slice(0, TM // 2), slice(TM // 2, TM))
HGRN_BLOCK = 1024
NA_ROWS = 8
NA_WIN = 3 * NA_ROWS
HALO = 8
V7X_VMEM_BYTES = 64 * 1024 * 1024
VMEM_LIMIT = V7X_VMEM_BYTES * 7 // 8

NT_DIMS = (((1,), (1,)), ((), ()))
TN_DIMS = (((0,), (0,)), ((), ()))


def _resident(shape):
    nd = len(shape)
    return pl.BlockSpec(shape, lambda *_: (0,) * nd, pipeline_mode=pl.Buffered(1))


def _rms(x, g):
    return x * lax.rsqrt(jnp.mean(x * x, axis=-1, keepdims=True) + EPS) * g


def _sigmoid(x):
    return jax.nn.sigmoid(x)


def _dot(a, b):
    return jnp.dot(a, b, preferred_element_type=F32)


def _dot_nt(a, b):
    return lax.dot_general(a, b, NT_DIMS, preferred_element_type=F32)


def _tile_starts(parts):
    starts = [0]
    for a in parts:
        starts.append(starts[-1] + a.shape[0] // TM)
    return tuple(starts)


def _part_specs(parts):
    starts = _tile_starts(parts)
    return [pl.BlockSpec((TM, D_MODEL),
                         lambda i, *_, s=starts[k], n=starts[k + 1] - starts[k]: (jnp.clip(i - s, 0, n - 1), 0))
            for k in range(len(parts))]


def _select_part(starts, refs):
    step = pl.program_id(0)
    x = refs[-1][...]
    for k in range(len(refs) - 2, -1, -1):
        x = jnp.where(step < starts[k + 1], refs[k][...], x)
    return x


def _in_even_body(starts, *refs):
    n = len(starts) - 1
    g_ref, w_ref, pa_ref, pn_ref = refs[n:]
    h = _rms(_select_part(starts, refs[:n]), g_ref[...]).astype(BF16)
    for c in range(0, PA_W, A_WIDTH):
        pa_ref[:, c:c + A_WIDTH] = _dot(h, w_ref[:, c:c + A_WIDTH])
    for c in range(0, PN_W, NA_WIDTH):
        pn_ref[:, c:c + NA_WIDTH] = _dot(h, w_ref[:, PA_W + c:PA_W + c + NA_WIDTH]).astype(BF16)


def _in_even(parts, g, w):
    starts = _tile_starts(parts)
    t = starts[-1] * TM
    return pl.pallas_call(
        functools.partial(_in_even_body, starts),
        grid=(starts[-1],),
        in_specs=_part_specs(parts) + [_resident((1, D_MODEL)), _resident((D_MODEL, EVEN_IN))],
        out_specs=[pl.BlockSpec((TM, PA_W), lambda i: (i, 0)),
                   pl.BlockSpec((TM, PN_W), lambda i: (i, 0))],
        out_shape=[jax.ShapeDtypeStruct((t, PA_W), F32),
                   jax.ShapeDtypeStruct((t, PN_W), BF16)],
        compiler_params=pltpu.CompilerParams(dimension_semantics=("arbitrary",),
                                             vmem_limit_bytes=VMEM_LIMIT),
        name="in_even",
    )(*parts, g, w)


def _hgrn_masks():
    t = np.arange(CHUNK)[:, None]
    s = np.arange(CHUNK)[None, :]
    out = np.zeros((2, len(LEVELS) - 1, CHUNK, CHUNK), np.float32)
    for d, rev in enumerate((False, True)):
        for li, L in enumerate(LEVELS[1:]):
            half = L // 2
            same = (t // L) == (s // L)
            if rev:
                m = same & ((t % L) < half) & ((s % L) >= half)
            else:
                m = same & ((t % L) >= half) & ((s % L) < half)
            out[d, li] = m
    return np.tile(out, (1, 1, 1, 2))


def _hgrn_signs():
    t = np.arange(CHUNK)
    out = np.zeros((2, len(LEVELS), CHUNK), np.float32)
    for d, rev in enumerate((False, True)):
        for li, L in enumerate(LEVELS):
            second = (t % L) >= L // 2
            out[d, li] = np.where(second != rev, 1.0, -1.0)
    return np.ascontiguousarray(np.broadcast_to(out[..., None], out.shape + (A_DK,)))


def _tri(rev):
    t = lax.broadcasted_iota(jnp.int32, (CHUNK, CHUNK), 0)
    s = lax.broadcasted_iota(jnp.int32, (CHUNK, CHUNK), 1)
    return jnp.where((s >= t) if rev else (s <= t), 1.0, 0.0).astype(BF16)


def _level_ref(b3, L, rev):
    g, sub, w = b3.shape
    half = L // 2
    if L >= 2 * SUBLANES:
        n = L // SUBLANES
        pieces = []
        for blk in range(CHUNK // L):
            r = blk * L + (half if rev else half - 1)
            pieces.append(jnp.broadcast_to(b3[r // SUBLANES:r // SUBLANES + 1, r % SUBLANES:r % SUBLANES + 1, :],
                                           (n, sub, w)))
        return pieces[0] if len(pieces) == 1 else jnp.concatenate(pieces, axis=0)
    srow = lax.broadcasted_iota(jnp.int32, b3.shape, 1)
    if L == SUBLANES:
        r = half if rev else half - 1
        return jnp.broadcast_to(b3[:, r:r + 1, :], b3.shape)
    if L == 4:
        r = half if rev else half - 1
        lo = jnp.broadcast_to(b3[:, r:r + 1, :], b3.shape)
        hi = jnp.broadcast_to(b3[:, r + 4:r + 5, :], b3.shape)
        return jnp.where(srow < 4, lo, hi)
    odd = (srow % 2) == 1
    if rev:
        return jnp.where(odd, b3, pltpu.roll(b3, SUBLANES - 1, 1))
    return jnp.where(odd, pltpu.roll(b3, 1, 1), b3)


def _pair_blockdiag(x):
    zero = jnp.zeros((CHUNK, A_DK), x.dtype)
    return jnp.concatenate([jnp.concatenate([x[:, :A_DK], zero], axis=1),
                            jnp.concatenate([zero, x[:, A_DK:]], axis=1)], axis=0)


def _hgrn_chunk(q, k, v, b, state, d, rev, m_ref, sgn_ref):
    tot = b[0:1, :] if rev else b[CHUNK - 1:CHUNK, :]
    qb = q.astype(BF16)
    kb = k.astype(BF16)
    q0 = qb * jnp.exp2(b).astype(BF16)
    k2 = kb * jnp.exp2(tot - b).astype(BF16)
    vb = v.astype(BF16)
    dec = jnp.exp2(tot)
    inter = []
    for h in range(A_HEADS):
        sl = slice(h * A_DK, (h + 1) * A_DK)
        s_t = state[h]
        inter.append(_dot_nt(q0[:, sl], s_t.astype(BF16)))
        ds = lax.dot_general(vb[:, sl], k2[:, sl], TN_DIMS, preferred_element_type=F32)
        state[h] = dec[:, sl] * s_t + ds
    b3 = b.reshape(CHUNK // SUBLANES, SUBLANES, A_WIDTH)

    def level_decay(li):
        sgn = jnp.concatenate([sgn_ref[d, li]] * A_HEADS, axis=1).reshape(b3.shape)
        return jnp.exp2((b3 - _level_ref(b3, LEVELS[li], rev)) * sgn)

    shift = SUBLANES - 1 if rev else 1
    k_adj = pltpu.roll(k.reshape(b3.shape), shift, 1).reshape(CHUNK, A_WIDTH)
    v_adj = pltpu.roll(v.reshape(b3.shape), shift, 1).reshape(CHUNK, A_WIDTH)
    p_same = q * k
    p_adj = q * level_decay(0).reshape(CHUNK, A_WIDTH) * k_adj
    row = lax.broadcasted_iota(jnp.int32, (CHUNK, 1), 0)
    is_query = (row % 2 == 0) if rev else (row % 2 == 1)
    near = []
    for h in range(A_HEADS):
        sl = slice(h * A_DK, (h + 1) * A_DK)
        a_same = jnp.sum(p_same[:, sl], axis=-1, keepdims=True)
        a_adj = jnp.where(is_query, jnp.sum(p_adj[:, sl], axis=-1, keepdims=True), 0.0)
        near.append(a_same * v[:, sl] + a_adj * v_adj[:, sl])
    pairs = [slice(p * 2 * A_DK, (p + 1) * 2 * A_DK) for p in range(A_HEADS // 2)]
    acc = None
    for li in range(1, len(LEVELS)):
        e = level_decay(li).reshape(CHUNK, A_WIDTH).astype(BF16)
        qs = qb * e
        ks = kb * e
        sc = [_dot_nt(qs[:, p], _pair_blockdiag(ks[:, p])) * m_ref[d, li - 1] for p in pairs]
        acc = sc if acc is None else [a + s for a, s in zip(acc, sc)]
    intra = [_dot(a.astype(BF16), _pair_blockdiag(vb[:, p])) for a, p in zip(acc, pairs)]
    return jnp.concatenate(intra, axis=1) + jnp.concatenate(inter, axis=1) + jnp.concatenate(near, axis=1)


def _hgrn_body(layer, tbl_ref, qf_ref, zf_ref, vf_ref, qb_ref, zb_ref, vb_ref, lbl_ref, m_ref, sgn_ref,
               of_ref, ob_ref, s_ref):
    step = pl.program_id(0)

    @pl.when(tbl_ref[2, step] == 1)
    def _():
        s_ref[...] = jnp.zeros_like(s_ref)

    n_even = lbl_ref.shape[0]
    logits = [lbl_ref[i] for i in range(n_even)]
    mx = functools.reduce(jnp.maximum, logits)
    ex = [jnp.exp(l - mx) for l in logits]
    den = functools.reduce(lambda x, y: x + y, ex)
    ps = [e / den for e in ex]
    lbs = functools.reduce(lambda x, y: x + y, ps[:layer + 1]) - ps[0]

    nc = HGRN_BLOCK // CHUNK
    refs = ((qf_ref, zf_ref, vf_ref, of_ref), (qb_ref, zb_ref, vb_ref, ob_ref))
    tris = (_tri(False), _tri(True))
    states = [[s_ref[d, h] for h in range(A_HEADS)] for d in range(2)]

    for c in range(nc):
        prep = []
        for d, rev in enumerate((False, True)):
            q_ref, z_ref, v_ref, _ = refs[d]
            r0 = ((nc - 1 - c) if rev else c) * CHUNK
            rows = slice(r0, r0 + CHUNK)
            qa = q_ref[rows, :]
            z = z_ref[rows, :]
            lb = lbs[d:d + 1, :]
            q = qa * _sigmoid(qa)
            f = lb + (1.0 - lb) * _sigmoid(z)
            k = 1.0 - f
            lf = jnp.log(f)
            hi = lf.astype(BF16)
            r1 = lf - hi.astype(F32)
            mid = r1.astype(BF16)
            lo = (r1 - mid.astype(F32)).astype(BF16)
            cs = _dot(tris[d], jnp.concatenate([hi, mid, lo], axis=1))
            b = (cs[:, :A_WIDTH] + cs[:, A_WIDTH:2 * A_WIDTH] + cs[:, 2 * A_WIDTH:]) * LOG2E
            prep.append((rows, q, k, v_ref[rows, :], b))
        for d, rev in enumerate((False, True)):
            rows, q, k, v, b = prep[d]
            refs[d][3][rows, :] = _hgrn_chunk(q, k, v, b, states[d], d, rev, m_ref, sgn_ref)
    for d in range(2):
        for h in range(A_HEADS):
            s_ref[d, h] = states[d][h]


def _hgrn_table(segs):
    rows = []
    for off, batch, seq in segs:
        assert seq % HGRN_BLOCK == 0 and off % HGRN_BLOCK == 0
        nblk = seq // HGRN_BLOCK
        for bi in range(batch):
            base = (off + bi * seq) // HGRN_BLOCK
            for i in range(nblk):
                rows.append((base + i, base + nblk - 1 - i, int(i == 0)))
    return np.asarray(rows, np.int32).T.copy()


def _hgrn(pa, lb_logits, layer, segs):
    t = pa.shape[0]
    tbl = _hgrn_table(segs)
    steps = tbl.shape[1]
    masks = _hgrn_masks()
    signs = _hgrn_signs()

    def spec(col, which):
        return pl.BlockSpec((HGRN_BLOCK, A_WIDTH), lambda i, tb: (tb[which, i], col))

    grid_spec = pltpu.PrefetchScalarGridSpec(
        num_scalar_prefetch=1,
        grid=(steps,),
        in_specs=[spec(PA_Q, 0), spec(PA_ZF, 0), spec(PA_I, 0), spec(PA_Q, 1), spec(PA_ZB, 1), spec(PA_I, 1),
                  pl.BlockSpec(lb_logits.shape, lambda i, tb: (0, 0, 0)),
                  pl.BlockSpec(masks.shape, lambda i, tb: (0, 0, 0, 0)),
                  pl.BlockSpec(signs.shape, lambda i, tb: (0, 0, 0, 0))],
        out_specs=[pl.BlockSpec((HGRN_BLOCK, A_WIDTH), lambda i, tb: (tb[0, i], 0)),
                   pl.BlockSpec((HGRN_BLOCK, A_WIDTH), lambda i, tb: (tb[1, i], 0))],
        scratch_shapes=[pltpu.VMEM((2, A_HEADS, A_DK, A_DK), F32)],
    )
    return pl.pallas_call(
        functools.partial(_hgrn_body, layer),
        grid_spec=grid_spec,
        out_shape=[jax.ShapeDtypeStruct((t, A_WIDTH), F32),
                   jax.ShapeDtypeStruct((t, A_WIDTH), F32)],
        compiler_params=pltpu.CompilerParams(dimension_semantics=("arbitrary",),
                                             vmem_limit_bytes=VMEM_LIMIT),
        name="hgrn",
    )(jnp.asarray(tbl), pa, pa, pa, pa, pa, pa, lb_logits, jnp.asarray(masks), jnp.asarray(signs))


def _na_build_bias(rpb_ref, bias_ref):
    n_ro = 2 * NA_KR - 1
    n_co = 2 * NA_KC - 1
    qc = lax.broadcasted_iota(jnp.int32, (GRID_W, GRID_W), 0)
    kc = lax.broadcasted_iota(jnp.int32, (GRID_W, GRID_W), 1)
    co = jnp.clip(kc - qc, -(NA_KC - 1), NA_KC - 1) + NA_KC - 1
    ws = jnp.clip(qc - NA_KC // 2, 0, GRID_W - NA_KC)
    valid = (kc >= ws) & (kc < ws + NA_KC)

    def build(hr, carry):
        h = hr // n_ro
        ro = hr % n_ro
        t = jnp.full((GRID_W, GRID_W), NEG, F32)
        for j in range(n_co):
            t = jnp.where(co == j, rpb_ref[hr * n_co + j], t)
        t = jnp.where(valid, t, NEG)
        for i in range(NA_KR):
            dd = ro - i

            @pl.when((dd >= 0) & (dd < NA_KR))
            def _():
                bias_ref[h, dd, :, i * GRID_W:(i + 1) * GRID_W] = t
        return carry

    lax.fori_loop(0, NA_HEADS * n_ro, build, 0)


def _na_body(tbl_ref, rpb_ref, q_ref, k_ref, v_ref, o_ref, bias_ref):
    step = pl.program_id(0)

    @pl.when(step == 0)
    def _():
        _na_build_bias(rpb_ref, bias_ref)

    r0 = tbl_ref[2, step]
    w0 = tbl_ref[3, step]
    n_rows = tbl_ref[4, step]
    head = lax.broadcasted_iota(jnp.int32, (GRID_W, NA_GROUP * NA_HD), 1) // NA_HD
    nkeys = NA_KR * GRID_W
    scale = jnp.asarray(NA_HD ** -0.5, BF16)

    groups = [slice(g * NA_GROUP * NA_HD, (g + 1) * NA_GROUP * NA_HD) for g in range(NA_HEADS // NA_GROUP)]

    units = []
    for rr in range(NA_ROWS):
        r = r0 + rr
        row_start = jnp.clip(r - NA_KR // 2, 0, n_rows - NA_KR)
        dd = row_start - r + (NA_KR - 1)
        keys = pl.ds(pl.multiple_of((row_start - w0) * GRID_W, GRID_W), nkeys)
        qrows = slice(rr * GRID_W, (rr + 1) * GRID_W)
        units += [(dd, keys, qrows, g, lanes) for g, lanes in enumerate(groups)]
    scores = []
    for dd, keys, qrows, g, lanes in units:
        qp = q_ref[qrows, lanes] * scale
        zero = jnp.zeros_like(qp)
        qs = jnp.concatenate([jnp.where(head == h, qp, zero) for h in range(NA_GROUP)], axis=0)
        scores.append(_dot_nt(qs, k_ref[keys, lanes]))
    probs = []
    for (dd, keys, qrows, g, lanes), s in zip(units, scores):
        s = s + jnp.concatenate([bias_ref[NA_GROUP * g + h, dd] for h in range(NA_GROUP)], axis=0)
        m = jnp.max(s, axis=-1, keepdims=True)
        p = jnp.exp(s - m)
        probs.append((p.astype(BF16), jnp.sum(p, axis=-1, keepdims=True)))
    for (dd, keys, qrows, g, lanes), (p, l) in zip(units, probs):
        pv = _dot(p, v_ref[keys, lanes]) / l
        o = pv[:GRID_W]
        for h in range(1, NA_GROUP):
            o = jnp.where(head == h, pv[h * GRID_W:(h + 1) * GRID_W], o)
        o_ref[qrows, lanes] = o.astype(o_ref.dtype)


def _na_table(segs):
    blk = NA_ROWS * GRID_W
    rows = []
    for off, batch, seq in segs:
        n_rows = seq // GRID_W
        assert n_rows >= NA_WIN and seq % blk == 0 and off % blk == 0
        for bi in range(batch):
            base = off + bi * seq
            for i in range(seq // blk):
                r0 = i * NA_ROWS
                w0 = min(max(r0 - NA_ROWS, 0), n_rows - NA_WIN)
                rows.append((base // blk + i, base // GRID_W + w0, r0, w0, n_rows))
    return np.asarray(rows, np.int32).T.copy()


def _na(pn, rpb, segs):
    t = pn.shape[0]
    tbl = _na_table(segs)
    steps = tbl.shape[1]
    blk = NA_ROWS * GRID_W

    def window(col):
        return pl.BlockSpec((pl.Element(NA_WIN * GRID_W), pl.Element(NA_WIDTH)),
                            lambda i, tb: (tb[1, i] * GRID_W, col * NA_WIDTH))

    grid_spec = pltpu.PrefetchScalarGridSpec(
        num_scalar_prefetch=1,
        grid=(steps,),
        in_specs=[pl.BlockSpec(memory_space=pltpu.SMEM),
                  pl.BlockSpec((blk, NA_WIDTH), lambda i, tb: (tb[0, i], 0)),
                  window(1), window(2)],
        out_specs=pl.BlockSpec((blk, NA_WIDTH), lambda i, tb: (tb[0, i], 0)),
        scratch_shapes=[pltpu.VMEM((NA_HEADS, NA_KR, GRID_W, NA_KR * GRID_W), F32)],
    )
    return pl.pallas_call(
        _na_body,
        grid_spec=grid_spec,
        out_shape=jax.ShapeDtypeStruct((t, NA_WIDTH), BF16),
        compiler_params=pltpu.CompilerParams(dimension_semantics=("arbitrary",),
                                             vmem_limit_bytes=VMEM_LIMIT),
        name="natten",
    )(jnp.asarray(tbl), rpb.astype(F32).reshape(-1), pn, pn, pn)


def _out_specs_shapes(out_tokens):
    starts = [0]
    for t in out_tokens:
        starts.append(starts[-1] + t // TM)
    specs = []
    for k in range(len(out_tokens)):
        s, n = starts[k], starts[k + 1] - starts[k]
        specs.append(pl.BlockSpec((TM, D_MODEL), lambda i, *_, s=s, n=n: (jnp.clip(i - s, 0, n - 1), 0)))
    shapes = [jax.ShapeDtypeStruct((t, D_MODEL), F32) for t in out_tokens]
    return tuple(starts), specs, shapes


def _store_parts(starts, out_refs, accs):
    def store(o_ref):
        for sl, acc in zip(SLABS, accs):
            o_ref[sl, :] = acc

    if len(out_refs) == 1:
        store(out_refs[0])
        return
    step = pl.program_id(0)
    for k, o_ref in enumerate(out_refs):
        @pl.when((step >= starts[k]) & (step < starts[k + 1]))
        def _(o_ref=o_ref):
            store(o_ref)


def _ffn_tail(x1s, gf_ref, wi_ref, wo_ref, gfin_ref, final):
    hs = [_rms(x1, gf_ref[...]).astype(BF16) for x1 in x1s]
    accs = list(x1s)
    for c in range(0, D_FF, FF_CHUNK):
        gus = [(_dot(h, wi_ref[:, c:c + FF_CHUNK]), _dot(h, wi_ref[:, D_FF + c:D_FF + c + FF_CHUNK])) for h in hs]
        acts = [(g * _sigmoid(g) * u).astype(BF16) for g, u in gus]
        accs = [acc + _dot(a, wo_ref[c:c + FF_CHUNK, :]) for acc, a in zip(accs, acts)]
    if final:
        accs = [_rms(acc, gfin_ref[...]) for acc in accs]
    return accs


def _post_even_body(final, starts, out_starts, *refs):
    n = len(starts) - 1
    of_ref, ob_ref, ga_ref, n_ref, gain_ref, wm_ref, gf_ref, wi_ref, wo_ref, gfin_ref = refs[n:n + 10]
    x = _select_part(starts, refs[:n])
    x1s = []
    for sl in SLABS:
        o = of_ref[sl, :] + ob_ref[sl, :]
        parts = []
        for h in range(A_HEADS):
            oh = o[:, h * A_DK:(h + 1) * A_DK]
            parts.append(oh * lax.rsqrt(jnp.mean(oh * oh, axis=-1, keepdims=True) + EPS))
        on = jnp.concatenate(parts, axis=1) * gain_ref[...]
        g = ga_ref[sl, :]
        oa = (on * (g * _sigmoid(g))).astype(BF16)
        mix = _dot(oa, wm_ref[0:A_WIDTH, :]) + _dot(n_ref[sl, :], wm_ref[A_WIDTH:A_WIDTH + NA_WIDTH, :])
        x1s.append(x[sl, :] + mix)
    accs = _ffn_tail(x1s, gf_ref, wi_ref, wo_ref, gfin_ref, final)
    _store_parts(out_starts, refs[n + 10:], accs)


def _post_even(parts, o_f, o_b, pa, o_n, gain, w_mix, g_ffn, w_ffn_in, w_ffn_out, g_final, final, out_tokens):
    starts = _tile_starts(parts)
    out_starts, out_specs, out_shapes = _out_specs_shapes(out_tokens)
    tok = lambda w, col=0: pl.BlockSpec((TM, w), lambda i: (i, col))
    return pl.pallas_call(
        functools.partial(_post_even_body, final, starts, out_starts),
        grid=(starts[-1],),
        in_specs=_part_specs(parts) + [tok(A_WIDTH), tok(A_WIDTH), tok(A_WIDTH, PA_G), tok(NA_WIDTH),
                                       _resident((1, A_WIDTH)), _resident((A_WIDTH + NA_WIDTH, D_MODEL)),
                                       _resident((1, D_MODEL)), _resident((D_MODEL, 2 * D_FF)),
                                       _resident((D_FF, D_MODEL)), _resident((1, D_MODEL))],
        out_specs=out_specs,
        out_shape=out_shapes,
        compiler_params=pltpu.CompilerParams(dimension_semantics=("arbitrary",),
                                             vmem_limit_bytes=VMEM_LIMIT),
        name="post_even",
    )(*parts, o_f, o_b, pa, o_n, gain, w_mix, g_ffn, w_ffn_in, w_ffn_out, g_final)


def _odd_body(final, out_starts, tbl_ref, x_ref, xp_ref, xn_ref, gm_ref, wc_ref, cw_ref, wco_ref, gf_ref,
              wi_ref, wo_ref, gfin_ref, *out_refs):
    step = pl.program_id(0)
    x = x_ref[...]
    gm = gm_ref[...]
    hs = [_rms(x[sl, :], gm).astype(BF16) for sl in SLABS]
    hh = _rms(jnp.concatenate([xp_ref[...], xn_ref[...]], axis=0), gm).astype(BF16)
    zin = [jnp.concatenate([hs[0], hh], axis=0)] + hs[1:]
    zs = [_dot(h, wc_ref[:, D_MODEL:2 * D_MODEL]) * _dot(h, wc_ref[:, 2 * D_MODEL:3 * D_MODEL]) for h in zin]
    zh = zs[0][TM // len(SLABS):, :]
    zs[0] = zs[0][:TM // len(SLABS), :]
    bgs = [_dot(h, wc_ref[:, 0:D_MODEL]) for h in hs]
    z_prev = jnp.where(tbl_ref[2, step] == 1, 0.0, zh[HALO - 1:HALO, :])
    z_next = jnp.where(tbl_ref[3, step] == 1, 0.0, zh[HALO:HALO + 1, :])
    rows = TM // len(SLABS)
    row = lax.broadcasted_iota(jnp.int32, (rows, 1), 0)
    x1s = []
    for k, (sl, z, bg) in enumerate(zip(SLABS, zs, bgs)):
        above = z_prev if k == 0 else zs[k - 1][rows - 1:rows, :]
        below = z_next if k == len(SLABS) - 1 else zs[k + 1][0:1, :]
        z_dn = jnp.where(row == 0, above, pltpu.roll(z, 1, 0))
        z_up = jnp.where(row == rows - 1, below, pltpu.roll(z, rows - 1, 0))
        conv = z_dn * cw_ref[0:1, :] + z * cw_ref[1:2, :] + z_up * cw_ref[2:3, :]
        x1s.append(x[sl, :] + _dot((bg * conv).astype(BF16), wco_ref[...]))
    accs = _ffn_tail(x1s, gf_ref, wi_ref, wo_ref, gfin_ref, final)
    _store_parts(out_starts, out_refs, accs)


def _odd_table(segs, t):
    rows = []
    starts = set()
    ends = set()
    for off, batch, seq in segs:
        for bi in range(batch):
            starts.add(off + bi * seq)
            ends.add(off + (bi + 1) * seq)
    for i in range(t // TM):
        t0 = i * TM
        first = int(t0 in starts)
        last = int(t0 + TM in ends)
        rows.append((max(t0 // HALO - 1, 0), min((t0 + TM) // HALO, t // HALO - 1), first, last))
    return np.asarray(rows, np.int32).T.copy()


def _odd(x, g_mix, w_conv_in, conv_w, w_conv_out, g_ffn, w_ffn_in, w_ffn_out, g_final, final, segs, out_tokens):
    t = x.shape[0]
    tbl = _odd_table(segs, t)
    out_starts, out_specs, out_shapes = _out_specs_shapes(out_tokens)
    res = lambda shape: pl.BlockSpec(shape, lambda i, tb: (0,) * len(shape), pipeline_mode=pl.Buffered(1))
    grid_spec = pltpu.PrefetchScalarGridSpec(
        num_scalar_prefetch=1,
        grid=(t // TM,),
        in_specs=[pl.BlockSpec((TM, D_MODEL), lambda i, tb: (i, 0)),
                  pl.BlockSpec((HALO, D_MODEL), lambda i, tb: (tb[0, i], 0)),
                  pl.BlockSpec((HALO, D_MODEL), lambda i, tb: (tb[1, i], 0)),
                  res((1, D_MODEL)), res((D_MODEL, 3 * D_MODEL)), res((CONV_W, D_MODEL)),
                  res((D_MODEL, D_MODEL)), res((1, D_MODEL)), res((D_MODEL, 2 * D_FF)),
                  res((D_FF, D_MODEL)), res((1, D_MODEL))],
        out_specs=out_specs,
    )
    return pl.pallas_call(
        functools.partial(_odd_body, final, out_starts),
        grid_spec=grid_spec,
        out_shape=out_shapes,
        compiler_params=pltpu.CompilerParams(dimension_semantics=("arbitrary",),
                                             vmem_limit_bytes=VMEM_LIMIT),
        name="odd_layer",
    )(jnp.asarray(tbl), x, x, x, g_mix, w_conv_in, conv_w, w_conv_out, g_ffn, w_ffn_in, w_ffn_out, g_final)


def _trunk(xs, norm_mix, norm_ffn, norm_final, even_w_in, even_w_out, hgrn_lb_logits, hgrn_norm,
           na_rpb, conv_w_in, conv_w, conv_w_out, ffn_w_in, ffn_w_out):
    depth = norm_mix.shape[0]
    segs = []
    off = 0
    for a in xs:
        segs.append((off, a.shape[0], a.shape[1]))
        off += a.shape[0] * a.shape[1]
    parts = [a.reshape(-1, D_MODEL) for a in xs]
    part_tokens = [p.shape[0] for p in parts]
    total = off
    row = lambda v: v.reshape(1, -1).astype(F32)
    g_final = row(norm_final)
    for l in range(depth):
        final = l == depth - 1
        out_tokens = part_tokens if final else [total]
        w_fi = ffn_w_in[l].astype(BF16)
        w_fo = ffn_w_out[l].astype(BF16)
        if l % 2 == 0:
            e = l // 2
            pa, pn = _in_even(parts, row(norm_mix[l]), even_w_in[e].astype(BF16))
            o_f, o_b = _hgrn(pa, hgrn_lb_logits.astype(F32), e, segs)
            o_n = _na(pn, na_rpb[e], segs)
            outs = _post_even(parts, o_f, o_b, pa, o_n, row(hgrn_norm[e]), even_w_out[e].astype(BF16),
                              row(norm_ffn[l]), w_fi, w_fo, g_final, final, out_tokens)
        else:
            o = l // 2
            x = parts[0] if len(parts) == 1 else jnp.concatenate(parts, axis=0)
            outs = _odd(x, row(norm_mix[l]), conv_w_in[o].astype(BF16), conv_w[o].astype(F32),
                        conv_w_out[o].astype(BF16), row(norm_ffn[l]), w_fi, w_fo, g_final, final, segs,
                        out_tokens)
        parts = list(outs)
    return tuple(p.reshape(a.shape) for p, a in zip(parts, xs))


def kernel(x_prompt, x_sample, norm_mix, norm_ffn, norm_final, even_w_in, even_w_out, hgrn_lb_logits,
           hgrn_norm, na_rpb, conv_w_in, conv_w, conv_w_out, ffn_w_in, ffn_w_out):
    return _trunk([x_prompt, x_sample], norm_mix, norm_ffn, norm_final, even_w_in, even_w_out,
                  hgrn_lb_logits, hgrn_norm, na_rpb, conv_w_in, conv_w, conv_w_out, ffn_w_in, ffn_w_out)
```

```python
import functools

import numpy as np
import jax
import jax.numpy as jnp
from jax import lax
from jax.experimental import pallas as pl
from jax.experimental.pallas import tpu as pltpu

F32 = jnp.float32
BF16 = jnp.bfloat16

SUBLANES = 8
D_MODEL = 1024
EPS = 1e-6
NEG = -1e30
LOG2E = 1.4426950408889634
GRID_W = 64
A_WIDTH = 512
A_HEADS = 4
A_DK = 128
CHUNK = 64
LEVELS = (2, 4, 8, 16, 32, 64)
NA_WIDTH = 512
NA_HEADS = 8
NA_HD = 64
NA_KR = 8
NA_KC = 16
NA_GROUP = 2
CONV_W = 3
D_FF = 2816
FF_CHUNK = 256
PA_W = 5 * A_WIDTH
PA_Q, PA_ZF, PA_ZB, PA_I, PA_G = range(5)
PN_W = 3 * NA_WIDTH
EVEN_IN = PA_W + PN_W

TM = 512
SLABS = (slice(0, TM // 2), slice(TM // 2, TM))
HGRN_BLOCK = 1024
NA_ROWS = 8
NA_WIN = 3 * NA_ROWS
HALO = 8
V7X_VMEM_BYTES = 64 * 1024 * 1024
VMEM_LIMIT = V7X_VMEM_BYTES * 7 // 8

NT_DIMS = (((1,), (1,)), ((), ()))
TN_DIMS = (((0,), (0,)), ((), ()))


def _resident(shape):
    nd = len(shape)
    return pl.BlockSpec(shape, lambda *_: (0,) * nd, pipeline_mode=pl.Buffered(1))


def _layer_of(stack, layer):
    nd = stack.ndim - 1
    return pl.BlockSpec((None,) + stack.shape[1:], lambda *_: (layer,) + (0,) * nd, pipeline_mode=pl.Buffered(1))


def _rms(x, g):
    return x * lax.rsqrt(jnp.mean(x * x, axis=-1, keepdims=True) + EPS) * g


def _sigmoid(x):
    return jax.nn.sigmoid(x)


def _dot(a, b):
    return jnp.dot(a, b, preferred_element_type=F32)


def _dot_nt(a, b):
    return lax.dot_general(a, b, NT_DIMS, preferred_element_type=F32)


def _tile_starts(parts):
    starts = [0]
    for a in parts:
        starts.append(starts[-1] + a.shape[0] // TM)
    return tuple(starts)


def _part_specs(parts):
    starts = _tile_starts(parts)
    return [pl.BlockSpec((TM, D_MODEL),
                         lambda i, *_, s=starts[k], n=starts[k + 1] - starts[k]: (jnp.clip(i - s, 0, n - 1), 0))
            for k in range(len(parts))]


def _select_part(starts, refs):
    step = pl.program_id(0)
    x = refs[-1][...]
    for k in range(len(refs) - 2, -1, -1):
        x = jnp.where(step < starts[k + 1], refs[k][...], x)
    return x


def _in_even_body(starts, *refs):
    n = len(starts) - 1
    g_ref, w_ref, pa_ref, pn_ref = refs[n:]
    h = _rms(_select_part(starts, refs[:n]), g_ref[...]).astype(BF16)
    for c in range(0, PA_W, A_WIDTH):
        pa_ref[:, c:c + A_WIDTH] = _dot(h, w_ref[:, c:c + A_WIDTH])
    for c in range(0, PN_W, NA_WIDTH):
        pn_ref[:, c:c + NA_WIDTH] = _dot(h, w_ref[:, PA_W + c:PA_W + c + NA_WIDTH]).astype(BF16)


def _in_even(parts, g, w, layer):
    starts = _tile_starts(parts)
    t = starts[-1] * TM
    return pl.pallas_call(
        functools.partial(_in_even_body, starts),
        grid=(starts[-1],),
        in_specs=_part_specs(parts) + [_resident((1, D_MODEL)), _layer_of(w, layer)],
        out_specs=[pl.BlockSpec((TM, PA_W), lambda i: (i, 0)),
                   pl.BlockSpec((TM, PN_W), lambda i: (i, 0))],
        out_shape=[jax.ShapeDtypeStruct((t, PA_W), F32),
                   jax.ShapeDtypeStruct((t, PN_W), BF16)],
        compiler_params=pltpu.CompilerParams(dimension_semantics=("arbitrary",),
                                             vmem_limit_bytes=VMEM_LIMIT),
        name="in_even",
    )(*parts, g, w)


def _hgrn_masks():
    t = np.arange(CHUNK)[:, None]
    s = np.arange(CHUNK)[None, :]
    out = np.zeros((2, len(LEVELS) - 1, CHUNK, CHUNK), np.float32)
    for d, rev in enumerate((False, True)):
        for li, L in enumerate(LEVELS[1:]):
            half = L // 2
            same = (t // L) == (s // L)
            if rev:
                m = same & ((t % L) < half) & ((s % L) >= half)
            else:
                m = same & ((t % L) >= half) & ((s % L) < half)
            out[d, li] = m
    return np.tile(out, (1, 1, 1, 2))


def _hgrn_signs():
    t = np.arange(CHUNK)
    out = np.zeros((2, len(LEVELS), CHUNK), np.float32)
    for d, rev in enumerate((False, True)):
        for li, L in enumerate(LEVELS):
            second = (t % L) >= L // 2
            out[d, li] = np.where(second != rev, 1.0, -1.0)
    return np.ascontiguousarray(np.broadcast_to(out[..., None], out.shape + (A_DK,)))


def _tri(rev):
    t = lax.broadcasted_iota(jnp.int32, (CHUNK, CHUNK), 0)
    s = lax.broadcasted_iota(jnp.int32, (CHUNK, CHUNK), 1)
    return jnp.where((s >= t) if rev else (s <= t), 1.0, 0.0).astype(BF16)


def _level_ref(b3, L, rev):
    g, sub, w = b3.shape
    half = L // 2
    if L >= 2 * SUBLANES:
        n = L // SUBLANES
        pieces = []
        for blk in range(CHUNK // L):
            r = blk * L + (half if rev else half - 1)
            pieces.append(jnp.broadcast_to(b3[r // SUBLANES:r // SUBLANES + 1, r % SUBLANES:r % SUBLANES + 1, :],
                                           (n, sub, w)))
        return pieces[0] if len(pieces) == 1 else jnp.concatenate(pieces, axis=0)
    srow = lax.broadcasted_iota(jnp.int32, b3.shape, 1)
    if L == SUBLANES:
        r = half if rev else half - 1
        return jnp.broadcast_to(b3[:, r:r + 1, :], b3.shape)
    if L == 4:
        r = half if rev else half - 1
        lo = jnp.broadcast_to(b3[:, r:r + 1, :], b3.shape)
        hi = jnp.broadcast_to(b3[:, r + 4:r + 5, :], b3.shape)
        return jnp.where(srow < 4, lo, hi)
    odd = (srow % 2) == 1
    if rev:
        return jnp.where(odd, b3, pltpu.roll(b3, SUBLANES - 1, 1))
    return jnp.where(odd, pltpu.roll(b3, 1, 1), b3)


def _pair_blockdiag(x):
    zero = jnp.zeros((CHUNK, A_DK), x.dtype)
    return jnp.concatenate([jnp.concatenate([x[:, :A_DK], zero], axis=1),
                            jnp.concatenate([zero, x[:, A_DK:]], axis=1)], axis=0)


def _hgrn_chunk(q, k, v, b, state, d, rev, m_ref, sgn_ref):
    tot = b[0:1, :] if rev else b[CHUNK - 1:CHUNK, :]
    qb = q.astype(BF16)
    kb = k.astype(BF16)
    q0 = qb * jnp.exp2(b).astype(BF16)
    k2 = kb * jnp.exp2(tot - b).astype(BF16)
    vb = v.astype(BF16)
    dec = jnp.exp2(tot)
    inter = []
    for h in range(A_HEADS):
        sl = slice(h * A_DK, (h + 1) * A_DK)
        s_t = state[h]
        inter.append(_dot_nt(q0[:, sl], s_t.astype(BF16)))
        ds = lax.dot_general(vb[:, sl], k2[:, sl], TN_DIMS, preferred_element_type=F32)
        state[h] = dec[:, sl] * s_t + ds
    b3 = b.reshape(CHUNK // SUBLANES, SUBLANES, A_WIDTH)

    def level_decay(li):
        sgn = jnp.concatenate([sgn_ref[d, li]] * A_HEADS, axis=1).reshape(b3.shape)
        return jnp.exp2((b3 - _level_ref(b3, LEVELS[li], rev)) * sgn)

    shift = SUBLANES - 1 if rev else 1
    k_adj = pltpu.roll(k.reshape(b3.shape), shift, 1).reshape(CHUNK, A_WIDTH)
    v_adj = pltpu.roll(v.reshape(b3.shape), shift, 1).reshape(CHUNK, A_WIDTH)
    p_same = q * k
    p_adj = q * level_decay(0).reshape(CHUNK, A_WIDTH) * k_adj
    row = lax.broadcasted_iota(jnp.int32, (CHUNK, 1), 0)
    is_query = (row % 2 == 0) if rev else (row % 2 == 1)
    near = []
    for h in range(A_HEADS):
        sl = slice(h * A_DK, (h + 1) * A_DK)
        a_same = jnp.sum(p_same[:, sl], axis=-1, keepdims=True)
        a_adj = jnp.where(is_query, jnp.sum(p_adj[:, sl], axis=-1, keepdims=True), 0.0)
        near.append(a_same * v[:, sl] + a_adj * v_adj[:, sl])
    pairs = [slice(p * 2 * A_DK, (p + 1) * 2 * A_DK) for p in range(A_HEADS // 2)]
    acc = None
    for li in range(1, len(LEVELS)):
        e = level_decay(li).reshape(CHUNK, A_WIDTH).astype(BF16)
        qs = qb * e
        ks = kb * e
        sc = [_dot_nt(qs[:, p], _pair_blockdiag(ks[:, p])) * m_ref[d, li - 1] for p in pairs]
        acc = sc if acc is None else [a + s for a, s in zip(acc, sc)]
    intra = [_dot(a.astype(BF16), _pair_blockdiag(vb[:, p])) for a, p in zip(acc, pairs)]
    return jnp.concatenate(intra, axis=1) + jnp.concatenate(inter, axis=1) + jnp.concatenate(near, axis=1)


def _hgrn_body(layer, tbl_ref, qf_ref, zf_ref, vf_ref, qb_ref, zb_ref, vb_ref, lbl_ref, m_ref, sgn_ref,
               of_ref, ob_ref, s_ref):
    step = pl.program_id(0)

    @pl.when(tbl_ref[2, step] == 1)
    def _():
        s_ref[...] = jnp.zeros_like(s_ref)

    n_even = lbl_ref.shape[0]
    logits = [lbl_ref[i] for i in range(n_even)]
    mx = functools.reduce(jnp.maximum, logits)
    ex = [jnp.exp(l - mx) for l in logits]
    den = functools.reduce(lambda x, y: x + y, ex)
    ps = [e / den for e in ex]
    lbs = functools.reduce(lambda x, y: x + y, ps[:layer + 1]) - ps[0]

    nc = HGRN_BLOCK // CHUNK
    refs = ((qf_ref, zf_ref, vf_ref, of_ref), (qb_ref, zb_ref, vb_ref, ob_ref))
    tris = (_tri(False), _tri(True))
    states = [[s_ref[d, h] for h in range(A_HEADS)] for d in range(2)]

    for c in range(nc):
        prep = []
        for d, rev in enumerate((False, True)):
            q_ref, z_ref, v_ref, _ = refs[d]
            r0 = ((nc - 1 - c) if rev else c) * CHUNK
            rows = slice(r0, r0 + CHUNK)
            qa = q_ref[rows, :]
            z = z_ref[rows, :]
            lb = lbs[d:d + 1, :]
            q = qa * _sigmoid(qa)
            f = lb + (1.0 - lb) * _sigmoid(z)
            k = 1.0 - f
            lf = jnp.log(f)
            hi = lf.astype(BF16)
            r1 = lf - hi.astype(F32)
            mid = r1.astype(BF16)
            lo = (r1 - mid.astype(F32)).astype(BF16)
            cs = _dot(tris[d], jnp.concatenate([hi, mid, lo], axis=1))
            b = (cs[:, :A_WIDTH] + cs[:, A_WIDTH:2 * A_WIDTH] + cs[:, 2 * A_WIDTH:]) * LOG2E
            prep.append((rows, q, k, v_ref[rows, :], b))
        for d, rev in enumerate((False, True)):
            rows, q, k, v, b = prep[d]
            refs[d][3][rows, :] = _hgrn_chunk(q, k, v, b, states[d], d, rev, m_ref, sgn_ref)
    for d in range(2):
        for h in range(A_HEADS):
            s_ref[d, h] = states[d][h]


def _hgrn_table(segs):
    rows = []
    for off, batch, seq in segs:
        assert seq % HGRN_BLOCK == 0 and off % HGRN_BLOCK == 0
        nblk = seq // HGRN_BLOCK
        for bi in range(batch):
            base = (off + bi * seq) // HGRN_BLOCK
            for i in range(nblk):
                rows.append((base + i, base + nblk - 1 - i, int(i == 0)))
    return np.asarray(rows, np.int32).T.copy()


def _hgrn(pa, lb_logits, layer, segs):
    t = pa.shape[0]
    tbl = _hgrn_table(segs)
    steps = tbl.shape[1]
    masks = _hgrn_masks()
    signs = _hgrn_signs()

    def spec(col, which):
        return pl.BlockSpec((HGRN_BLOCK, A_WIDTH), lambda i, tb: (tb[which, i], col))

    grid_spec = pltpu.PrefetchScalarGridSpec(
        num_scalar_prefetch=1,
        grid=(steps,),
        in_specs=[spec(PA_Q, 0), spec(PA_ZF, 0), spec(PA_I, 0), spec(PA_Q, 1), spec(PA_ZB, 1), spec(PA_I, 1),
                  pl.BlockSpec(lb_logits.shape, lambda i, tb: (0, 0, 0)),
                  pl.BlockSpec(masks.shape, lambda i, tb: (0, 0, 0, 0)),
                  pl.BlockSpec(signs.shape, lambda i, tb: (0, 0, 0, 0))],
        out_specs=[pl.BlockSpec((HGRN_BLOCK, A_WIDTH), lambda i, tb: (tb[0, i], 0)),
                   pl.BlockSpec((HGRN_BLOCK, A_WIDTH), lambda i, tb: (tb[1, i], 0))],
        scratch_shapes=[pltpu.VMEM((2, A_HEADS, A_DK, A_DK), F32)],
    )
    return pl.pallas_call(
        functools.partial(_hgrn_body, layer),
        grid_spec=grid_spec,
        out_shape=[jax.ShapeDtypeStruct((t, A_WIDTH), F32),
                   jax.ShapeDtypeStruct((t, A_WIDTH), F32)],
        compiler_params=pltpu.CompilerParams(dimension_semantics=("arbitrary",),
                                             vmem_limit_bytes=VMEM_LIMIT),
        name="hgrn",
    )(jnp.asarray(tbl), pa, pa, pa, pa, pa, pa, lb_logits, jnp.asarray(masks), jnp.asarray(signs))


def _na_build_bias(rpb_ref, bias_ref):
    n_ro = 2 * NA_KR - 1
    n_co = 2 * NA_KC - 1
    qc = lax.broadcasted_iota(jnp.int32, (GRID_W, GRID_W), 0)
    kc = lax.broadcasted_iota(jnp.int32, (GRID_W, GRID_W), 1)
    co = jnp.clip(kc - qc, -(NA_KC - 1), NA_KC - 1) + NA_KC - 1
    ws = jnp.clip(qc - NA_KC // 2, 0, GRID_W - NA_KC)
    valid = (kc >= ws) & (kc < ws + NA_KC)

    def build(hr, carry):
        h = hr // n_ro
        ro = hr % n_ro
        t = jnp.full((GRID_W, GRID_W), NEG, F32)
        for j in range(n_co):
            t = jnp.where(co == j, rpb_ref[hr * n_co + j], t)
        t = jnp.where(valid, t, NEG)
        for i in range(NA_KR):
            dd = ro - i

            @pl.when((dd >= 0) & (dd < NA_KR))
            def _():
                bias_ref[h, dd, :, i * GRID_W:(i + 1) * GRID_W] = t
        return carry

    lax.fori_loop(0, NA_HEADS * n_ro, build, 0)


def _na_body(tbl_ref, rpb_ref, q_ref, k_ref, v_ref, o_ref, bias_ref):
    step = pl.program_id(0)

    @pl.when(step == 0)
    def _():
        _na_build_bias(rpb_ref, bias_ref)

    r0 = tbl_ref[2, step]
    w0 = tbl_ref[3, step]
    n_rows = tbl_ref[4, step]
    head = lax.broadcasted_iota(jnp.int32, (GRID_W, NA_GROUP * NA_HD), 1) // NA_HD
    nkeys = NA_KR * GRID_W
    scale = jnp.asarray(NA_HD ** -0.5, BF16)

    groups = [slice(g * NA_GROUP * NA_HD, (g + 1) * NA_GROUP * NA_HD) for g in range(NA_HEADS // NA_GROUP)]

    units = []
    for rr in range(NA_ROWS):
        r = r0 + rr
        row_start = jnp.clip(r - NA_KR // 2, 0, n_rows - NA_KR)
        dd = row_start - r + (NA_KR - 1)
        keys = pl.ds(pl.multiple_of((row_start - w0) * GRID_W, GRID_W), nkeys)
        qrows = slice(rr * GRID_W, (rr + 1) * GRID_W)
        units += [(dd, keys, qrows, g, lanes) for g, lanes in enumerate(groups)]
    scores = []
    for dd, keys, qrows, g, lanes in units:
        qp = q_ref[qrows, lanes] * scale
        zero = jnp.zeros_like(qp)
        qs = jnp.concatenate([jnp.where(head == h, qp, zero) for h in range(NA_GROUP)], axis=0)
        scores.append(_dot_nt(qs, k_ref[keys, lanes]))
    probs = []
    for (dd, keys, qrows, g, lanes), s in zip(units, scores):
        s = s + jnp.concatenate([bias_ref[NA_GROUP * g + h, dd] for h in range(NA_GROUP)], axis=0)
        m = jnp.max(s, axis=-1, keepdims=True)
        p = jnp.exp(s - m)
        probs.append((p.astype(BF16), jnp.sum(p, axis=-1, keepdims=True)))
    for (dd, keys, qrows, g, lanes), (p, l) in zip(units, probs):
        pv = _dot(p, v_ref[keys, lanes]) / l
        o = pv[:GRID_W]
        for h in range(1, NA_GROUP):
            o = jnp.where(head == h, pv[h * GRID_W:(h + 1) * GRID_W], o)
        o_ref[qrows, lanes] = o.astype(o_ref.dtype)


def _na_table(segs):
    blk = NA_ROWS * GRID_W
    rows = []
    for off, batch, seq in segs:
        n_rows = seq // GRID_W
        assert n_rows >= NA_WIN and seq % blk == 0 and off % blk == 0
        for bi in range(batch):
            base = off + bi * seq
            for i in range(seq // blk):
                r0 = i * NA_ROWS
                w0 = min(max(r0 - NA_ROWS, 0), n_rows - NA_WIN)
                rows.append((base // blk + i, base // GRID_W + w0, r0, w0, n_rows))
    return np.asarray(rows, np.int32).T.copy()


def _na(pn, rpb, segs):
    t = pn.shape[0]
    tbl = _na_table(segs)
    steps = tbl.shape[1]
    blk = NA_ROWS * GRID_W

    def window(col):
        return pl.BlockSpec((pl.Element(NA_WIN * GRID_W), pl.Element(NA_WIDTH)),
                            lambda i, tb: (tb[1, i] * GRID_W, col * NA_WIDTH))

    grid_spec = pltpu.PrefetchScalarGridSpec(
        num_scalar_prefetch=1,
        grid=(steps,),
        in_specs=[pl.BlockSpec(memory_space=pltpu.SMEM),
                  pl.BlockSpec((blk, NA_WIDTH), lambda i, tb: (tb[0, i], 0)),
                  window(1), window(2)],
        out_specs=pl.BlockSpec((blk, NA_WIDTH), lambda i, tb: (tb[0, i], 0)),
        scratch_shapes=[pltpu.VMEM((NA_HEADS, NA_KR, GRID_W, NA_KR * GRID_W), F32)],
    )
    return pl.pallas_call(
        _na_body,
        grid_spec=grid_spec,
        out_shape=jax.ShapeDtypeStruct((t, NA_WIDTH), BF16),
        compiler_params=pltpu.CompilerParams(dimension_semantics=("arbitrary",),
                                             vmem_limit_bytes=VMEM_LIMIT),
        name="natten",
    )(jnp.asarray(tbl), rpb.astype(F32).reshape(-1), pn, pn, pn)


def _out_specs_shapes(out_tokens):
    starts = [0]
    for t in out_tokens:
        starts.append(starts[-1] + t // TM)
    specs = []
    for k in range(len(out_tokens)):
        s, n = starts[k], starts[k + 1] - starts[k]
        specs.append(pl.BlockSpec((TM, D_MODEL), lambda i, *_, s=s, n=n: (jnp.clip(i - s, 0, n - 1), 0)))
    shapes = [jax.ShapeDtypeStruct((t, D_MODEL), F32) for t in out_tokens]
    return tuple(starts), specs, shapes


def _store_parts(starts, out_refs, accs):
    def store(o_ref):
        for sl, acc in zip(SLABS, accs):
            o_ref[sl, :] = acc

    if len(out_refs) == 1:
        store(out_refs[0])
        return
    step = pl.program_id(0)
    for k, o_ref in enumerate(out_refs):
        @pl.when((step >= starts[k]) & (step < starts[k + 1]))
        def _(o_ref=o_ref):
            store(o_ref)


def _ffn_tail(x1s, gf_ref, wi_ref, wo_ref, gfin_ref, final):
    hs = [_rms(x1, gf_ref[...]).astype(BF16) for x1 in x1s]
    accs = list(x1s)
    for c in range(0, D_FF, FF_CHUNK):
        gus = [(_dot(h, wi_ref[:, c:c + FF_CHUNK]), _dot(h, wi_ref[:, D_FF + c:D_FF + c + FF_CHUNK])) for h in hs]
        acts = [(g * _sigmoid(g) * u).astype(BF16) for g, u in gus]
        accs = [acc + _dot(a, wo_ref[c:c + FF_CHUNK, :]) for acc, a in zip(accs, acts)]
    if final:
        accs = [_rms(acc, gfin_ref[...]) for acc in accs]
    return accs


def _post_even_body(final, starts, out_starts, *refs):
    n = len(starts) - 1
    of_ref, ob_ref, ga_ref, n_ref, gain_ref, wm_ref, gf_ref, wi_ref, wo_ref, gfin_ref = refs[n:n + 10]
    x = _select_part(starts, refs[:n])
    x1s = []
    for sl in SLABS:
        o = of_ref[sl, :] + ob_ref[sl, :]
        parts = []
        for h in range(A_HEADS):
            oh = o[:, h * A_DK:(h + 1) * A_DK]
            parts.append(oh * lax.rsqrt(jnp.mean(oh * oh, axis=-1, keepdims=True) + EPS))
        on = jnp.concatenate(parts, axis=1) * gain_ref[...]
        g = ga_ref[sl, :]
        oa = (on * (g * _sigmoid(g))).astype(BF16)
        mix = _dot(oa, wm_ref[0:A_WIDTH, :]) + _dot(n_ref[sl, :], wm_ref[A_WIDTH:A_WIDTH + NA_WIDTH, :])
        x1s.append(x[sl, :] + mix)
    accs = _ffn_tail(x1s, gf_ref, wi_ref, wo_ref, gfin_ref, final)
    _store_parts(out_starts, refs[n + 10:], accs)


def _post_even(parts, o_f, o_b, pa, o_n, gain, w_mix, e, g_ffn, w_ffn_in, w_ffn_out, l, g_final, final, out_tokens):
    starts = _tile_starts(parts)
    out_starts, out_specs, out_shapes = _out_specs_shapes(out_tokens)
    tok = lambda w, col=0: pl.BlockSpec((TM, w), lambda i: (i, col))
    return pl.pallas_call(
        functools.partial(_post_even_body, final, starts, out_starts),
        grid=(starts[-1],),
        in_specs=_part_specs(parts) + [tok(A_WIDTH), tok(A_WIDTH), tok(A_WIDTH, PA_G), tok(NA_WIDTH),
                                       _resident((1, A_WIDTH)), _layer_of(w_mix, e),
                                       _resident((1, D_MODEL)), _layer_of(w_ffn_in, l),
                                       _layer_of(w_ffn_out, l), _resident((1, D_MODEL))],
        out_specs=out_specs,
        out_shape=out_shapes,
        compiler_params=pltpu.CompilerParams(dimension_semantics=("arbitrary",),
                                             vmem_limit_bytes=VMEM_LIMIT),
        name="post_even",
    )(*parts, o_f, o_b, pa, o_n, gain, w_mix, g_ffn, w_ffn_in, w_ffn_out, g_final)


def _odd_body(final, out_starts, tbl_ref, x_ref, xp_ref, xn_ref, gm_ref, wc_ref, cw_ref, wco_ref, gf_ref,
              wi_ref, wo_ref, gfin_ref, *out_refs):
    step = pl.program_id(0)
    x = x_ref[...]
    gm = gm_ref[...]
    hs = [_rms(x[sl, :], gm).astype(BF16) for sl in SLABS]
    hh = _rms(jnp.concatenate([xp_ref[...], xn_ref[...]], axis=0), gm).astype(BF16)
    zin = [jnp.concatenate([hs[0], hh], axis=0)] + hs[1:]
    zs = [_dot(h, wc_ref[:, D_MODEL:2 * D_MODEL]) * _dot(h, wc_ref[:, 2 * D_MODEL:3 * D_MODEL]) for h in zin]
    zh = zs[0][TM // len(SLABS):, :]
    zs[0] = zs[0][:TM // len(SLABS), :]
    bgs = [_dot(h, wc_ref[:, 0:D_MODEL]) for h in hs]
    z_prev = jnp.where(tbl_ref[2, step] == 1, 0.0, zh[HALO - 1:HALO, :])
    z_next = jnp.where(tbl_ref[3, step] == 1, 0.0, zh[HALO:HALO + 1, :])
    rows = TM // len(SLABS)
    row = lax.broadcasted_iota(jnp.int32, (rows, 1), 0)
    x1s = []
    for k, (sl, z, bg) in enumerate(zip(SLABS, zs, bgs)):
        above = z_prev if k == 0 else zs[k - 1][rows - 1:rows, :]
        below = z_next if k == len(SLABS) - 1 else zs[k + 1][0:1, :]
        z_dn = jnp.where(row == 0, above, pltpu.roll(z, 1, 0))
        z_up = jnp.where(row == rows - 1, below, pltpu.roll(z, rows - 1, 0))
        conv = z_dn * cw_ref[0:1, :] + z * cw_ref[1:2, :] + z_up * cw_ref[2:3, :]
        x1s.append(x[sl, :] + _dot((bg * conv).astype(BF16), wco_ref[...]))
    accs = _ffn_tail(x1s, gf_ref, wi_ref, wo_ref, gfin_ref, final)
    _store_parts(out_starts, out_refs, accs)


def _odd_table(segs, t):
    rows = []
    starts = set()
    ends = set()
    for off, batch, seq in segs:
        for bi in range(batch):
            starts.add(off + bi * seq)
            ends.add(off + (bi + 1) * seq)
    for i in range(t // TM):
        t0 = i * TM
        first = int(t0 in starts)
        last = int(t0 + TM in ends)
        rows.append((max(t0 // HALO - 1, 0), min((t0 + TM) // HALO, t // HALO - 1), first, last))
    return np.asarray(rows, np.int32).T.copy()


def _odd(x, g_mix, w_conv_in, conv_w, w_conv_out, o, g_ffn, w_ffn_in, w_ffn_out, l, g_final, final, segs, out_tokens):
    t = x.shape[0]
    tbl = _odd_table(segs, t)
    out_starts, out_specs, out_shapes = _out_specs_shapes(out_tokens)
    res = lambda shape: pl.BlockSpec(shape, lambda i, tb: (0,) * len(shape), pipeline_mode=pl.Buffered(1))
    grid_spec = pltpu.PrefetchScalarGridSpec(
        num_scalar_prefetch=1,
        grid=(t // TM,),
        in_specs=[pl.BlockSpec((TM, D_MODEL), lambda i, tb: (i, 0)),
                  pl.BlockSpec((HALO, D_MODEL), lambda i, tb: (tb[0, i], 0)),
                  pl.BlockSpec((HALO, D_MODEL), lambda i, tb: (tb[1, i], 0)),
                  res((1, D_MODEL)), _layer_of(w_conv_in, o), _layer_of(conv_w, o),
                  _layer_of(w_conv_out, o), res((1, D_MODEL)), _layer_of(w_ffn_in, l),
                  _layer_of(w_ffn_out, l), res((1, D_MODEL))],
        out_specs=out_specs,
    )
    return pl.pallas_call(
        functools.partial(_odd_body, final, out_starts),
        grid_spec=grid_spec,
        out_shape=out_shapes,
        compiler_params=pltpu.CompilerParams(dimension_semantics=("arbitrary",),
                                             vmem_limit_bytes=VMEM_LIMIT),
        name="odd_layer",
    )(jnp.asarray(tbl), x, x, x, g_mix, w_conv_in, conv_w, w_conv_out, g_ffn, w_ffn_in, w_ffn_out, g_final)


def _trunk(xs, norm_mix, norm_ffn, norm_final, even_w_in, even_w_out, hgrn_lb_logits, hgrn_norm,
           na_rpb, conv_w_in, conv_w, conv_w_out, ffn_w_in, ffn_w_out):
    depth = norm_mix.shape[0]
    segs = []
    off = 0
    for a in xs:
        segs.append((off, a.shape[0], a.shape[1]))
        off += a.shape[0] * a.shape[1]
    parts = [a.reshape(-1, D_MODEL) for a in xs]
    part_tokens = [p.shape[0] for p in parts]
    total = off
    row = lambda v: v.reshape(1, -1).astype(F32)
    g_final = row(norm_final)
    w_fi, w_fo = ffn_w_in.astype(BF16), ffn_w_out.astype(BF16)
    w_ei, w_eo = even_w_in.astype(BF16), even_w_out.astype(BF16)
    w_ci, w_co, w_cw = conv_w_in.astype(BF16), conv_w_out.astype(BF16), conv_w.astype(F32)
    for l in range(depth):
        final = l == depth - 1
        out_tokens = part_tokens if final else [total]
        if l % 2 == 0:
            e = l // 2
            pa, pn = _in_even(parts, row(norm_mix[l]), w_ei, e)
            o_f, o_b = _hgrn(pa, hgrn_lb_logits.astype(F32), e, segs)
            o_n = _na(pn, na_rpb[e], segs)
            outs = _post_even(parts, o_f, o_b, pa, o_n, row(hgrn_norm[e]), w_eo, e,
                              row(norm_ffn[l]), w_fi, w_fo, l, g_final, final, out_tokens)
        else:
            o = l // 2
            x = parts[0] if len(parts) == 1 else jnp.concatenate(parts, axis=0)
            outs = _odd(x, row(norm_mix[l]), w_ci, w_cw, w_co, o, row(norm_ffn[l]), w_fi, w_fo, l, g_final, final,
                        segs, out_tokens)
        parts = list(outs)
    return tuple(p.reshape(a.shape) for p, a in zip(parts, xs))


def kernel(x_prompt, x_sample, norm_mix, norm_ffn, norm_final, even_w_in, even_w_out, hgrn_lb_logits,
           hgrn_norm, na_rpb, conv_w_in, conv_w, conv_w_out, ffn_w_in, ffn_w_out):
    return _trunk([x_prompt, x_sample], norm_mix, norm_ffn, norm_final, even_w_in, even_w_out,
                  hgrn_lb_logits, hgrn_norm, na_rpb, conv_w_in, conv_w, conv_w_out, ffn_w_in, ffn_w_out)
```

```python
import functools

import numpy as np
import jax
import jax.numpy as jnp
from jax import lax
from jax.experimental import pallas as pl
from jax.experimental.pallas import tpu as pltpu

F32 = jnp.float32
BF16 = jnp.bfloat16

SUBLANES = 8
D_MODEL = 1024
EPS = 1e-6
NEG = -1e30
LOG2E = 1.4426950408889634
GRID_W = 64
A_WIDTH = 512
A_HEADS = 4
A_DK = 128
CHUNK = 64
LEVELS = (2, 4, 8, 16, 32, 64)
NA_WIDTH = 512
NA_HEADS = 8
NA_HD = 64
NA_KR = 8
NA_KC = 16
NA_GROUP = 2
CONV_W = 3
D_FF = 2816
FF_CHUNK = 256
PA_W = 5 * A_WIDTH
PA_Q, PA_ZF, PA_ZB, PA_I, PA_G = range(5)
PN_W = 3 * NA_WIDTH
EVEN_IN = PA_W + PN_W

TM = 512
SLABS = (slice(0, TM // 2), slice(TM // 2, TM))
HGRN_BLOCK = 1024
NA_ROWS = 8
NA_WIN = 3 * NA_ROWS
HALO = 8
V7X_VMEM_BYTES = 64 * 1024 * 1024
VMEM_LIMIT = V7X_VMEM_BYTES * 7 // 8

NT_DIMS = (((1,), (1,)), ((), ()))
TN_DIMS = (((0,), (0,)), ((), ()))


def _resident(shape):
    nd = len(shape)
    return pl.BlockSpec(shape, lambda *_: (0,) * nd, pipeline_mode=pl.Buffered(1))


def _layer_of(stack, layer):
    nd = stack.ndim - 1
    return pl.BlockSpec((None,) + stack.shape[1:], lambda *_: (layer,) + (0,) * nd, pipeline_mode=pl.Buffered(1))


def _rms(x, g):
    return x * lax.rsqrt(jnp.mean(x * x, axis=-1, keepdims=True) + EPS) * g


def _sigmoid(x):
    return jax.nn.sigmoid(x)


def _dot(a, b):
    return jnp.dot(a, b, preferred_element_type=F32)


def _dot_nt(a, b):
    return lax.dot_general(a, b, NT_DIMS, preferred_element_type=F32)


def _tile_starts(parts):
    starts = [0]
    for a in parts:
        starts.append(starts[-1] + a.shape[0] // TM)
    return tuple(starts)


def _part_specs(parts):
    starts = _tile_starts(parts)
    return [pl.BlockSpec((TM, D_MODEL),
                         lambda i, *_, s=starts[k], n=starts[k + 1] - starts[k]: (jnp.clip(i - s, 0, n - 1), 0))
            for k in range(len(parts))]


def _select_part(starts, refs):
    step = pl.program_id(0)
    x = refs[-1][...]
    for k in range(len(refs) - 2, -1, -1):
        x = jnp.where(step < starts[k + 1], refs[k][...], x)
    return x


def _in_even_body(starts, *refs):
    n = len(starts) - 1
    g_ref, w_ref, pa_ref, pn_ref = refs[n:]
    h = _rms(_select_part(starts, refs[:n]), g_ref[...]).astype(BF16)
    for c in range(0, PA_W, A_WIDTH):
        pa_ref[:, c:c + A_WIDTH] = _dot(h, w_ref[:, c:c + A_WIDTH])
    for c in range(0, PN_W, NA_WIDTH):
        pn_ref[:, c:c + NA_WIDTH] = _dot(h, w_ref[:, PA_W + c:PA_W + c + NA_WIDTH]).astype(BF16)


def _in_even(parts, g, w, layer):
    assert w.shape[1:] == (D_MODEL, EVEN_IN) and all(p.shape[0] % TM == 0 for p in parts)
    starts = _tile_starts(parts)
    t = starts[-1] * TM
    return pl.pallas_call(
        functools.partial(_in_even_body, starts),
        grid=(starts[-1],),
        in_specs=_part_specs(parts) + [_resident((1, D_MODEL)), _layer_of(w, layer)],
        out_specs=[pl.BlockSpec((TM, PA_W), lambda i: (i, 0)),
                   pl.BlockSpec((TM, PN_W), lambda i: (i, 0))],
        out_shape=[jax.ShapeDtypeStruct((t, PA_W), F32),
                   jax.ShapeDtypeStruct((t, PN_W), BF16)],
        compiler_params=pltpu.CompilerParams(dimension_semantics=("arbitrary",),
                                             vmem_limit_bytes=VMEM_LIMIT),
        name="in_even",
    )(*parts, g, w)


def _hgrn_masks():
    t = np.arange(CHUNK)[:, None]
    s = np.arange(CHUNK)[None, :]
    out = np.zeros((2, len(LEVELS) - 1, CHUNK, CHUNK), np.float32)
    for d, rev in enumerate((False, True)):
        for li, L in enumerate(LEVELS[1:]):
            half = L // 2
            same = (t // L) == (s // L)
            if rev:
                m = same & ((t % L) < half) & ((s % L) >= half)
            else:
                m = same & ((t % L) >= half) & ((s % L) < half)
            out[d, li] = m
    return np.tile(out, (1, 1, 1, 2))


def _hgrn_signs():
    t = np.arange(CHUNK)
    out = np.zeros((2, len(LEVELS), CHUNK), np.float32)
    for d, rev in enumerate((False, True)):
        for li, L in enumerate(LEVELS):
            second = (t % L) >= L // 2
            out[d, li] = np.where(second != rev, 1.0, -1.0)
    return np.ascontiguousarray(np.broadcast_to(out[..., None], out.shape + (A_DK,)))


def _tri(rev):
    t = lax.broadcasted_iota(jnp.int32, (CHUNK, CHUNK), 0)
    s = lax.broadcasted_iota(jnp.int32, (CHUNK, CHUNK), 1)
    return jnp.where((s >= t) if rev else (s <= t), 1.0, 0.0).astype(BF16)


def _level_ref(b3, L, rev):
    g, sub, w = b3.shape
    half = L // 2
    if L >= 2 * SUBLANES:
        n = L // SUBLANES
        pieces = []
        for blk in range(CHUNK // L):
            r = blk * L + (half if rev else half - 1)
            pieces.append(jnp.broadcast_to(b3[r // SUBLANES:r // SUBLANES + 1, r % SUBLANES:r % SUBLANES + 1, :],
                                           (n, sub, w)))
        return pieces[0] if len(pieces) == 1 else jnp.concatenate(pieces, axis=0)
    srow = lax.broadcasted_iota(jnp.int32, b3.shape, 1)
    if L == SUBLANES:
        r = half if rev else half - 1
        return jnp.broadcast_to(b3[:, r:r + 1, :], b3.shape)
    if L == 4:
        r = half if rev else half - 1
        lo = jnp.broadcast_to(b3[:, r:r + 1, :], b3.shape)
        hi = jnp.broadcast_to(b3[:, r + 4:r + 5, :], b3.shape)
        return jnp.where(srow < 4, lo, hi)
    odd = (srow % 2) == 1
    if rev:
        return jnp.where(odd, b3, pltpu.roll(b3, SUBLANES - 1, 1))
    return jnp.where(odd, pltpu.roll(b3, 1, 1), b3)


def _pair_blockdiag(x):
    zero = jnp.zeros((CHUNK, A_DK), x.dtype)
    return jnp.concatenate([jnp.concatenate([x[:, :A_DK], zero], axis=1),
                            jnp.concatenate([zero, x[:, A_DK:]], axis=1)], axis=0)


def _hgrn_chunk(q, k, v, b, state, d, rev, m_ref, sgn_ref):
    tot = b[0:1, :] if rev else b[CHUNK - 1:CHUNK, :]
    qb = q.astype(BF16)
    kb = k.astype(BF16)
    q0 = qb * jnp.exp2(b).astype(BF16)
    k2 = kb * jnp.exp2(tot - b).astype(BF16)
    vb = v.astype(BF16)
    dec = jnp.exp2(tot)
    inter = []
    for h in range(A_HEADS):
        sl = slice(h * A_DK, (h + 1) * A_DK)
        s_t = state[h]
        inter.append(_dot_nt(q0[:, sl], s_t.astype(BF16)))
        ds = lax.dot_general(vb[:, sl], k2[:, sl], TN_DIMS, preferred_element_type=F32)
        state[h] = dec[:, sl] * s_t + ds
    b3 = b.reshape(CHUNK // SUBLANES, SUBLANES, A_WIDTH)

    def level_decay(li):
        sgn = jnp.concatenate([sgn_ref[d, li]] * A_HEADS, axis=1).reshape(b3.shape)
        return jnp.exp2((b3 - _level_ref(b3, LEVELS[li], rev)) * sgn)

    shift = SUBLANES - 1 if rev else 1
    k_adj = pltpu.roll(k.reshape(b3.shape), shift, 1).reshape(CHUNK, A_WIDTH)
    v_adj = pltpu.roll(v.reshape(b3.shape), shift, 1).reshape(CHUNK, A_WIDTH)
    p_same = q * k
    p_adj = q * level_decay(0).reshape(CHUNK, A_WIDTH) * k_adj
    row = lax.broadcasted_iota(jnp.int32, (CHUNK, 1), 0)
    is_query = (row % 2 == 0) if rev else (row % 2 == 1)
    near = []
    for h in range(A_HEADS):
        sl = slice(h * A_DK, (h + 1) * A_DK)
        a_same = jnp.sum(p_same[:, sl], axis=-1, keepdims=True)
        a_adj = jnp.where(is_query, jnp.sum(p_adj[:, sl], axis=-1, keepdims=True), 0.0)
        near.append(a_same * v[:, sl] + a_adj * v_adj[:, sl])
    pairs = [slice(p * 2 * A_DK, (p + 1) * 2 * A_DK) for p in range(A_HEADS // 2)]
    acc = None
    for li in range(1, len(LEVELS)):
        e = level_decay(li).reshape(CHUNK, A_WIDTH).astype(BF16)
        qs = qb * e
        ks = kb * e
        sc = [_dot_nt(qs[:, p], _pair_blockdiag(ks[:, p])) * m_ref[d, li - 1] for p in pairs]
        acc = sc if acc is None else [a + s for a, s in zip(acc, sc)]
    intra = [_dot(a.astype(BF16), _pair_blockdiag(vb[:, p])) for a, p in zip(acc, pairs)]
    return jnp.concatenate(intra, axis=1) + jnp.concatenate(inter, axis=1) + jnp.concatenate(near, axis=1)


def _hgrn_body(layer, tbl_ref, qf_ref, zf_ref, vf_ref, qb_ref, zb_ref, vb_ref, lbl_ref, m_ref, sgn_ref,
               of_ref, ob_ref, s_ref):
    step = pl.program_id(0)

    @pl.when(tbl_ref[2, step] == 1)
    def _():
        s_ref[...] = jnp.zeros_like(s_ref)

    n_even = lbl_ref.shape[0]
    logits = [lbl_ref[i] for i in range(n_even)]
    mx = functools.reduce(jnp.maximum, logits)
    ex = [jnp.exp(l - mx) for l in logits]
    den = functools.reduce(lambda x, y: x + y, ex)
    ps = [e / den for e in ex]
    lbs = functools.reduce(lambda x, y: x + y, ps[:layer + 1]) - ps[0]

    nc = HGRN_BLOCK // CHUNK
    refs = ((qf_ref, zf_ref, vf_ref, of_ref), (qb_ref, zb_ref, vb_ref, ob_ref))
    tris = (_tri(False), _tri(True))
    states = [[s_ref[d, h] for h in range(A_HEADS)] for d in range(2)]

    for c in range(nc):
        prep = []
        for d, rev in enumerate((False, True)):
            q_ref, z_ref, v_ref, _ = refs[d]
            r0 = ((nc - 1 - c) if rev else c) * CHUNK
            rows = slice(r0, r0 + CHUNK)
            qa = q_ref[rows, :]
            z = z_ref[rows, :]
            lb = lbs[d:d + 1, :]
            q = qa * _sigmoid(qa)
            f = lb + (1.0 - lb) * _sigmoid(z)
            k = 1.0 - f
            lf = jnp.log(f)
            hi = lf.astype(BF16)
            r1 = lf - hi.astype(F32)
            mid = r1.astype(BF16)
            lo = (r1 - mid.astype(F32)).astype(BF16)
            cs = _dot(tris[d], jnp.concatenate([hi, mid, lo], axis=1))
            b = (cs[:, :A_WIDTH] + cs[:, A_WIDTH:2 * A_WIDTH] + cs[:, 2 * A_WIDTH:]) * LOG2E
            prep.append((rows, q, k, v_ref[rows, :], b))
        for d, rev in enumerate((False, True)):
            rows, q, k, v, b = prep[d]
            refs[d][3][rows, :] = _hgrn_chunk(q, k, v, b, states[d], d, rev, m_ref, sgn_ref)
    for d in range(2):
        for h in range(A_HEADS):
            s_ref[d, h] = states[d][h]


def _hgrn_table(segs):
    rows = []
    for off, batch, seq in segs:
        assert seq % HGRN_BLOCK == 0 and off % HGRN_BLOCK == 0
        nblk = seq // HGRN_BLOCK
        for bi in range(batch):
            base = (off + bi * seq) // HGRN_BLOCK
            for i in range(nblk):
                rows.append((base + i, base + nblk - 1 - i, int(i == 0)))
    return np.asarray(rows, np.int32).T.copy()


def _hgrn(pa, lb_logits, layer, segs):
    t = pa.shape[0]
    tbl = _hgrn_table(segs)
    steps = tbl.shape[1]
    masks = _hgrn_masks()
    signs = _hgrn_signs()

    def spec(col, which):
        return pl.BlockSpec((HGRN_BLOCK, A_WIDTH), lambda i, tb: (tb[which, i], col))

    grid_spec = pltpu.PrefetchScalarGridSpec(
        num_scalar_prefetch=1,
        grid=(steps,),
        in_specs=[spec(PA_Q, 0), spec(PA_ZF, 0), spec(PA_I, 0), spec(PA_Q, 1), spec(PA_ZB, 1), spec(PA_I, 1),
                  pl.BlockSpec(lb_logits.shape, lambda i, tb: (0, 0, 0)),
                  pl.BlockSpec(masks.shape, lambda i, tb: (0, 0, 0, 0)),
                  pl.BlockSpec(signs.shape, lambda i, tb: (0, 0, 0, 0))],
        out_specs=[pl.BlockSpec((HGRN_BLOCK, A_WIDTH), lambda i, tb: (tb[0, i], 0)),
                   pl.BlockSpec((HGRN_BLOCK, A_WIDTH), lambda i, tb: (tb[1, i], 0))],
        scratch_shapes=[pltpu.VMEM((2, A_HEADS, A_DK, A_DK), F32)],
    )
    return pl.pallas_call(
        functools.partial(_hgrn_body, layer),
        grid_spec=grid_spec,
        out_shape=[jax.ShapeDtypeStruct((t, A_WIDTH), F32),
                   jax.ShapeDtypeStruct((t, A_WIDTH), F32)],
        compiler_params=pltpu.CompilerParams(dimension_semantics=("arbitrary",),
                                             vmem_limit_bytes=VMEM_LIMIT),
        name="hgrn",
    )(jnp.asarray(tbl), pa, pa, pa, pa, pa, pa, lb_logits, jnp.asarray(masks), jnp.asarray(signs))


def _na_build_bias(rpb_ref, bias_ref):
    n_ro = 2 * NA_KR - 1
    n_co = 2 * NA_KC - 1
    qc = lax.broadcasted_iota(jnp.int32, (GRID_W, GRID_W), 0)
    kc = lax.broadcasted_iota(jnp.int32, (GRID_W, GRID_W), 1)
    co = jnp.clip(kc - qc, -(NA_KC - 1), NA_KC - 1) + NA_KC - 1
    ws = jnp.clip(qc - NA_KC // 2, 0, GRID_W - NA_KC)
    valid = (kc >= ws) & (kc < ws + NA_KC)

    def build(hr, carry):
        h = hr // n_ro
        ro = hr % n_ro
        t = jnp.full((GRID_W, GRID_W), NEG, F32)
        for j in range(n_co):
            t = jnp.where(co == j, rpb_ref[hr * n_co + j], t)
        t = jnp.where(valid, t, NEG)
        for i in range(NA_KR):
            dd = ro - i

            @pl.when((dd >= 0) & (dd < NA_KR))
            def _():
                bias_ref[h, dd, :, i * GRID_W:(i + 1) * GRID_W] = t
        return carry

    lax.fori_loop(0, NA_HEADS * n_ro, build, 0)


def _na_body(tbl_ref, rpb_ref, q_ref, k_ref, v_ref, o_ref, bias_ref):
    step = pl.program_id(0)

    @pl.when(step == 0)
    def _():
        _na_build_bias(rpb_ref, bias_ref)

    r0 = tbl_ref[2, step]
    w0 = tbl_ref[3, step]
    n_rows = tbl_ref[4, step]
    head = lax.broadcasted_iota(jnp.int32, (GRID_W, NA_GROUP * NA_HD), 1) // NA_HD
    nkeys = NA_KR * GRID_W
    scale = jnp.asarray(NA_HD ** -0.5, BF16)

    groups = [slice(g * NA_GROUP * NA_HD, (g + 1) * NA_GROUP * NA_HD) for g in range(NA_HEADS // NA_GROUP)]

    units = []
    for rr in range(NA_ROWS):
        r = r0 + rr
        row_start = jnp.clip(r - NA_KR // 2, 0, n_rows - NA_KR)
        dd = row_start - r + (NA_KR - 1)
        keys = pl.ds(pl.multiple_of((row_start - w0) * GRID_W, GRID_W), nkeys)
        qrows = slice(rr * GRID_W, (rr + 1) * GRID_W)
        units += [(dd, keys, qrows, g, lanes) for g, lanes in enumerate(groups)]
    scores = []
    for dd, keys, qrows, g, lanes in units:
        qp = q_ref[qrows, lanes] * scale
        zero = jnp.zeros_like(qp)
        qs = jnp.concatenate([jnp.where(head == h, qp, zero) for h in range(NA_GROUP)], axis=0)
        scores.append(_dot_nt(qs, k_ref[keys, lanes]))
    probs = []
    for (dd, keys, qrows, g, lanes), s in zip(units, scores):
        s = s + jnp.concatenate([bias_ref[NA_GROUP * g + h, dd] for h in range(NA_GROUP)], axis=0)
        m = jnp.max(s, axis=-1, keepdims=True)
        p = jnp.exp(s - m)
        probs.append((p.astype(BF16), jnp.sum(p, axis=-1, keepdims=True)))
    for (dd, keys, qrows, g, lanes), (p, l) in zip(units, probs):
        pv = _dot(p, v_ref[keys, lanes]) / l
        o = pv[:GRID_W]
        for h in range(1, NA_GROUP):
            o = jnp.where(head == h, pv[h * GRID_W:(h + 1) * GRID_W], o)
        o_ref[qrows, lanes] = o.astype(o_ref.dtype)


def _na_table(segs):
    blk = NA_ROWS * GRID_W
    rows = []
    for off, batch, seq in segs:
        n_rows = seq // GRID_W
        assert n_rows >= NA_WIN and seq % blk == 0 and off % blk == 0
        for bi in range(batch):
            base = off + bi * seq
            for i in range(seq // blk):
                r0 = i * NA_ROWS
                w0 = min(max(r0 - NA_ROWS, 0), n_rows - NA_WIN)
                rows.append((base // blk + i, base // GRID_W + w0, r0, w0, n_rows))
    return np.asarray(rows, np.int32).T.copy()


def _na(pn, rpb, segs):
    t = pn.shape[0]
    tbl = _na_table(segs)
    steps = tbl.shape[1]
    blk = NA_ROWS * GRID_W

    def window(col):
        return pl.BlockSpec((pl.Element(NA_WIN * GRID_W), pl.Element(NA_WIDTH)),
                            lambda i, tb: (tb[1, i] * GRID_W, col * NA_WIDTH))

    grid_spec = pltpu.PrefetchScalarGridSpec(
        num_scalar_prefetch=1,
        grid=(steps,),
        in_specs=[pl.BlockSpec(memory_space=pltpu.SMEM),
                  pl.BlockSpec((blk, NA_WIDTH), lambda i, tb: (tb[0, i], 0)),
                  window(1), window(2)],
        out_specs=pl.BlockSpec((blk, NA_WIDTH), lambda i, tb: (tb[0, i], 0)),
        scratch_shapes=[pltpu.VMEM((NA_HEADS, NA_KR, GRID_W, NA_KR * GRID_W), F32)],
    )
    return pl.pallas_call(
        _na_body,
        grid_spec=grid_spec,
        out_shape=jax.ShapeDtypeStruct((t, NA_WIDTH), BF16),
        compiler_params=pltpu.CompilerParams(dimension_semantics=("arbitrary",),
                                             vmem_limit_bytes=VMEM_LIMIT),
        name="natten",
    )(jnp.asarray(tbl), rpb.astype(F32).reshape(-1), pn, pn, pn)


def _out_specs_shapes(out_tokens):
    starts = [0]
    for t in out_tokens:
        starts.append(starts[-1] + t // TM)
    specs = []
    for k in range(len(out_tokens)):
        s, n = starts[k], starts[k + 1] - starts[k]
        specs.append(pl.BlockSpec((TM, D_MODEL), lambda i, *_, s=s, n=n: (jnp.clip(i - s, 0, n - 1), 0)))
    shapes = [jax.ShapeDtypeStruct((t, D_MODEL), F32) for t in out_tokens]
    return tuple(starts), specs, shapes


def _store_parts(starts, out_refs, accs):
    def store(o_ref):
        for sl, acc in zip(SLABS, accs):
            o_ref[sl, :] = acc

    if len(out_refs) == 1:
        store(out_refs[0])
        return
    step = pl.program_id(0)
    for k, o_ref in enumerate(out_refs):
        @pl.when((step >= starts[k]) & (step < starts[k + 1]))
        def _(o_ref=o_ref):
            store(o_ref)


def _ffn_tail(x1s, gf_ref, wi_ref, wo_ref, gfin_ref, final):
    hs = [_rms(x1, gf_ref[...]).astype(BF16) for x1 in x1s]
    accs = list(x1s)
    for c in range(0, D_FF, FF_CHUNK):
        gus = [(_dot(h, wi_ref[:, c:c + FF_CHUNK]), _dot(h, wi_ref[:, D_FF + c:D_FF + c + FF_CHUNK])) for h in hs]
        acts = [(g * _sigmoid(g) * u).astype(BF16) for g, u in gus]
        accs = [acc + _dot(a, wo_ref[c:c + FF_CHUNK, :]) for acc, a in zip(accs, acts)]
    if final:
        accs = [_rms(acc, gfin_ref[...]) for acc in accs]
    return accs


def _post_even_body(final, starts, out_starts, *refs):
    n = len(starts) - 1
    of_ref, ob_ref, ga_ref, n_ref, gain_ref, wm_ref, gf_ref, wi_ref, wo_ref, gfin_ref = refs[n:n + 10]
    x = _select_part(starts, refs[:n])
    x1s = []
    for sl in SLABS:
        o = of_ref[sl, :] + ob_ref[sl, :]
        parts = []
        for h in range(A_HEADS):
            oh = o[:, h * A_DK:(h + 1) * A_DK]
            parts.append(oh * lax.rsqrt(jnp.mean(oh * oh, axis=-1, keepdims=True) + EPS))
        on = jnp.concatenate(parts, axis=1) * gain_ref[...]
        g = ga_ref[sl, :]
        oa = (on * (g * _sigmoid(g))).astype(BF16)
        mix = _dot(oa, wm_ref[0:A_WIDTH, :]) + _dot(n_ref[sl, :], wm_ref[A_WIDTH:A_WIDTH + NA_WIDTH, :])
        x1s.append(x[sl, :] + mix)
    accs = _ffn_tail(x1s, gf_ref, wi_ref, wo_ref, gfin_ref, final)
    _store_parts(out_starts, refs[n + 10:], accs)


def _post_even(parts, o_f, o_b, pa, o_n, gain, w_mix, e, g_ffn, w_ffn_in, w_ffn_out, l, g_final, final, out_tokens):
    assert w_mix.shape[1:] == (A_WIDTH + NA_WIDTH, D_MODEL)
    assert w_ffn_in.shape[1:] == (D_MODEL, 2 * D_FF) and w_ffn_out.shape[1:] == (D_FF, D_MODEL) and D_FF % FF_CHUNK == 0
    starts = _tile_starts(parts)
    out_starts, out_specs, out_shapes = _out_specs_shapes(out_tokens)
    tok = lambda w, col=0: pl.BlockSpec((TM, w), lambda i: (i, col))
    return pl.pallas_call(
        functools.partial(_post_even_body, final, starts, out_starts),
        grid=(starts[-1],),
        in_specs=_part_specs(parts) + [tok(A_WIDTH), tok(A_WIDTH), tok(A_WIDTH, PA_G), tok(NA_WIDTH),
                                       _resident((1, A_WIDTH)), _layer_of(w_mix, e),
                                       _resident((1, D_MODEL)), _layer_of(w_ffn_in, l),
                                       _layer_of(w_ffn_out, l), _resident((1, D_MODEL))],
        out_specs=out_specs,
        out_shape=out_shapes,
        compiler_params=pltpu.CompilerParams(dimension_semantics=("arbitrary",),
                                             vmem_limit_bytes=VMEM_LIMIT),
        name="post_even",
    )(*parts, o_f, o_b, pa, o_n, gain, w_mix, g_ffn, w_ffn_in, w_ffn_out, g_final)


def _odd_body(final, out_starts, tbl_ref, x_ref, xp_ref, xn_ref, gm_ref, wc_ref, cw_ref, wco_ref, gf_ref,
              wi_ref, wo_ref, gfin_ref, *out_refs):
    step = pl.program_id(0)
    x = x_ref[...]
    gm = gm_ref[...]
    hs = [_rms(x[sl, :], gm).astype(BF16) for sl in SLABS]
    hh = _rms(jnp.concatenate([xp_ref[...], xn_ref[...]], axis=0), gm).astype(BF16)
    zin = [jnp.concatenate([hs[0], hh], axis=0)] + hs[1:]
    zs = [_dot(h, wc_ref[:, D_MODEL:2 * D_MODEL]) * _dot(h, wc_ref[:, 2 * D_MODEL:3 * D_MODEL]) for h in zin]
    zh = zs[0][TM // len(SLABS):, :]
    zs[0] = zs[0][:TM // len(SLABS), :]
    bgs = [_dot(h, wc_ref[:, 0:D_MODEL]) for h in hs]
    z_prev = jnp.where(tbl_ref[2, step] == 1, 0.0, zh[HALO - 1:HALO, :])
    z_next = jnp.where(tbl_ref[3, step] == 1, 0.0, zh[HALO:HALO + 1, :])
    rows = TM // len(SLABS)
    row = lax.broadcasted_iota(jnp.int32, (rows, 1), 0)
    x1s = []
    for k, (sl, z, bg) in enumerate(zip(SLABS, zs, bgs)):
        above = z_prev if k == 0 else zs[k - 1][rows - 1:rows, :]
        below = z_next if k == len(SLABS) - 1 else zs[k + 1][0:1, :]
        z_dn = jnp.where(row == 0, above, pltpu.roll(z, 1, 0))
        z_up = jnp.where(row == rows - 1, below, pltpu.roll(z, rows - 1, 0))
        conv = z_dn * cw_ref[0:1, :] + z * cw_ref[1:2, :] + z_up * cw_ref[2:3, :]
        x1s.append(x[sl, :] + _dot((bg * conv).astype(BF16), wco_ref[...]))
    accs = _ffn_tail(x1s, gf_ref, wi_ref, wo_ref, gfin_ref, final)
    _store_parts(out_starts, out_refs, accs)


def _odd_table(segs, t):
    rows = []
    starts = set()
    ends = set()
    for off, batch, seq in segs:
        for bi in range(batch):
            starts.add(off + bi * seq)
            ends.add(off + (bi + 1) * seq)
    for i in range(t // TM):
        t0 = i * TM
        first = int(t0 in starts)
        last = int(t0 + TM in ends)
        rows.append((max(t0 // HALO - 1, 0), min((t0 + TM) // HALO, t // HALO - 1), first, last))
    return np.asarray(rows, np.int32).T.copy()


def _odd(x, g_mix, w_conv_in, conv_w, w_conv_out, o, g_ffn, w_ffn_in, w_ffn_out, l, g_final, final, segs, out_tokens):
    t = x.shape[0]
    assert conv_w.shape[1:] == (CONV_W, D_MODEL) and w_conv_in.shape[1:] == (D_MODEL, CONV_W * D_MODEL)
    assert all(seq % TM == 0 for _, _, seq in segs)
    tbl = _odd_table(segs, t)
    out_starts, out_specs, out_shapes = _out_specs_shapes(out_tokens)
    res = lambda shape: pl.BlockSpec(shape, lambda i, tb: (0,) * len(shape), pipeline_mode=pl.Buffered(1))
    grid_spec = pltpu.PrefetchScalarGridSpec(
        num_scalar_prefetch=1,
        grid=(t // TM,),
        in_specs=[pl.BlockSpec((TM, D_MODEL), lambda i, tb: (i, 0)),
                  pl.BlockSpec((HALO, D_MODEL), lambda i, tb: (tb[0, i], 0)),
                  pl.BlockSpec((HALO, D_MODEL), lambda i, tb: (tb[1, i], 0)),
                  res((1, D_MODEL)), _layer_of(w_conv_in, o), _layer_of(conv_w, o),
                  _layer_of(w_conv_out, o), res((1, D_MODEL)), _layer_of(w_ffn_in, l),
                  _layer_of(w_ffn_out, l), res((1, D_MODEL))],
        out_specs=out_specs,
    )
    return pl.pallas_call(
        functools.partial(_odd_body, final, out_starts),
        grid_spec=grid_spec,
        out_shape=out_shapes,
        compiler_params=pltpu.CompilerParams(dimension_semantics=("arbitrary",),
                                             vmem_limit_bytes=VMEM_LIMIT),
        name="odd_layer",
    )(jnp.asarray(tbl), x, x, x, g_mix, w_conv_in, conv_w, w_conv_out, g_ffn, w_ffn_in, w_ffn_out, g_final)


def _trunk(xs, norm_mix, norm_ffn, norm_final, even_w_in, even_w_out, hgrn_lb_logits, hgrn_norm,
           na_rpb, conv_w_in, conv_w, conv_w_out, ffn_w_in, ffn_w_out):
    depth = norm_mix.shape[0]
    segs = []
    off = 0
    for a in xs:
        segs.append((off, a.shape[0], a.shape[1]))
        off += a.shape[0] * a.shape[1]
    parts = [a.reshape(-1, D_MODEL) for a in xs]
    part_tokens = [p.shape[0] for p in parts]
    total = off
    row = lambda v: v.reshape(1, -1).astype(F32)
    g_final = row(norm_final)
    w_fi, w_fo = ffn_w_in.astype(BF16), ffn_w_out.astype(BF16)
    w_ei, w_eo = even_w_in.astype(BF16), even_w_out.astype(BF16)
    w_ci, w_co, w_cw = conv_w_in.astype(BF16), conv_w_out.astype(BF16), conv_w.astype(F32)
    for l in range(depth):
        final = l == depth - 1
        out_tokens = part_tokens if final else [total]
        if l % 2 == 0:
            e = l // 2
            pa, pn = _in_even(parts, row(norm_mix[l]), w_ei, e)
            o_f, o_b = _hgrn(pa, hgrn_lb_logits.astype(F32), e, segs)
            o_n = _na(pn, na_rpb[e], segs)
            outs = _post_even(parts, o_f, o_b, pa, o_n, row(hgrn_norm[e]), w_eo, e,
                              row(norm_ffn[l]), w_fi, w_fo, l, g_final, final, out_tokens)
        else:
            o = l // 2
            x = parts[0] if len(parts) == 1 else jnp.concatenate(parts, axis=0)
            outs = _odd(x, row(norm_mix[l]), w_ci, w_cw, w_co, o, row(norm_ffn[l]), w_fi, w_fo, l, g_final, final,
                        segs, out_tokens)
        parts = list(outs)
    return tuple(p.reshape(a.shape) for p, a in zip(parts, xs))


def kernel(x_prompt, x_sample, norm_mix, norm_ffn, norm_final, even_w_in, even_w_out, hgrn_lb_logits,
           hgrn_norm, na_rpb, conv_w_in, conv_w, conv_w_out, ffn_w_in, ffn_w_out):
    return _trunk([x_prompt, x_sample], norm_mix, norm_ffn, norm_final, even_w_in, even_w_out,
                  hgrn_lb_logits, hgrn_norm, na_rpb, conv_w_in, conv_w, conv_w_out, ffn_w_in, ffn_w_out)
```
